```python
import jax, jax.numpy as jnp
from jax import lax
import numpy as np

D_MODEL = 1024
BATCH = 8
SEQ = 2048
DEPTH = 4
DEC_BATCH = 128
DEC_SEQ = 4
PAST_LEN = 16384
PAGE_SIZE = 128

N_MIXERS = 2
N_RET_LAYERS = (DEPTH + 1) // 2
N_DN_LAYERS = DEPTH // 2
RET_HEADS = 4
RET_DK = D_MODEL // RET_HEADS
RET_DV = 2 * RET_DK
RET_QW = RET_HEADS * RET_DK
RET_VW = RET_HEADS * RET_DV
RET_IN = 2 * RET_QW + 2 * RET_VW
RET_CHUNK = 128
DN_QK_HEADS = 8
DN_V_HEADS = 16
DN_DK = 128
DN_DV = 128
DN_QW = DN_QK_HEADS * DN_DK
DN_VW = DN_V_HEADS * DN_DV
DN_CONV_CH = 2 * DN_QW + DN_VW
DN_IN = DN_CONV_CH + DN_VW + 2 * DN_V_HEADS
DN_CONV_W = 4
DN_CHUNK = 64
N_EXPERTS = 16
N_GROUPS = 4
EXPERTS_PER_GROUP = N_EXPERTS // N_GROUPS
TOP_K = 2
GROUP_SCORE_K = 2
D_EXPERT = D_MODEL // 4
N_MOD = 6
ALPHA = (2.0 * DEPTH) ** 0.25
BETA_INIT = (8.0 * DEPTH) ** -0.25
LN_EPS = 1e-5
F32 = jnp.float32

kernel_name = "hybrid_retention_gdn_groupmoe_step"


def _layer_norm(x, g, b):
    xf = x.astype(F32)
    mu = xf.mean(-1, keepdims=True)
    var = jnp.square(xf - mu).mean(-1, keepdims=True)
    return ((xf - mu) * lax.rsqrt(var + LN_EPS) * g.astype(F32) + b.astype(F32)).astype(x.dtype)


def _rms_norm(x, g):
    xf = x.astype(F32)
    return xf * lax.rsqrt(jnp.mean(xf * xf, -1, keepdims=True) + 1e-6) * g.astype(F32)


def _l2norm(x):
    return x * lax.rsqrt(jnp.sum(x * x, -1, keepdims=True) + 1e-6)


def _rotary(x, pos0):
    L, d = x.shape[1], x.shape[-1]
    half = d // 2
    inv_freq = 10000.0 ** (-jnp.linspace(0.0, 1.0, half, dtype=F32))
    pos = (pos0 + jnp.arange(L)).astype(F32)
    ang = pos[:, None] * inv_freq[None, :]
    cos = jnp.cos(ang)[None, :, None, :]
    sin = jnp.sin(ang)[None, :, None, :]
    x1 = x[..., :half].astype(F32)
    x2 = x[..., half:].astype(F32)
    return jnp.concatenate([x1 * cos - x2 * sin, x2 * cos + x1 * sin], axis=-1)


def _to_chunks(t, n, c):
    B, L, H = t.shape[:3]
    t = t.reshape((B, n, c, H) + t.shape[3:])
    return jnp.moveaxis(t, (1, 3), (0, 2))


def _from_chunks(t):
    n, B, H, c = t.shape[:4]
    t = jnp.moveaxis(t, (0, 2), (1, 3))
    return t.reshape((B, n * c, H) + t.shape[4:])


def _retention_chunked(q, k, v, s0):
    L = q.shape[1]
    c = min(RET_CHUNK, L)
    n = L // c
    lg = jnp.log(1.0 - 2.0 ** (-5.0 - jnp.arange(RET_HEADS, dtype=F32)))
    idx = jnp.arange(c, dtype=F32)
    diff = idx[:, None] - idx[None, :]
    decay_intra = jnp.exp(jnp.where(diff[None] >= 0, diff[None] * lg[:, None, None], -jnp.inf))
    q_dec = jnp.exp((idx + 1.0)[None, :] * lg[:, None])
    k_dec = jnp.exp((c - 1.0 - idx)[None, :] * lg[:, None])
    s_dec = jnp.exp(c * lg)

    def step(s, xs):
        qi, ki, vi = xs
        att = jnp.einsum('bhid,bhjd->bhij', qi, ki) * decay_intra
        o = (jnp.einsum('bhij,bhjv->bhiv', att, vi)
             + jnp.einsum('bhid,bhdv->bhiv', qi * q_dec[..., None], s))
        s = s * s_dec[:, None, None] + jnp.einsum('bhjd,bhjv->bhdv', ki * k_dec[..., None], vi)
        return s, o

    s, o = lax.scan(step, s0, (_to_chunks(q, n, c), _to_chunks(k, n, c), _to_chunks(v, n, c)))
    return _from_chunks(o), s


def _retention_mixer(h, s0, pos0, w_in, gn_g, w_out):
    B, L, _ = h.shape
    p = h @ w_in
    q = p[..., :RET_QW].reshape(B, L, RET_HEADS, RET_DK)
    k = p[..., RET_QW:2 * RET_QW].reshape(B, L, RET_HEADS, RET_DK)
    v = p[..., 2 * RET_QW:2 * RET_QW + RET_VW].reshape(B, L, RET_HEADS, RET_DV).astype(F32)
    gate = p[..., 2 * RET_QW + RET_VW:]
    q = _rotary(q, pos0)
    k = _rotary(k, pos0) * (RET_DK ** -0.5)
    o, s = _retention_chunked(q, k, v, s0.astype(F32))
    mu = o.mean(-1, keepdims=True)
    var = jnp.square(o - mu).mean(-1, keepdims=True)
    o = (o - mu) * lax.rsqrt(var + LN_EPS) * gn_g.astype(F32)
    o = o.reshape(B, L, RET_VW).astype(h.dtype) * jax.nn.silu(gate)
    return o @ w_out, s.astype(s0.dtype)


def _causal_conv(x, buf, w):
    L = x.shape[1]
    xp = jnp.concatenate([buf.astype(x.dtype), x], axis=1)
    y = xp[:, 0:L] * w[0]
    for t in range(1, DN_CONV_W):
        y = y + xp[:, t:t + L] * w[t]
    return y, xp[:, L:]


def _gated_delta_chunked(q, k, v, g, beta, s0):
    L = q.shape[1]
    c = min(DN_CHUNK, L)
    n = L // c
    incl = jnp.tril(jnp.ones((c, c), dtype=bool))
    strict = jnp.tril(jnp.ones((c, c), dtype=bool), -1)
    eye = jnp.eye(c, dtype=F32)

    def step(s, xs):
        qi, ki, vi, gi, bi = xs
        G = jnp.cumsum(gi, axis=-1)
        decay = jnp.exp(jnp.where(incl, G[..., :, None] - G[..., None, :], -jnp.inf))
        kb = ki * bi[..., None]
        a = jnp.where(strict, jnp.einsum('bhid,bhjd->bhij', kb, ki) * decay, 0.0)
        rhs = jnp.concatenate([vi * bi[..., None], kb * jnp.exp(G)[..., None]], axis=-1)
        sol = lax.linalg.triangular_solve(eye + a, rhs, left_side=True, lower=True,
                                          unit_diagonal=True)
        u = sol[..., :DN_DV] - sol[..., DN_DV:] @ s
        att = jnp.einsum('bhid,bhjd->bhij', qi, ki) * decay
        o = (qi * jnp.exp(G)[..., None]) @ s + att @ u
        g_last = G[..., -1:]
        s = (s * jnp.exp(g_last)[..., None]
             + jnp.einsum('bhjd,bhjv->bhdv', ki * jnp.exp(g_last - G)[..., None], u))
        return s, o

    xs = (_to_chunks(q, n, c), _to_chunks(k, n, c), _to_chunks(v, n, c),
          _to_chunks(g, n, c), _to_chunks(beta, n, c))
    s, o = lax.scan(step, s0, xs)
    return _from_chunks(o), s


def _deltanet_mixer(h, s0, buf, w_in, conv_w, a_log, dt_bias, norm_g, w_out):
    B, L, _ = h.shape
    p = h @ w_in
    qkv, buf_new = _causal_conv(p[..., :DN_CONV_CH], buf, conv_w)
    qkv = jax.nn.silu(qkv).astype(F32)
    z = p[..., DN_CONV_CH:DN_CONV_CH + DN_VW]
    a = p[..., DN_CONV_CH + DN_VW:DN_CONV_CH + DN_VW + DN_V_HEADS].astype(F32)
    b = p[..., DN_CONV_CH + DN_VW + DN_V_HEADS:].astype(F32)
    rep = DN_V_HEADS // DN_QK_HEADS
    q = _l2norm(qkv[..., :DN_QW].reshape(B, L, DN_QK_HEADS, DN_DK))
    k = _l2norm(qkv[..., DN_QW:2 * DN_QW].reshape(B, L, DN_QK_HEADS, DN_DK))
    q = jnp.repeat(q, rep, axis=2) * (DN_DK ** -0.5)
    k = jnp.repeat(k, rep, axis=2)
    v = qkv[..., 2 * DN_QW:].reshape(B, L, DN_V_HEADS, DN_DV)
    g = -jnp.exp(a_log.astype(F32)) * jax.nn.softplus(a + dt_bias.astype(F32))
    beta = jax.nn.sigmoid(b)
    o, s = _gated_delta_chunked(q, k, v, g, beta, s0.astype(F32))
    o = _rms_norm(o, norm_g) * jax.nn.silu(z.astype(F32).reshape(B, L, DN_V_HEADS, DN_DV))
    return o.reshape(B, L, DN_VW).astype(h.dtype) @ w_out, s.astype(s0.dtype), buf_new.astype(buf.dtype)


def _moe(h, w_router, b_router, w_gu, w_down):
    B, L, D = h.shape
    t = h.reshape(B * L, D)
    probs = jax.nn.softmax((t @ w_router).astype(F32), axis=-1)
    sel = probs + b_router.astype(F32)
    grp_score = lax.top_k(sel.reshape(-1, N_GROUPS, EXPERTS_PER_GROUP), GROUP_SCORE_K)[0].sum(-1)
    gsel = jnp.argmax(grp_score, axis=-1)
    in_grp = (jnp.arange(N_EXPERTS) // EXPERTS_PER_GROUP)[None, :] == gsel[:, None]
    _, eidx = lax.top_k(jnp.where(in_grp, sel, -jnp.inf), TOP_K)
    w = jnp.take_along_axis(probs, eidx, axis=-1)
    w = w / w.sum(-1, keepdims=True)
    gates = jnp.sum(jax.nn.one_hot(eidx, N_EXPERTS, dtype=F32) * w[..., None], axis=1)
    gu = jnp.einsum('td,edf->tef', t, w_gu)
    act = jax.nn.silu(gu[..., :D_EXPERT]) * gu[..., D_EXPERT:]
    act = act * gates.astype(act.dtype)[..., None]
    y = jnp.einsum('tef,efd->td', act, w_down)
    return y.reshape(B, L, D)


def _trunk(x, c, pos0, s_ret, s_dn, s_conv, w_ada, b_ada, ln_g, ln_b, ret_w_in, ret_gn_g, ret_w_out,
           dn_w_in, dn_conv_w, dn_a_log, dn_dt_bias, dn_norm_g, dn_w_out, router_w, router_b,
           moe_w_gu, moe_w_down):
    B = x.shape[0]
    new_ret, new_dn, new_conv = [], [], []
    cs = jax.nn.silu(c)
    for i in range(DEPTH):
        mod = (cs @ w_ada[i] + b_ada[i]).reshape(B, N_MOD, 1, D_MODEL)
        shift_m, scale_m, gate_m, shift_f, scale_f, gate_f = (mod[:, j] for j in range(N_MOD))
        h = x * (1.0 + scale_m) + shift_m
        j = i // N_MIXERS
        if i % N_MIXERS == 0:
            y, s = _retention_mixer(h, s_ret[j], pos0, ret_w_in[j], ret_gn_g[j], ret_w_out[j])
            new_ret.append(s)
        else:
            y, s, buf = _deltanet_mixer(h, s_dn[j], s_conv[j], dn_w_in[j], dn_conv_w[j], dn_a_log[j],
                                        dn_dt_bias[j], dn_norm_g[j], dn_w_out[j])
            new_dn.append(s)
            new_conv.append(buf)
        x = _layer_norm(ALPHA * x + (1.0 + gate_m) * y, ln_g[i, 0], ln_b[i, 0])
        h = x * (1.0 + scale_f) + shift_f
        y = _moe(h, router_w, router_b, moe_w_gu[i], moe_w_down[i])
        x = _layer_norm(ALPHA * x + (1.0 + gate_f) * y, ln_g[i, 1], ln_b[i, 1])
    return x, jnp.stack(new_ret), jnp.stack(new_dn), jnp.stack(new_conv)


def setup_inputs(seed: int = 0) -> dict:
    key = jax.random.key(seed)
    ks = jax.random.split(key, 32)

    def nrm(k, shape, scale):
        return jax.random.normal(k, shape, F32) * scale

    ret_cols = jnp.concatenate([jnp.ones((2 * RET_QW,), F32), jnp.full((RET_VW,), BETA_INIT, F32),
                                jnp.ones((RET_VW,), F32)])
    dn_cols = jnp.concatenate([jnp.ones((2 * DN_QW,), F32), jnp.full((DN_VW,), BETA_INIT, F32),
                               jnp.ones((DN_VW + 2 * DN_V_HEADS,), F32)])
    dt = jnp.exp(jax.random.uniform(ks[17], (N_DN_LAYERS, DN_V_HEADS), F32,
                                    jnp.log(1e-3), jnp.log(1e-1)))
    return {
        "x_prompt": nrm(ks[0], (BATCH, SEQ, D_MODEL), 1.0),
        "x_sample": nrm(ks[1], (DEC_BATCH, DEC_SEQ, D_MODEL), 1.0),
        "state_ret": nrm(ks[2], (N_RET_LAYERS, DEC_BATCH, RET_HEADS, RET_DK, RET_DV), 0.1),
        "state_dn": nrm(ks[3], (N_DN_LAYERS, DEC_BATCH, DN_V_HEADS, DN_DK, DN_DV), 0.1),
        "state_conv": nrm(ks[4], (N_DN_LAYERS, DEC_BATCH, DN_CONV_W - 1, DN_CONV_CH), 1.0),
        "c_prompt": nrm(ks[5], (BATCH, D_MODEL), 1.0),
        "c_sample": nrm(ks[6], (DEC_BATCH, D_MODEL), 1.0),
        "w_ada": nrm(ks[7], (DEPTH, D_MODEL, N_MOD * D_MODEL), 0.1 * D_MODEL ** -0.5),
        "b_ada": nrm(ks[8], (DEPTH, N_MOD * D_MODEL), 0.01),
        "ln_g": 1.0 + nrm(ks[9], (DEPTH, 2, D_MODEL), 0.02),
        "ln_b": nrm(ks[10], (DEPTH, 2, D_MODEL), 0.02),
        "ret_w_in": nrm(ks[11], (N_RET_LAYERS, D_MODEL, RET_IN), D_MODEL ** -0.5) * ret_cols,
        "ret_gn_g": 1.0 + nrm(ks[12], (N_RET_LAYERS, RET_DV), 0.02),
        "ret_w_out": nrm(ks[13], (N_RET_LAYERS, RET_VW, D_MODEL), BETA_INIT * RET_VW ** -0.5),
        "dn_w_in": nrm(ks[14], (N_DN_LAYERS, D_MODEL, DN_IN), D_MODEL ** -0.5) * dn_cols,
        "dn_conv_w": nrm(ks[15], (N_DN_LAYERS, DN_CONV_W, DN_CONV_CH), DN_CONV_W ** -0.5),
        "dn_a_log": jnp.log(jax.random.uniform(ks[16], (N_DN_LAYERS, DN_V_HEADS), F32, 1.0, 16.0)),
        "dn_dt_bias": dt + jnp.log(-jnp.expm1(-dt)),
        "dn_norm_g": 1.0 + nrm(ks[18], (N_DN_LAYERS, DN_DV), 0.02),
        "dn_w_out": nrm(ks[19], (N_DN_LAYERS, DN_VW, D_MODEL), BETA_INIT * DN_VW ** -0.5),
        "router_w": nrm(ks[20], (D_MODEL, N_EXPERTS), D_MODEL ** -0.5),
        "router_b": nrm(ks[21], (N_EXPERTS,), 0.01),
        "moe_w_gu": nrm(ks[22], (DEPTH, N_EXPERTS, D_MODEL, 2 * D_EXPERT), D_MODEL ** -0.5),
        "moe_w_down": nrm(ks[23], (DEPTH, N_EXPERTS, D_EXPERT, D_MODEL), BETA_INIT * D_EXPERT ** -0.5),
    }


def reference(x_prompt, x_sample, state_ret, state_dn, state_conv, c_prompt, c_sample, w_ada, b_ada,
              ln_g, ln_b, ret_w_in, ret_gn_g, ret_w_out, dn_w_in, dn_conv_w, dn_a_log, dn_dt_bias,
              dn_norm_g, dn_w_out, router_w, router_b, moe_w_gu, moe_w_down):
    b_p = x_prompt.shape[0]
    zr = jnp.zeros((N_RET_LAYERS, b_p) + state_ret.shape[2:], state_ret.dtype)
    zd = jnp.zeros((N_DN_LAYERS, b_p) + state_dn.shape[2:], state_dn.dtype)
    zc = jnp.zeros((N_DN_LAYERS, b_p) + state_conv.shape[2:], state_conv.dtype)
    y_prompt, ret_p, dn_p, conv_p = _trunk(
        x_prompt, c_prompt, 0, zr, zd, zc, w_ada, b_ada, ln_g, ln_b, ret_w_in, ret_gn_g, ret_w_out,
        dn_w_in, dn_conv_w, dn_a_log, dn_dt_bias, dn_norm_g, dn_w_out, router_w, router_b,
        moe_w_gu, moe_w_down)
    y_sample, ret_s, dn_s, conv_s = _trunk(
        x_sample, c_sample, PAST_LEN, state_ret, state_dn, state_conv, w_ada, b_ada, ln_g, ln_b,
        ret_w_in, ret_gn_g, ret_w_out, dn_w_in, dn_conv_w, dn_a_log, dn_dt_bias, dn_norm_g, dn_w_out,
        router_w, router_b, moe_w_gu, moe_w_down)
    return (y_prompt, y_sample, ret_p, ret_s, dn_p, dn_s, conv_p, conv_s)
```

```python
import functools
from typing import NamedTuple

import jax
import jax.numpy as jnp
from jax import lax
from jax.experimental import pallas as pl
from jax.experimental.pallas import tpu as pltpu

F32 = jnp.float32
BF16 = jnp.bfloat16

D_MODEL = 1024
DEPTH = 4
PAST_LEN = 16384
RET_HEADS = 4
RET_DK = 256
RET_DV = 512
RET_QW = RET_HEADS * RET_DK
RET_VW = RET_HEADS * RET_DV
RET_CHUNK = 128
DN_QK_HEADS = 8
DN_V_HEADS = 16
DN_DK = 128
DN_DV = 128
DN_QW = DN_QK_HEADS * DN_DK
DN_VW = DN_V_HEADS * DN_DV
DN_CONV_CH = 2 * DN_QW + DN_VW
DN_CONV_W = 4
DN_CHUNK = 64
DN_MAIN = DN_CONV_CH + DN_VW
N_EXPERTS = 16
N_GROUPS = 4
EXPERTS_PER_GROUP = N_EXPERTS // N_GROUPS
D_EXPERT = 256
N_MOD = 6
ALPHA = (2.0 * DEPTH) ** 0.25
LN_EPS = 1e-5

LANES = 128
SUBLANES = 8
MXU_DIM = 256
VMEM_LIMIT = 56 * 1024 * 1024


class _Geo(NamedTuple):
    B: int
    L: int
    l_true: int
    pos0: int
    tm: int
    per_token: bool
    act_dtype: object

    @property
    def T(self):
        return self.B * self.L


def _cparams(sem):
    return pltpu.CompilerParams(dimension_semantics=sem, vmem_limit_bytes=VMEM_LIMIT)


def _mod_spec(geo, chunk):
    if geo.per_token:
        return pl.BlockSpec((geo.tm, D_MODEL), lambda *g: (g[0], chunk))
    tpr = geo.L // geo.tm
    return pl.BlockSpec((None, 1, D_MODEL), lambda *g: (g[0] // tpr, 0, chunk))


def _silu(x):
    return x * jax.nn.sigmoid(x)


def _dot(a, b):
    return jnp.dot(a, b, preferred_element_type=F32)


def _dot_nt(a, b):
    return lax.dot_general(a, b, (((1,), (1,)), ((), ())), preferred_element_type=F32)


def _dot_tn(a, b):
    return lax.dot_general(a, b, (((0,), (0,)), ((), ())), preferred_element_type=F32)


def _layer_norm(z, g, b):
    mu = jnp.mean(z, axis=-1, keepdims=True)
    zc = z - mu
    var = jnp.mean(zc * zc, axis=-1, keepdims=True)
    return zc * lax.rsqrt(var + LN_EPS) * g + b


def _ada_kernel(c_ref, w_ref, b_ref, o_ref):
    cs = _silu(c_ref[...]).astype(BF16)
    o_ref[...] = _dot(cs, w_ref[...].astype(BF16)) + b_ref[...]


def _ada(c_all, w_ada, b_ada):
    R = c_all.shape[0]
    tn = 1024
    n_out = N_MOD * D_MODEL
    return pl.pallas_call(
        _ada_kernel,
        grid=(DEPTH, n_out // tn),
        in_specs=[
            pl.BlockSpec((R, D_MODEL), lambda l, j: (0, 0)),
            pl.BlockSpec((None, D_MODEL, tn), lambda l, j: (l, 0, j)),
            pl.BlockSpec((None, 1, tn), lambda l, j: (l, 0, j)),
        ],
        out_specs=pl.BlockSpec((None, R, tn), lambda l, j: (l, 0, j)),
        out_shape=jax.ShapeDtypeStruct((DEPTH, R, n_out), F32),
        compiler_params=_cparams(("parallel", "parallel")),
        name="ada_mod",
    )(c_all, w_ada, b_ada.reshape(DEPTH, 1, n_out))


def _inproj_kernel(*refs, has_tail):
    if has_tail:
        x_ref, sc_ref, sh_ref, w_ref, wt_ref, o_ref, ot_ref, h_scr = refs
    else:
        x_ref, sc_ref, sh_ref, w_ref, o_ref, h_scr = refs

    @pl.when(pl.program_id(1) == 0)
    def _():
        h = x_ref[...] * (1.0 + sc_ref[...]) + sh_ref[...]
        hb = h.astype(BF16)
        h_scr[...] = hb
        if has_tail:
            ot_ref[...] = _dot(hb, wt_ref[...])

    o_ref[...] = _dot(h_scr[...], w_ref[...]).astype(o_ref.dtype)


def _inproj(geo, x, mod, w, w_tail, tn):
    T, tm = geo.T, geo.tm
    N = w.shape[1]
    has_tail = w_tail is not None
    in_specs = [
        pl.BlockSpec((tm, D_MODEL), lambda i, j: (i, 0)),
        _mod_spec(geo, 1),
        _mod_spec(geo, 0),
        pl.BlockSpec((D_MODEL, tn), lambda i, j: (0, j)),
    ]
    args = [x, mod, mod, w]
    out_specs = [pl.BlockSpec((tm, tn), lambda i, j: (i, j))]
    out_shape = [jax.ShapeDtypeStruct((T, N), geo.act_dtype)]
    if has_tail:
        in_specs.append(pl.BlockSpec((D_MODEL, LANES), lambda i, j: (0, 0)))
        args.append(w_tail)
        out_specs.append(pl.BlockSpec((tm, LANES), lambda i, j: (i, 0)))
        out_shape.append(jax.ShapeDtypeStruct((T, LANES), F32))
    res = pl.pallas_call(
        functools.partial(_inproj_kernel, has_tail=has_tail),
        grid=(T // tm, N // tn),
        in_specs=in_specs,
        out_specs=out_specs,
        out_shape=out_shape,
        scratch_shapes=[pltpu.VMEM((tm, D_MODEL), BF16)],
        compiler_params=_cparams(("parallel", "arbitrary")),
        name="in_proj",
    )(*args)
    return res if has_tail else (res[0], None)


def _pad_rows(x, rows):
    if x.shape[0] == rows:
        return x
    return jnp.concatenate([x, jnp.zeros((rows - x.shape[0], x.shape[1]), x.dtype)], axis=0)


def _ret_kernel(*refs, rows, nchunk, has_state):
    if has_state:
        (q_ref, k_ref, v_ref, g_ref, cos_ref, sin_ref, dintra_ref, qdec_ref, kdec_ref, sdec_ref, gn_ref,
         s0_ref, o_ref, s_out_ref, s_scr) = refs
    else:
        (q_ref, k_ref, v_ref, g_ref, cos_ref, sin_ref, dintra_ref, qdec_ref, kdec_ref, sdec_ref, gn_ref,
         o_ref, s_out_ref, s_scr) = refs
    i = pl.program_id(1)
    c = RET_CHUNK
    half = RET_DK // 2

    @pl.when(i == 0)
    def _():
        if has_state:
            s_scr[...] = s0_ref[...]
        else:
            s_scr[...] = jnp.zeros_like(s_scr)

    def rot(ref, r, h, cos, sin):
        x1 = ref[r, h * RET_DK:h * RET_DK + half].astype(F32)
        x2 = ref[r, h * RET_DK + half:(h + 1) * RET_DK].astype(F32)
        return jnp.concatenate([x1 * cos - x2 * sin, x2 * cos + x1 * sin], axis=1)

    for ci in range(nchunk):
        r = slice(ci * rows, (ci + 1) * rows)
        cos = cos_ref[r, :]
        sin = sin_ref[r, :]
        for h in range(RET_HEADS):
            vs = slice(h * RET_DV, (h + 1) * RET_DV)
            q = _pad_rows(rot(q_ref, r, h, cos, sin), c)
            k = _pad_rows(rot(k_ref, r, h, cos, sin), c) * (RET_DK ** -0.5)
            if rows == c:
                v = v_ref[r, vs].astype(BF16)
            else:
                v = _pad_rows(v_ref[r, vs].astype(F32), c).astype(BF16)
            s = s_scr[h]
            att = _dot_nt(q.astype(BF16), k.astype(BF16)) * dintra_ref[h]
            o = _dot(att.astype(BF16), v) + _dot((q * qdec_ref[h]).astype(BF16), s.astype(BF16))
            s_scr[h] = s * sdec_ref[h] + _dot_tn((k * kdec_ref[h]).astype(BF16), v)
            o = o[:rows]
            mu = jnp.mean(o, axis=-1, keepdims=True)
            oc = o - mu
            var = jnp.mean(oc * oc, axis=-1, keepdims=True)
            on = oc * lax.rsqrt(var + LN_EPS) * gn_ref[...]
            gate = g_ref[r, vs].astype(F32)
            o_ref[r, vs] = (on * _silu(gate)).astype(o_ref.dtype)

    @pl.when(i == pl.num_programs(1) - 1)
    def _():
        s_out_ref[...] = s_scr[...]


def _ret_tables(geo):
    c_true = min(RET_CHUNK, geo.l_true)
    half = RET_DK // 2
    inv_freq = 10000.0 ** (-jnp.linspace(0.0, 1.0, half, dtype=F32))
    pos = (geo.pos0 + jnp.arange(geo.L)).astype(F32)
    ang = pos[:, None] * inv_freq[None, :]
    lg = jnp.log(1.0 - 2.0 ** (-5.0 - jnp.arange(RET_HEADS, dtype=F32)))
    idx = jnp.arange(c_true, dtype=F32)
    diff = idx[:, None] - idx[None, :]
    dintra = jnp.exp(jnp.where(diff[None] >= 0, diff[None] * lg[:, None, None], -jnp.inf))
    qdec = jnp.exp((idx + 1.0)[None, :] * lg[:, None])
    kdec = jnp.exp((c_true - 1.0 - idx)[None, :] * lg[:, None])
    sdec = jnp.exp(c_true * lg)
    pad = RET_CHUNK - c_true
    dintra = jnp.pad(dintra, ((0, 0), (0, pad), (0, pad)))
    qdec = jnp.pad(qdec, ((0, 0), (0, pad)))[..., None]
    kdec = jnp.pad(kdec, ((0, 0), (0, pad)))[..., None]
    sdec = jnp.broadcast_to(sdec[:, None, None], (RET_HEADS, 1, RET_DV))
    return jnp.cos(ang), jnp.sin(ang), dintra, qdec, kdec, sdec


def _retention(geo, p, tables, gn_g, s_all, layer_j, tl):
    B, L = geo.B, geo.L
    nblk = L // tl
    rows = min(tl, RET_CHUNK)
    nchunk = tl // rows
    has_state = s_all is not None
    cos, sin, dintra, qdec, kdec, sdec = tables
    rb = lambda b, i: b * nblk + i
    full3 = lambda b, i: (0, 0, 0)
    in_specs = [
        pl.BlockSpec((tl, RET_QW), lambda b, i: (rb(b, i), 0)),
        pl.BlockSpec((tl, RET_QW), lambda b, i: (rb(b, i), 1)),
        pl.BlockSpec((tl, RET_VW), lambda b, i: (rb(b, i), 1)),
        pl.BlockSpec((tl, RET_VW), lambda b, i: (rb(b, i), 2)),
        pl.BlockSpec((tl, RET_DK // 2), lambda b, i: (i, 0)),
        pl.BlockSpec((tl, RET_DK // 2), lambda b, i: (i, 0)),
        pl.BlockSpec((RET_HEADS, RET_CHUNK, RET_CHUNK), full3),
        pl.BlockSpec((RET_HEADS, RET_CHUNK, 1), full3),
        pl.BlockSpec((RET_HEADS, RET_CHUNK, 1), full3),
        pl.BlockSpec((RET_HEADS, 1, RET_DV), full3),
        pl.BlockSpec((1, RET_DV), lambda b, i: (0, 0)),
    ]
    args = [p, p, p, p, cos, sin, dintra, qdec, kdec, sdec, gn_g.reshape(1, RET_DV)]
    if has_state:
        in_specs.append(pl.BlockSpec((None, None, RET_HEADS, RET_DK, RET_DV),
                                     lambda b, i: (layer_j, b, 0, 0, 0)))
        args.append(s_all)
    o, s = pl.pallas_call(
        functools.partial(_ret_kernel, rows=rows, nchunk=nchunk, has_state=has_state),
        grid=(B, nblk),
        in_specs=in_specs,
        out_specs=[
            pl.BlockSpec((tl, RET_VW), lambda b, i: (rb(b, i), 0)),
            pl.BlockSpec((None, RET_HEADS, RET_DK, RET_DV), lambda b, i: (b, 0, 0, 0)),
        ],
        out_shape=[
            jax.ShapeDtypeStruct((geo.T, RET_VW), geo.act_dtype),
            jax.ShapeDtypeStruct((B, RET_HEADS, RET_DK, RET_DV), F32),
        ],
        scratch_shapes=[pltpu.VMEM((RET_HEADS, RET_DK, RET_DV), F32)],
        compiler_params=_cparams(("parallel", "arbitrary")),
        name="retention",
    )(*args)
    return o, s


DN_QUAD = MXU_DIM // DN_CHUNK
DN_CONV_PAD = SUBLANES


def _block_diag(x_cat, n, mask):
    return jnp.where(mask, jnp.concatenate([x_cat] * n, axis=0), 0.0)


def _dn_kernel(*refs, rows, nchunk, l_last, has_state):
    if has_state:
        (qkv_ref, z_ref, ab_ref, cw_ref, alog_ref, dtb_ref, ng_ref, s0_ref, buf_ref,
         o_ref, s_out_ref, buf_out_ref, s_scr, xp_scr, y_scr) = refs
    else:
        (qkv_ref, z_ref, ab_ref, cw_ref, alog_ref, dtb_ref, ng_ref,
         o_ref, s_out_ref, buf_out_ref, s_scr, xp_scr, y_scr) = refs
    i = pl.program_id(1)
    c = DN_CHUNK
    tl = rows * nchunk
    npair = DN_V_HEADS // 2
    W = DN_CONV_W
    P0 = DN_CONV_PAD

    @pl.when(i == 0)
    def _():
        if has_state:
            for pr in range(npair):
                s_scr[pr] = jnp.concatenate([s0_ref[2 * pr], s0_ref[2 * pr + 1]], axis=1)
            xp_scr[P0 - (W - 1):P0, :] = buf_ref[...]
        else:
            s_scr[...] = jnp.zeros_like(s_scr)
            xp_scr[0:P0, :] = jnp.zeros((P0, DN_CONV_CH), F32)

    xp_scr[P0:P0 + tl, :] = qkv_ref[...].astype(F32)
    y = xp_scr[P0 - (W - 1):P0 - (W - 1) + tl, :] * cw_ref[0:1, :]
    for t in range(1, W):
        y = y + xp_scr[P0 - (W - 1) + t:P0 - (W - 1) + t + tl, :] * cw_ref[t:t + 1, :]
    y_scr[0:tl, :] = _silu(y)
    if rows < c:
        y_scr[tl:, :] = jnp.zeros((y_scr.shape[0] - tl, DN_CONV_CH), F32)

    @pl.when(i == pl.num_programs(1) - 1)
    def _():
        buf_out_ref[...] = xp_scr[P0 + l_last - (W - 1):P0 + l_last, :]

    xp_scr[P0 - (W - 1):P0, :] = xp_scr[P0 + tl - (W - 1):P0 + tl, :]

    ri = lax.broadcasted_iota(jnp.int32, (c, c), 0)
    ci_ = lax.broadcasted_iota(jnp.int32, (c, c), 1)
    incl = ri >= ci_
    strict = ri > ci_
    eye = ri == ci_
    tri_f = incl.astype(F32)
    eye_cat = jnp.concatenate([eye.astype(F32)] * DN_QUAD, axis=1)
    bi = lax.broadcasted_iota(jnp.int32, (MXU_DIM, MXU_DIM), 0) // c
    bj = lax.broadcasted_iota(jnp.int32, (MXU_DIM, MXU_DIM), 1) // c
    bd_mask = bi == bj
    lane2 = lax.broadcasted_iota(jnp.int32, (DN_DK, 2 * DN_DV), 1)
    left = lane2 < DN_DV
    row_ok = lax.broadcasted_iota(jnp.int32, (c, 1), 0) < l_last if rows < c else None
    n_levels = c.bit_length() - 1

    for ck in range(nchunk):
        r0 = ck * c if rows == c else 0
        r = slice(r0, r0 + c)
        ab = _pad_rows(ab_ref[ck * rows:(ck + 1) * rows, :], c)
        sp = ab + dtb_ref[...]
        g_full = -jnp.exp(alog_ref[...]) * (jnp.maximum(sp, 0.0) + jnp.log1p(jnp.exp(-jnp.abs(sp))))
        beta_full = jax.nn.sigmoid(ab)
        if row_ok is not None:
            g_full = jnp.where(row_ok, g_full, 0.0)
            beta_full = jnp.where(row_ok, beta_full, 0.0)
        G_full = jnp.dot(tri_f, g_full, preferred_element_type=F32, precision=lax.Precision.HIGHEST)

        for quad in range(DN_V_HEADS // DN_QUAD):
            kn, qn, KK, QK = {}, {}, {}, {}
            for j in (2 * quad, 2 * quad + 1):
                qh = y_scr[r, j * DN_DK:(j + 1) * DN_DK]
                kh = y_scr[r, DN_QW + j * DN_DK:DN_QW + (j + 1) * DN_DK]
                qn[j] = qh * lax.rsqrt(jnp.sum(qh * qh, axis=-1, keepdims=True) + 1e-6) * (DN_DK ** -0.5)
                kn[j] = kh * lax.rsqrt(jnp.sum(kh * kh, axis=-1, keepdims=True) + 1e-6)
                kb16 = kn[j].astype(BF16)
                kq = _dot_nt(jnp.concatenate([kb16, qn[j].astype(BF16)], axis=0), kb16)
                KK[j], QK[j] = kq[:c], kq[c:]

            heads = [DN_QUAD * quad + t for t in range(DN_QUAD)]
            A, att, eG, Gc, bc, rhs = {}, {}, {}, {}, {}, {}
            for h in heads:
                j = h // 2
                Gc[h] = G_full[:, h:h + 1]
                bc[h] = beta_full[:, DN_V_HEADS + h:DN_V_HEADS + h + 1]
                Gr = jnp.sum(jnp.where(eye, Gc[h], 0.0), axis=0, keepdims=True)
                dec = jnp.exp(jnp.where(incl, Gc[h] - Gr, -jnp.inf))
                A[h] = jnp.where(strict, bc[h] * KK[j] * dec, 0.0)
                att[h] = QK[j] * dec
                eG[h] = jnp.exp(Gc[h])
                vh = y_scr[r, 2 * DN_QW + h * DN_DV:2 * DN_QW + (h + 1) * DN_DV]
                rhs[h] = jnp.concatenate([vh * bc[h], kn[j] * (bc[h] * eG[h])], axis=1)

            X = jnp.concatenate([A[h] for h in heads], axis=1)
            Pm = eye_cat - X
            Xw = _block_diag(X, DN_QUAD, bd_mask).astype(BF16)
            X = _dot(X.astype(BF16), Xw)
            for lvl in range(1, n_levels):
                Xw = _block_diag(X, DN_QUAD, bd_mask).astype(BF16)
                if lvl < n_levels - 1:
                    both = _dot(jnp.concatenate([X, Pm], axis=0).astype(BF16), Xw)
                    X = both[:c]
                    Pm = Pm + both[c:]
                else:
                    Pm = Pm + _dot(Pm.astype(BF16), Xw)
            Tbd = _block_diag(Pm, DN_QUAD, bd_mask).astype(BF16)
            sol = _dot(Tbd, jnp.concatenate([rhs[h] for h in heads], axis=0).astype(BF16))

            u, oS = {}, {}
            for pp in range(2):
                h0, h1 = heads[2 * pp], heads[2 * pp + 1]
                j = h0 // 2
                pr = h0 // 2
                S = s_scr[pr]
                Sbd = jnp.concatenate([jnp.where(left, S, 0.0), jnp.where(left, 0.0, S)], axis=0).astype(BF16)
                w_cat = jnp.concatenate([sol[c * (2 * pp):c * (2 * pp + 1), DN_DV:],
                                         sol[c * (2 * pp + 1):c * (2 * pp + 2), DN_DV:]], axis=1)
                qe_cat = jnp.concatenate([qn[j] * eG[h0], qn[j] * eG[h1]], axis=1)
                R = _dot(jnp.concatenate([w_cat, qe_cat], axis=0).astype(BF16), Sbd)
                u[h0] = sol[c * (2 * pp):c * (2 * pp + 1), :DN_DV] - R[:c, :DN_DV]
                u[h1] = sol[c * (2 * pp + 1):c * (2 * pp + 2), :DN_DV] - R[:c, DN_DV:]
                oS[h0] = R[c:, :DN_DV]
                oS[h1] = R[c:, DN_DV:]

            att_bd = _block_diag(jnp.concatenate([att[h] for h in heads], axis=1), DN_QUAD, bd_mask)
            o_intra = _dot(att_bd.astype(BF16),
                           jnp.concatenate([u[h] for h in heads], axis=0).astype(BF16))

            for pp in range(2):
                h0, h1 = heads[2 * pp], heads[2 * pp + 1]
                j = h0 // 2
                pr = h0 // 2
                gl0 = Gc[h0][c - 1:c, :]
                gl1 = Gc[h1][c - 1:c, :]
                du = jnp.concatenate([u[h0] * jnp.exp(gl0 - Gc[h0]), u[h1] * jnp.exp(gl1 - Gc[h1])], axis=1)
                sdec = jnp.where(left[0:1, :], jnp.exp(gl0), jnp.exp(gl1))
                s_scr[pr] = s_scr[pr] * sdec + _dot_tn(kn[j].astype(BF16), du.astype(BF16))

            for t, h in enumerate(heads):
                o = (oS[h] + o_intra[c * t:c * (t + 1)])[:rows]
                o = o * lax.rsqrt(jnp.mean(o * o, axis=-1, keepdims=True) + 1e-6) * ng_ref[...]
                hs = slice(h * DN_DV, (h + 1) * DN_DV)
                zr = slice(ck * rows, (ck + 1) * rows)
                z = z_ref[zr, hs].astype(F32)
                o_ref[zr, hs] = (o * _silu(z)).astype(o_ref.dtype)

    @pl.when(i == pl.num_programs(1) - 1)
    def _():
        for pr in range(npair):
            s_out_ref[2 * pr] = s_scr[pr][:, :DN_DV]
            s_out_ref[2 * pr + 1] = s_scr[pr][:, DN_DV:]


def _deltanet(geo, p, ab, conv_w, a_log, dt_bias, norm_g, s_all, buf_all, layer_j, tl):
    B, L = geo.B, geo.L
    nblk = L // tl
    rows = min(tl, DN_CHUNK)
    nchunk = tl // rows
    l_last = geo.l_true - (nblk - 1) * tl
    has_state = s_all is not None
    rb = lambda b, i: b * nblk + i
    pad16 = lambda v: jnp.pad(v.reshape(1, DN_V_HEADS), ((0, 0), (0, LANES - DN_V_HEADS)))
    in_specs = [
        pl.BlockSpec((tl, DN_CONV_CH), lambda b, i: (rb(b, i), 0)),
        pl.BlockSpec((tl, DN_VW), lambda b, i: (rb(b, i), DN_CONV_CH // DN_VW)),
        pl.BlockSpec((tl, LANES), lambda b, i: (rb(b, i), 0)),
        pl.BlockSpec((DN_CONV_W, DN_CONV_CH), lambda b, i: (0, 0)),
        pl.BlockSpec((1, LANES), lambda b, i: (0, 0)),
        pl.BlockSpec((1, LANES), lambda b, i: (0, 0)),
        pl.BlockSpec((1, DN_DV), lambda b, i: (0, 0)),
    ]
    args = [p, p, ab, conv_w, pad16(a_log), pad16(dt_bias), norm_g.reshape(1, DN_DV)]
    if has_state:
        in_specs += [
            pl.BlockSpec((None, None, DN_V_HEADS, DN_DK, DN_DV), lambda b, i: (layer_j, b, 0, 0, 0)),
            pl.BlockSpec((None, None, DN_CONV_W - 1, DN_CONV_CH), lambda b, i: (layer_j, b, 0, 0)),
        ]
        args += [s_all, buf_all]
    y_rows = max(tl, DN_CHUNK)
    o, s, buf = pl.pallas_call(
        functools.partial(_dn_kernel, rows=rows, nchunk=nchunk, l_last=l_last, has_state=has_state),
        grid=(B, nblk),
        in_specs=in_specs,
        out_specs=[
            pl.BlockSpec((tl, DN_VW), lambda b, i: (rb(b, i), 0)),
            pl.BlockSpec((None, DN_V_HEADS, DN_DK, DN_DV), lambda b, i: (b, 0, 0, 0)),
            pl.BlockSpec((None, DN_CONV_W - 1, DN_CONV_CH), lambda b, i: (b, 0, 0)),
        ],
        out_shape=[
            jax.ShapeDtypeStruct((geo.T, DN_VW), geo.act_dtype),
            jax.ShapeDtypeStruct((B, DN_V_HEADS, DN_DK, DN_DV), F32),
            jax.ShapeDtypeStruct((B, DN_CONV_W - 1, DN_CONV_CH), F32),
        ],
        scratch_shapes=[
            pltpu.VMEM((DN_V_HEADS // 2, DN_DK, 2 * DN_DV), F32),
            pltpu.VMEM((DN_CONV_PAD + tl, DN_CONV_CH), F32),
            pltpu.VMEM((y_rows, DN_CONV_CH), F32),
        ],
        compiler_params=_cparams(("parallel", "arbitrary")),
        name="deltanet",
    )(*args)
    return o, s, buf


def _outproj_kernel(o_ref, w_ref, x_ref, gate_ref, g_ref, b_ref, out_ref):
    y = _dot(o_ref[...].astype(BF16), w_ref[...])
    z = ALPHA * x_ref[...] + (1.0 + gate_ref[...]) * y
    out_ref[...] = _layer_norm(z, g_ref[...], b_ref[...])


def _outproj_ln(geo, o, w_out, x, mod, ln_g, ln_b):
    T, tm = geo.T, geo.tm
    K = w_out.shape[0]
    vec = pl.BlockSpec((1, D_MODEL), lambda i: (0, 0))
    return pl.pallas_call(
        _outproj_kernel,
        grid=(T // tm,),
        in_specs=[
            pl.BlockSpec((tm, K), lambda i: (i, 0)),
            pl.BlockSpec((K, D_MODEL), lambda i: (0, 0)),
            pl.BlockSpec((tm, D_MODEL), lambda i: (i, 0)),
            _mod_spec(geo, 2),
            vec, vec,
        ],
        out_specs=pl.BlockSpec((tm, D_MODEL), lambda i: (i, 0)),
        out_shape=jax.ShapeDtypeStruct((T, D_MODEL), F32),
        compiler_params=_cparams(("parallel",)),
        name="out_proj_ln",
    )(o, w_out, x, mod, ln_g.reshape(1, D_MODEL), ln_b.reshape(1, D_MODEL))


def _first_max(v, lane, valid):
    m = jnp.max(jnp.where(valid, v, -jnp.inf), axis=1, keepdims=True)
    idx = jnp.min(jnp.where(valid & (v == m), lane, float(N_EXPERTS)), axis=1, keepdims=True)
    return m, idx


def _router_gates(h, wr, br):
    logits = jnp.dot(h, wr, preferred_element_type=F32, precision=lax.Precision.HIGHEST)
    mx = jnp.max(logits, axis=1, keepdims=True)
    ex = jnp.exp(logits - mx)
    probs = ex / jnp.sum(ex, axis=1, keepdims=True)
    sel = probs + br
    lane_i = lax.broadcasted_iota(jnp.int32, sel.shape, 1)
    grp = (lane_i // EXPERTS_PER_GROUP).astype(F32)
    lane = lane_i.astype(F32)
    best = None
    gsel = None
    for g in range(N_GROUPS):
        in_g = grp == g
        m1, i1 = _first_max(sel, lane, in_g)
        m2, _ = _first_max(sel, lane, in_g & (lane != i1))
        score = m1 + m2
        if g == 0:
            best, gsel = score, jnp.zeros_like(i1)
        else:
            better = score > best
            gsel = jnp.where(better, float(g), gsel)
            best = jnp.where(better, score, best)
    in_grp = grp == gsel
    _, i1 = _first_max(sel, lane, in_grp)
    _, i2 = _first_max(sel, lane, in_grp & (lane != i1))
    w1 = jnp.sum(jnp.where(lane == i1, probs, 0.0), axis=1, keepdims=True)
    w2 = jnp.sum(jnp.where(lane == i2, probs, 0.0), axis=1, keepdims=True)
    wsum = w1 + w2
    return jnp.where(lane == i1, w1 / wsum, 0.0) + jnp.where(lane == i2, w2 / wsum, 0.0)


def _moe_kernel(x_ref, sc_ref, sh_ref, gt_ref, wr_ref, br_ref, wgu_ref, wd_ref, g_ref, b_ref, out_ref,
                h_scr, gates_scr, acc_scr):
    e = pl.program_id(1)

    @pl.when(e == 0)
    def _():
        h = x_ref[...] * (1.0 + sc_ref[...]) + sh_ref[...]
        h_scr[...] = h.astype(BF16)
        gates_scr[...] = _router_gates(h, wr_ref[...], br_ref[...])
        acc_scr[...] = jnp.zeros_like(acc_scr)

    gu = _dot(h_scr[...], wgu_ref[...])
    gates = gates_scr[...]
    lane = lax.broadcasted_iota(jnp.int32, gates.shape, 1)
    gate_e = jnp.sum(jnp.where(lane == e, gates, 0.0), axis=1, keepdims=True)
    act = _silu(gu[:, :D_EXPERT]) * gu[:, D_EXPERT:] * gate_e
    acc_scr[...] += _dot(act.astype(BF16), wd_ref[...])

    @pl.when(e == pl.num_programs(1) - 1)
    def _():
        z = ALPHA * x_ref[...] + (1.0 + gt_ref[...]) * acc_scr[...]
        out_ref[...] = _layer_norm(z, g_ref[...], b_ref[...])


def _moe_ln(geo, x, mod, w_router, b_router, w_gu, w_down, layer, ln_g, ln_b):
    T, tm = geo.T, geo.tm
    vec = pl.BlockSpec((1, D_MODEL), lambda i, e: (0, 0))
    return pl.pallas_call(
        _moe_kernel,
        grid=(T // tm, N_EXPERTS),
        in_specs=[
            pl.BlockSpec((tm, D_MODEL), lambda i, e: (i, 0)),
            _mod_spec(geo, 4),
            _mod_spec(geo, 3),
            _mod_spec(geo, 5),
            pl.BlockSpec((D_MODEL, N_EXPERTS), lambda i, e: (0, 0)),
            pl.BlockSpec((1, N_EXPERTS), lambda i, e: (0, 0)),
            pl.BlockSpec((None, None, D_MODEL, 2 * D_EXPERT), lambda i, e: (layer, e, 0, 0)),
            pl.BlockSpec((None, None, D_EXPERT, D_MODEL), lambda i, e: (layer, e, 0, 0)),
            vec, vec,
        ],
        out_specs=pl.BlockSpec((tm, D_MODEL), lambda i, e: (i, 0)),
        out_shape=jax.ShapeDtypeStruct((T, D_MODEL), F32),
        scratch_shapes=[
            pltpu.VMEM((tm, D_MODEL), BF16),
            pltpu.VMEM((tm, N_EXPERTS), F32),
            pltpu.VMEM((tm, D_MODEL), F32),
        ],
        compiler_params=_cparams(("parallel", "arbitrary")),
        name="moe_ln",
    )(x, mod, mod, mod, w_router, b_router.reshape(1, N_EXPERTS), w_gu, w_down,
      ln_g.reshape(1, D_MODEL), ln_b.reshape(1, D_MODEL))


def _trunk(geo, x, mods, s_ret, s_dn, s_conv, wts, ret_tl, dn_tl):
    (ln_g, ln_b, ret_w_in, ret_gn_g, ret_w_out, dn_w_main, dn_w_tail, dn_conv_w, dn_a_log, dn_dt_bias,
     dn_norm_g, dn_w_out, router_w, router_b, moe_w_gu, moe_w_down) = wts
    tables = _ret_tables(geo)
    new_ret, new_dn, new_conv = [], [], []
    for i in range(DEPTH):
        if geo.per_token:
            mod = jnp.repeat(mods[i], geo.L, axis=0)
        else:
            mod = mods[i].reshape(geo.B, 1, N_MOD * D_MODEL)
        j = i // 2
        if i % 2 == 0:
            p, _ = _inproj(geo, x, mod, ret_w_in[j], None, tn=1536)
            o, s = _retention(geo, p, tables, ret_gn_g[j], s_ret, j, ret_tl)
            new_ret.append(s)
            w_out = ret_w_out[j]
        else:
            p, ab = _inproj(geo, x, mod, dn_w_main[j], dn_w_tail[j], tn=1536)
            o, s, buf = _deltanet(geo, p, ab, dn_conv_w[j], dn_a_log[j], dn_dt_bias[j], dn_norm_g[j],
                                  s_dn, s_conv, j, dn_tl)
            new_dn.append(s)
            new_conv.append(buf)
            w_out = dn_w_out[j]
        x = _outproj_ln(geo, o, w_out, x, mod, ln_g[i, 0], ln_b[i, 0])
        x = _moe_ln(geo, x, mod, router_w, router_b, moe_w_gu, moe_w_down, i, ln_g[i, 1], ln_b[i, 1])
    return x, jnp.stack(new_ret), jnp.stack(new_dn), jnp.stack(new_conv)


def _run(x_prompt, x_sample, state_ret, state_dn, state_conv, c_prompt, c_sample, w_ada, b_ada, ln_g, ln_b,
         ret_w_in, ret_gn_g, ret_w_out, dn_w_in, dn_conv_w, dn_a_log, dn_dt_bias, dn_norm_g, dn_w_out,
         router_w, router_b, moe_w_gu, moe_w_down, *, past_len):
    Bp, Lp, _ = x_prompt.shape
    Bs, Ls, _ = x_sample.shape
    Ls_pad = -(-Ls // SUBLANES) * SUBLANES
    tm_p = min(512, Lp)
    geo_p = _Geo(Bp, Lp, Lp, 0, tm_p, False, BF16)
    Ts = Bs * Ls_pad
    geo_s = _Geo(Bs, Ls_pad, Ls, past_len, min(512, Ts), True, F32)

    dn_w_tail = jnp.pad(dn_w_in[:, :, DN_MAIN:], ((0, 0), (0, 0), (0, LANES - 2 * DN_V_HEADS))).astype(BF16)
    wts = (ln_g, ln_b, ret_w_in.astype(BF16), ret_gn_g, ret_w_out.astype(BF16),
           dn_w_in[:, :, :DN_MAIN].astype(BF16), dn_w_tail, dn_conv_w, dn_a_log, dn_dt_bias, dn_norm_g,
           dn_w_out.astype(BF16), router_w, router_b, moe_w_gu.astype(BF16), moe_w_down.astype(BF16))

    mods = _ada(jnp.concatenate([c_prompt, c_sample], axis=0), w_ada, b_ada)

    xp = x_prompt.reshape(Bp * Lp, D_MODEL)
    yp, ret_p, dn_p, conv_p = _trunk(geo_p, xp, mods[:, :Bp], None, None, None, wts,
                                     ret_tl=min(512, Lp), dn_tl=min(256, Lp))
    xs = jnp.pad(x_sample, ((0, 0), (0, Ls_pad - Ls), (0, 0))).reshape(Ts, D_MODEL)
    ys, ret_s, dn_s, conv_s = _trunk(geo_s, xs, mods[:, Bp:], state_ret, state_dn, state_conv, wts,
                                     ret_tl=Ls_pad, dn_tl=Ls_pad)
    y_prompt = yp.reshape(Bp, Lp, D_MODEL)
    y_sample = ys.reshape(Bs, Ls_pad, D_MODEL)[:, :Ls]
    return (y_prompt, y_sample, ret_p, ret_s, dn_p, dn_s, conv_p, conv_s)


def kernel(x_prompt, x_sample, state_ret, state_dn, state_conv, c_prompt, c_sample, w_ada, b_ada, ln_g, ln_b,
           ret_w_in, ret_gn_g, ret_w_out, dn_w_in, dn_conv_w, dn_a_log, dn_dt_bias, dn_norm_g, dn_w_out,
           router_w, router_b, moe_w_gu, moe_w_down):
    return _run(x_prompt, x_sample, state_ret, state_dn, state_conv, c_prompt, c_sample, w_ada, b_ada,
                ln_g, ln_b, ret_w_in, ret_gn_g, ret_w_out, dn_w_in, dn_conv_w, dn_a_log, dn_dt_bias,
                dn_norm_g, dn_w_out, router_w, router_b, moe_w_gu, moe_w_down, past_len=PAST_LEN)
```

```python
import functools
from typing import NamedTuple

import jax
import jax.numpy as jnp
from jax import lax
from jax.experimental import pallas as pl
from jax.experimental.pallas import tpu as pltpu

F32 = jnp.float32
BF16 = jnp.bfloat16

D_MODEL = 1024
DEPTH = 4
PAST_LEN = 16384
RET_HEADS = 4
RET_DK = 256
RET_DV = 512
RET_QW = RET_HEADS * RET_DK
RET_VW = RET_HEADS * RET_DV
RET_CHUNK = 128
DN_QK_HEADS = 8
DN_V_HEADS = 16
DN_DK = 128
DN_DV = 128
DN_QW = DN_QK_HEADS * DN_DK
DN_VW = DN_V_HEADS * DN_DV
DN_CONV_CH = 2 * DN_QW + DN_VW
DN_CONV_W = 4
DN_CHUNK = 64
DN_MAIN = DN_CONV_CH + DN_VW
N_EXPERTS = 16
N_GROUPS = 4
EXPERTS_PER_GROUP = N_EXPERTS // N_GROUPS
D_EXPERT = 256
N_MOD = 6
ALPHA = (2.0 * DEPTH) ** 0.25
LN_EPS = 1e-5

LANES = 128
SUBLANES = 8
MXU_DIM = 256
VMEM_LIMIT = 56 * 1024 * 1024


class _Geo(NamedTuple):
    B: int
    L: int
    l_true: int
    pos0: int
    tm: int
    per_token: bool
    act_dtype: object

    @property
    def T(self):
        return self.B * self.L


def _cparams(sem):
    return pltpu.CompilerParams(dimension_semantics=sem, vmem_limit_bytes=VMEM_LIMIT)


def _mod_spec(geo, chunk):
    if geo.per_token:
        return pl.BlockSpec((geo.tm, D_MODEL), lambda *g: (g[0], chunk))
    tpr = geo.L // geo.tm
    return pl.BlockSpec((None, 1, D_MODEL), lambda *g: (g[0] // tpr, 0, chunk))


def _silu(x):
    return x * jax.nn.sigmoid(x)


def _dot(a, b):
    return jnp.dot(a, b, preferred_element_type=F32)


def _dot_nt(a, b):
    return lax.dot_general(a, b, (((1,), (1,)), ((), ())), preferred_element_type=F32)


def _dot_tn(a, b):
    return lax.dot_general(a, b, (((0,), (0,)), ((), ())), preferred_element_type=F32)


def _layer_norm(z, g, b):
    mu = jnp.mean(z, axis=-1, keepdims=True)
    zc = z - mu
    var = jnp.mean(zc * zc, axis=-1, keepdims=True)
    return zc * lax.rsqrt(var + LN_EPS) * g + b


def _ada_kernel(c_ref, w_ref, b_ref, o_ref):
    cs = _silu(c_ref[...]).astype(BF16)
    o_ref[...] = _dot(cs, w_ref[...].astype(BF16)) + b_ref[...]


def _ada(c_all, w_ada, b_ada):
    R = c_all.shape[0]
    tn = 1024
    n_out = N_MOD * D_MODEL
    return pl.pallas_call(
        _ada_kernel,
        grid=(DEPTH, n_out // tn),
        in_specs=[
            pl.BlockSpec((R, D_MODEL), lambda l, j: (0, 0)),
            pl.BlockSpec((None, D_MODEL, tn), lambda l, j: (l, 0, j)),
            pl.BlockSpec((None, 1, tn), lambda l, j: (l, 0, j)),
        ],
        out_specs=pl.BlockSpec((None, R, tn), lambda l, j: (l, 0, j)),
        out_shape=jax.ShapeDtypeStruct((DEPTH, R, n_out), F32),
        compiler_params=_cparams(("parallel", "parallel")),
        name="ada_mod",
    )(c_all, w_ada, b_ada.reshape(DEPTH, 1, n_out))


def _inproj_kernel(*refs, has_tail):
    if has_tail:
        x_ref, sc_ref, sh_ref, w_ref, wt_ref, o_ref, ot_ref, h_scr = refs
    else:
        x_ref, sc_ref, sh_ref, w_ref, o_ref, h_scr = refs

    @pl.when(pl.program_id(1) == 0)
    def _():
        h = x_ref[...] * (1.0 + sc_ref[...]) + sh_ref[...]
        hb = h.astype(BF16)
        h_scr[...] = hb
        if has_tail:
            ot_ref[...] = _dot(hb, wt_ref[...])

    o_ref[...] = _dot(h_scr[...], w_ref[...]).astype(o_ref.dtype)


def _inproj(geo, x, mod, w, w_tail, tn):
    T, tm = geo.T, geo.tm
    N = w.shape[1]
    has_tail = w_tail is not None
    in_specs = [
        pl.BlockSpec((tm, D_MODEL), lambda i, j: (i, 0)),
        _mod_spec(geo, 1),
        _mod_spec(geo, 0),
        pl.BlockSpec((D_MODEL, tn), lambda i, j: (0, j)),
    ]
    args = [x, mod, mod, w]
    out_specs = [pl.BlockSpec((tm, tn), lambda i, j: (i, j))]
    out_shape = [jax.ShapeDtypeStruct((T, N), geo.act_dtype)]
    if has_tail:
        in_specs.append(pl.BlockSpec((D_MODEL, LANES), lambda i, j: (0, 0)))
        args.append(w_tail)
        out_specs.append(pl.BlockSpec((tm, LANES), lambda i, j: (i, 0)))
        out_shape.append(jax.ShapeDtypeStruct((T, LANES), F32))
    res = pl.pallas_call(
        functools.partial(_inproj_kernel, has_tail=has_tail),
        grid=(T // tm, N // tn),
        in_specs=in_specs,
        out_specs=out_specs,
        out_shape=out_shape,
        scratch_shapes=[pltpu.VMEM((tm, D_MODEL), BF16)],
        compiler_params=_cparams(("parallel", "arbitrary")),
        name="in_proj",
    )(*args)
    return res if has_tail else (res[0], None)


def _pad_rows(x, rows):
    if x.shape[0] == rows:
        return x
    return jnp.concatenate([x, jnp.zeros((rows - x.shape[0], x.shape[1]), x.dtype)], axis=0)


def _ret_kernel(*refs, rows, nchunk, has_state):
    if has_state:
        (q_ref, k_ref, v_ref, g_ref, cos_ref, sin_ref, dintra_ref, qdec_ref, kdec_ref, sdec_ref, gn_ref,
         s0_ref, o_ref, s_out_ref, s_scr) = refs
    else:
        (q_ref, k_ref, v_ref, g_ref, cos_ref, sin_ref, dintra_ref, qdec_ref, kdec_ref, sdec_ref, gn_ref,
         o_ref, s_out_ref, s_scr) = refs
    i = pl.program_id(1)
    c = RET_CHUNK
    half = RET_DK // 2

    @pl.when(i == 0)
    def _():
        if has_state:
            s_scr[...] = s0_ref[...]
        else:
            s_scr[...] = jnp.zeros_like(s_scr)

    def rot(ref, r, h, cos, sin):
        x1 = ref[r, h * RET_DK:h * RET_DK + half].astype(F32)
        x2 = ref[r, h * RET_DK + half:(h + 1) * RET_DK].astype(F32)
        return jnp.concatenate([x1 * cos - x2 * sin, x2 * cos + x1 * sin], axis=1)

    for ci in range(nchunk):
        r = slice(ci * rows, (ci + 1) * rows)
        cos = cos_ref[r, :]
        sin = sin_ref[r, :]
        for h in range(RET_HEADS):
            vs = slice(h * RET_DV, (h + 1) * RET_DV)
            q = _pad_rows(rot(q_ref, r, h, cos, sin), c)
            k = _pad_rows(rot(k_ref, r, h, cos, sin), c) * (RET_DK ** -0.5)
            if rows == c:
                v = v_ref[r, vs].astype(BF16)
            else:
                v = _pad_rows(v_ref[r, vs].astype(F32), c).astype(BF16)
            s = s_scr[h]
            att = _dot_nt(q.astype(BF16), k.astype(BF16)) * dintra_ref[h]
            o = _dot(att.astype(BF16), v) + _dot((q * qdec_ref[h]).astype(BF16), s.astype(BF16))
            s_scr[h] = s * sdec_ref[h] + _dot_tn((k * kdec_ref[h]).astype(BF16), v)
            o = o[:rows]
            mu = jnp.mean(o, axis=-1, keepdims=True)
            oc = o - mu
            var = jnp.mean(oc * oc, axis=-1, keepdims=True)
            on = oc * lax.rsqrt(var + LN_EPS) * gn_ref[...]
            gate = g_ref[r, vs].astype(F32)
            o_ref[r, vs] = (on * _silu(gate)).astype(o_ref.dtype)

    @pl.when(i == pl.num_programs(1) - 1)
    def _():
        s_out_ref[...] = s_scr[...]


def _ret_tables(geo):
    c_true = min(RET_CHUNK, geo.l_true)
    half = RET_DK // 2
    inv_freq = 10000.0 ** (-jnp.linspace(0.0, 1.0, half, dtype=F32))
    pos = (geo.pos0 + jnp.arange(geo.L)).astype(F32)
    ang = pos[:, None] * inv_freq[None, :]
    lg = jnp.log(1.0 - 2.0 ** (-5.0 - jnp.arange(RET_HEADS, dtype=F32)))
    idx = jnp.arange(c_true, dtype=F32)
    diff = idx[:, None] - idx[None, :]
    dintra = jnp.exp(jnp.where(diff[None] >= 0, diff[None] * lg[:, None, None], -jnp.inf))
    qdec = jnp.exp((idx + 1.0)[None, :] * lg[:, None])
    kdec = jnp.exp((c_true - 1.0 - idx)[None, :] * lg[:, None])
    sdec = jnp.exp(c_true * lg)
    pad = RET_CHUNK - c_true
    dintra = jnp.pad(dintra, ((0, 0), (0, pad), (0, pad)))
    qdec = jnp.pad(qdec, ((0, 0), (0, pad)))[..., None]
    kdec = jnp.pad(kdec, ((0, 0), (0, pad)))[..., None]
    sdec = jnp.broadcast_to(sdec[:, None, None], (RET_HEADS, 1, RET_DV))
    return jnp.cos(ang), jnp.sin(ang), dintra, qdec, kdec, sdec


def _retention(geo, p, tables, gn_g, s_all, layer_j, tl):
    B, L = geo.B, geo.L
    nblk = L // tl
    rows = min(tl, RET_CHUNK)
    nchunk = tl // rows
    has_state = s_all is not None
    cos, sin, dintra, qdec, kdec, sdec = tables
    rb = lambda b, i: b * nblk + i
    full3 = lambda b, i: (0, 0, 0)
    in_specs = [
        pl.BlockSpec((tl, RET_QW), lambda b, i: (rb(b, i), 0)),
        pl.BlockSpec((tl, RET_QW), lambda b, i: (rb(b, i), 1)),
        pl.BlockSpec((tl, RET_VW), lambda b, i: (rb(b, i), 1)),
        pl.BlockSpec((tl, RET_VW), lambda b, i: (rb(b, i), 2)),
        pl.BlockSpec((tl, RET_DK // 2), lambda b, i: (i, 0)),
        pl.BlockSpec((tl, RET_DK // 2), lambda b, i: (i, 0)),
        pl.BlockSpec((RET_HEADS, RET_CHUNK, RET_CHUNK), full3),
        pl.BlockSpec((RET_HEADS, RET_CHUNK, 1), full3),
        pl.BlockSpec((RET_HEADS, RET_CHUNK, 1), full3),
        pl.BlockSpec((RET_HEADS, 1, RET_DV), full3),
        pl.BlockSpec((1, RET_DV), lambda b, i: (0, 0)),
    ]
    args = [p, p, p, p, cos, sin, dintra, qdec, kdec, sdec, gn_g.reshape(1, RET_DV)]
    if has_state:
        in_specs.append(pl.BlockSpec((None, None, RET_HEADS, RET_DK, RET_DV),
                                     lambda b, i: (layer_j, b, 0, 0, 0)))
        args.append(s_all)
    o, s = pl.pallas_call(
        functools.partial(_ret_kernel, rows=rows, nchunk=nchunk, has_state=has_state),
        grid=(B, nblk),
        in_specs=in_specs,
        out_specs=[
            pl.BlockSpec((tl, RET_VW), lambda b, i: (rb(b, i), 0)),
            pl.BlockSpec((None, RET_HEADS, RET_DK, RET_DV), lambda b, i: (b, 0, 0, 0)),
        ],
        out_shape=[
            jax.ShapeDtypeStruct((geo.T, RET_VW), geo.act_dtype),
            jax.ShapeDtypeStruct((B, RET_HEADS, RET_DK, RET_DV), F32),
        ],
        scratch_shapes=[pltpu.VMEM((RET_HEADS, RET_DK, RET_DV), F32)],
        compiler_params=_cparams(("parallel", "arbitrary")),
        name="retention",
    )(*args)
    return o, s


DN_QUAD = MXU_DIM // DN_CHUNK
DN_CONV_PAD = SUBLANES


def _block_diag(x_cat, n, mask):
    return jnp.where(mask, jnp.concatenate([x_cat] * n, axis=0), 0.0)


def _dn_kernel(*refs, rows, nchunk, l_last, has_state):
    if has_state:
        (qkv_ref, z_ref, ab_ref, cw_ref, alog_ref, dtb_ref, ng_ref, s0_ref, buf_ref,
         o_ref, s_out_ref, buf_out_ref, s_scr, xp_scr, y_scr) = refs
    else:
        (qkv_ref, z_ref, ab_ref, cw_ref, alog_ref, dtb_ref, ng_ref,
         o_ref, s_out_ref, buf_out_ref, s_scr, xp_scr, y_scr) = refs
    i = pl.program_id(1)
    c = DN_CHUNK
    tl = rows * nchunk
    npair = DN_V_HEADS // 2
    W = DN_CONV_W
    P0 = DN_CONV_PAD

    @pl.when(i == 0)
    def _():
        if has_state:
            for pr in range(npair):
                s_scr[pr] = jnp.concatenate([s0_ref[2 * pr], s0_ref[2 * pr + 1]], axis=1)
            xp_scr[P0 - (W - 1):P0, :] = buf_ref[...]
        else:
            s_scr[...] = jnp.zeros_like(s_scr)
            xp_scr[0:P0, :] = jnp.zeros((P0, DN_CONV_CH), F32)

    xp_scr[P0:P0 + tl, :] = qkv_ref[...].astype(F32)
    y = xp_scr[P0 - (W - 1):P0 - (W - 1) + tl, :] * cw_ref[0:1, :]
    for t in range(1, W):
        y = y + xp_scr[P0 - (W - 1) + t:P0 - (W - 1) + t + tl, :] * cw_ref[t:t + 1, :]
    y_scr[0:tl, :] = _silu(y)
    if rows < c:
        y_scr[tl:, :] = jnp.zeros((y_scr.shape[0] - tl, DN_CONV_CH), F32)

    @pl.when(i == pl.num_programs(1) - 1)
    def _():
        buf_out_ref[...] = xp_scr[P0 + l_last - (W - 1):P0 + l_last, :]

    xp_scr[P0 - (W - 1):P0, :] = xp_scr[P0 + tl - (W - 1):P0 + tl, :]

    ri = lax.broadcasted_iota(jnp.int32, (c, c), 0)
    ci_ = lax.broadcasted_iota(jnp.int32, (c, c), 1)
    incl = ri >= ci_
    strict = ri > ci_
    eye = ri == ci_
    tri_f = incl.astype(F32)
    eye_cat = jnp.concatenate([eye.astype(F32)] * DN_QUAD, axis=1)
    bi = lax.broadcasted_iota(jnp.int32, (MXU_DIM, MXU_DIM), 0) // c
    bj = lax.broadcasted_iota(jnp.int32, (MXU_DIM, MXU_DIM), 1) // c
    bd_mask = bi == bj
    lane2 = lax.broadcasted_iota(jnp.int32, (DN_DK, 2 * DN_DV), 1)
    left = lane2 < DN_DV
    row_ok = lax.broadcasted_iota(jnp.int32, (c, 1), 0) < l_last if rows < c else None
    n_levels = c.bit_length() - 1

    nquad = DN_V_HEADS // DN_QUAD
    quads = [[DN_QUAD * q + t for t in range(DN_QUAD)] for q in range(nquad)]

    pre = []
    for ck in range(nchunk):
        r0 = ck * c if rows == c else 0
        r = slice(r0, r0 + c)
        ab = _pad_rows(ab_ref[ck * rows:(ck + 1) * rows, :], c)
        sp = ab + dtb_ref[...]
        g_full = -jnp.exp(alog_ref[...]) * (jnp.maximum(sp, 0.0) + jnp.log1p(jnp.exp(-jnp.abs(sp))))
        beta_full = jax.nn.sigmoid(ab)
        if row_ok is not None:
            g_full = jnp.where(row_ok, g_full, 0.0)
            beta_full = jnp.where(row_ok, beta_full, 0.0)
        G_full = jnp.dot(tri_f, g_full, preferred_element_type=F32, precision=lax.Precision.HIGHEST)

        kn, qn, KK, QK = {}, {}, {}, {}
        for j in range(DN_QK_HEADS):
            qh = y_scr[r, j * DN_DK:(j + 1) * DN_DK]
            kh = y_scr[r, DN_QW + j * DN_DK:DN_QW + (j + 1) * DN_DK]
            qn[j] = qh * lax.rsqrt(jnp.sum(qh * qh, axis=-1, keepdims=True) + 1e-6) * (DN_DK ** -0.5)
            kn[j] = kh * lax.rsqrt(jnp.sum(kh * kh, axis=-1, keepdims=True) + 1e-6)
        for j in range(DN_QK_HEADS):
            kb16 = kn[j].astype(BF16)
            kq = _dot_nt(jnp.concatenate([kb16, qn[j].astype(BF16)], axis=0), kb16)
            KK[j], QK[j] = kq[:c], kq[c:]

        A, att, eG, Gc, rhs = {}, {}, {}, {}, {}
        for h in range(DN_V_HEADS):
            j = h // 2
            Gc[h] = G_full[:, h:h + 1]
            bc = beta_full[:, DN_V_HEADS + h:DN_V_HEADS + h + 1]
            Gr = jnp.sum(jnp.where(eye, Gc[h], 0.0), axis=0, keepdims=True)
            dec = jnp.exp(jnp.where(incl, Gc[h] - Gr, -jnp.inf))
            A[h] = jnp.where(strict, bc * KK[j] * dec, 0.0)
            att[h] = QK[j] * dec
            eG[h] = jnp.exp(Gc[h])
            vh = y_scr[r, 2 * DN_QW + h * DN_DV:2 * DN_QW + (h + 1) * DN_DV]
            rhs[h] = jnp.concatenate([vh * bc, kn[j] * (bc * eG[h])], axis=1)

        X = [jnp.concatenate([A[h] for h in hs], axis=1) for hs in quads]
        Pm = [eye_cat - x for x in X]
        X = [_dot(x.astype(BF16), _block_diag(x, DN_QUAD, bd_mask).astype(BF16)) for x in X]
        for lvl in range(1, n_levels):
            Xw = [_block_diag(x, DN_QUAD, bd_mask).astype(BF16) for x in X]
            if lvl < n_levels - 1:
                both = [_dot(jnp.concatenate([X[q], Pm[q]], axis=0).astype(BF16), Xw[q]) for q in range(nquad)]
                X = [b[:c] for b in both]
                Pm = [Pm[q] + both[q][c:] for q in range(nquad)]
            else:
                Pm = [Pm[q] + _dot(Pm[q].astype(BF16), Xw[q]) for q in range(nquad)]
        sol = [_dot(_block_diag(Pm[q], DN_QUAD, bd_mask).astype(BF16),
                    jnp.concatenate([rhs[h] for h in quads[q]], axis=0).astype(BF16))
               for q in range(nquad)]
        att_bd = [_block_diag(jnp.concatenate([att[h] for h in quads[q]], axis=1), DN_QUAD, bd_mask).astype(BF16)
                  for q in range(nquad)]
        pre.append((kn, qn, Gc, eG, sol, att_bd))

    for ck in range(nchunk):
        kn, qn, Gc, eG, sol, att_bd = pre[ck]
        sol_h = {h: sol[h // DN_QUAD][c * (h % DN_QUAD):c * (h % DN_QUAD + 1)] for h in range(DN_V_HEADS)}
        u, oS = {}, {}
        for pr in range(npair):
            h0, h1 = 2 * pr, 2 * pr + 1
            S = s_scr[pr]
            Sbd = jnp.concatenate([jnp.where(left, S, 0.0), jnp.where(left, 0.0, S)], axis=0).astype(BF16)
            w_cat = jnp.concatenate([sol_h[h0][:, DN_DV:], sol_h[h1][:, DN_DV:]], axis=1)
            qe_cat = jnp.concatenate([qn[pr] * eG[h0], qn[pr] * eG[h1]], axis=1)
            R = _dot(jnp.concatenate([w_cat, qe_cat], axis=0).astype(BF16), Sbd)
            u[h0] = sol_h[h0][:, :DN_DV] - R[:c, :DN_DV]
            u[h1] = sol_h[h1][:, :DN_DV] - R[:c, DN_DV:]
            oS[h0] = R[c:, :DN_DV]
            oS[h1] = R[c:, DN_DV:]

        o_intra = [_dot(att_bd[q], jnp.concatenate([u[h] for h in quads[q]], axis=0).astype(BF16))
                   for q in range(nquad)]

        for pr in range(npair):
            h0, h1 = 2 * pr, 2 * pr + 1
            gl0 = Gc[h0][c - 1:c, :]
            gl1 = Gc[h1][c - 1:c, :]
            du = jnp.concatenate([u[h0] * jnp.exp(gl0 - Gc[h0]), u[h1] * jnp.exp(gl1 - Gc[h1])], axis=1)
            sdec = jnp.where(left[0:1, :], jnp.exp(gl0), jnp.exp(gl1))
            s_scr[pr] = s_scr[pr] * sdec + _dot_tn(kn[pr].astype(BF16), du.astype(BF16))

        zr = slice(ck * rows, (ck + 1) * rows)
        for h in range(DN_V_HEADS):
            t = h % DN_QUAD
            o = (oS[h] + o_intra[h // DN_QUAD][c * t:c * (t + 1)])[:rows]
            o = o * lax.rsqrt(jnp.mean(o * o, axis=-1, keepdims=True) + 1e-6) * ng_ref[...]
            hs = slice(h * DN_DV, (h + 1) * DN_DV)
            z = z_ref[zr, hs].astype(F32)
            o_ref[zr, hs] = (o * _silu(z)).astype(o_ref.dtype)

    @pl.when(i == pl.num_programs(1) - 1)
    def _():
        for pr in range(npair):
            s_out_ref[2 * pr] = s_scr[pr][:, :DN_DV]
            s_out_ref[2 * pr + 1] = s_scr[pr][:, DN_DV:]


def _deltanet(geo, p, ab, conv_w, a_log, dt_bias, norm_g, s_all, buf_all, layer_j, tl):
    B, L = geo.B, geo.L
    nblk = L // tl
    rows = min(tl, DN_CHUNK)
    nchunk = tl // rows
    l_last = geo.l_true - (nblk - 1) * tl
    has_state = s_all is not None
    rb = lambda b, i: b * nblk + i
    pad16 = lambda v: jnp.pad(v.reshape(1, DN_V_HEADS), ((0, 0), (0, LANES - DN_V_HEADS)))
    in_specs = [
        pl.BlockSpec((tl, DN_CONV_CH), lambda b, i: (rb(b, i), 0)),
        pl.BlockSpec((tl, DN_VW), lambda b, i: (rb(b, i), DN_CONV_CH // DN_VW)),
        pl.BlockSpec((tl, LANES), lambda b, i: (rb(b, i), 0)),
        pl.BlockSpec((DN_CONV_W, DN_CONV_CH), lambda b, i: (0, 0)),
        pl.BlockSpec((1, LANES), lambda b, i: (0, 0)),
        pl.BlockSpec((1, LANES), lambda b, i: (0, 0)),
        pl.BlockSpec((1, DN_DV), lambda b, i: (0, 0)),
    ]
    args = [p, p, ab, conv_w, pad16(a_log), pad16(dt_bias), norm_g.reshape(1, DN_DV)]
    if has_state:
        in_specs += [
            pl.BlockSpec((None, None, DN_V_HEADS, DN_DK, DN_DV), lambda b, i: (layer_j, b, 0, 0, 0)),
            pl.BlockSpec((None, None, DN_CONV_W - 1, DN_CONV_CH), lambda b, i: (layer_j, b, 0, 0)),
        ]
        args += [s_all, buf_all]
    y_rows = max(tl, DN_CHUNK)
    o, s, buf = pl.pallas_call(
        functools.partial(_dn_kernel, rows=rows, nchunk=nchunk, l_last=l_last, has_state=has_state),
        grid=(B, nblk),
        in_specs=in_specs,
        out_specs=[
            pl.BlockSpec((tl, DN_VW), lambda b, i: (rb(b, i), 0)),
            pl.BlockSpec((None, DN_V_HEADS, DN_DK, DN_DV), lambda b, i: (b, 0, 0, 0)),
            pl.BlockSpec((None, DN_CONV_W - 1, DN_CONV_CH), lambda b, i: (b, 0, 0)),
        ],
        out_shape=[
            jax.ShapeDtypeStruct((geo.T, DN_VW), geo.act_dtype),
            jax.ShapeDtypeStruct((B, DN_V_HEADS, DN_DK, DN_DV), F32),
            jax.ShapeDtypeStruct((B, DN_CONV_W - 1, DN_CONV_CH), F32),
        ],
        scratch_shapes=[
            pltpu.VMEM((DN_V_HEADS // 2, DN_DK, 2 * DN_DV), F32),
            pltpu.VMEM((DN_CONV_PAD + tl, DN_CONV_CH), F32),
            pltpu.VMEM((y_rows, DN_CONV_CH), F32),
        ],
        compiler_params=_cparams(("parallel", "arbitrary")),
        name="deltanet",
    )(*args)
    return o, s, buf


def _outproj_kernel(o_ref, w_ref, x_ref, gate_ref, g_ref, b_ref, out_ref):
    y = _dot(o_ref[...].astype(BF16), w_ref[...])
    z = ALPHA * x_ref[...] + (1.0 + gate_ref[...]) * y
    out_ref[...] = _layer_norm(z, g_ref[...], b_ref[...])


def _outproj_ln(geo, o, w_out, x, mod, ln_g, ln_b):
    T, tm = geo.T, geo.tm
    K = w_out.shape[0]
    vec = pl.BlockSpec((1, D_MODEL), lambda i: (0, 0))
    return pl.pallas_call(
        _outproj_kernel,
        grid=(T // tm,),
        in_specs=[
            pl.BlockSpec((tm, K), lambda i: (i, 0)),
            pl.BlockSpec((K, D_MODEL), lambda i: (0, 0)),
            pl.BlockSpec((tm, D_MODEL), lambda i: (i, 0)),
            _mod_spec(geo, 2),
            vec, vec,
        ],
        out_specs=pl.BlockSpec((tm, D_MODEL), lambda i: (i, 0)),
        out_shape=jax.ShapeDtypeStruct((T, D_MODEL), F32),
        compiler_params=_cparams(("parallel",)),
        name="out_proj_ln",
    )(o, w_out, x, mod, ln_g.reshape(1, D_MODEL), ln_b.reshape(1, D_MODEL))


MOE_ROWS = 144


def _moe_blocks(tm):
    nb = (tm - N_GROUPS) // MOE_ROWS + N_GROUPS
    ns = -(-(nb * MOE_ROWS) // LANES) * LANES
    return nb, ns


def _first_max(v, row, valid):
    m = jnp.max(jnp.where(valid, v, -jnp.inf), axis=0, keepdims=True)
    idx = jnp.min(jnp.where(valid & (v == m), row, float(N_EXPERTS)), axis=0, keepdims=True)
    return m, idx


def _route(h, wrt, br):
    logits = lax.dot_general(wrt, h, (((1,), (1,)), ((), ())), preferred_element_type=F32,
                             precision=lax.Precision.HIGHEST)
    mx = jnp.max(logits, axis=0, keepdims=True)
    ex = jnp.exp(logits - mx)
    probs = ex / jnp.sum(ex, axis=0, keepdims=True)
    sel = probs + br
    row_i = lax.broadcasted_iota(jnp.int32, sel.shape, 0)
    grp = (row_i // EXPERTS_PER_GROUP).astype(F32)
    row = row_i.astype(F32)
    best = None
    gsel = None
    for g in range(N_GROUPS):
        in_g = grp == g
        m1, i1 = _first_max(sel, row, in_g)
        m2, _ = _first_max(sel, row, in_g & (row != i1))
        score = m1 + m2
        if g == 0:
            best, gsel = score, jnp.zeros_like(i1)
        else:
            better = score > best
            gsel = jnp.where(better, float(g), gsel)
            best = jnp.where(better, score, best)
    in_grp = grp == gsel
    _, i1 = _first_max(sel, row, in_grp)
    _, i2 = _first_max(sel, row, in_grp & (row != i1))
    w1 = jnp.sum(jnp.where(row == i1, probs, 0.0), axis=0, keepdims=True)
    w2 = jnp.sum(jnp.where(row == i2, probs, 0.0), axis=0, keepdims=True)
    wsum = w1 + w2
    gates = jnp.where(row == i1, w1 / wsum, 0.0) + jnp.where(row == i2, w2 / wsum, 0.0)
    return gsel, gates


def _router_kernel(x_ref, sc_ref, sh_ref, wrt_ref, br_ref, posr_ref, posc_ref, gates_ref, tbl_ref):
    tm = x_ref.shape[0]
    h = x_ref[...] * (1.0 + sc_ref[...]) + sh_ref[...]
    gsel, gates = _route(h, wrt_ref[...], br_ref[...])
    gates_ref[...] = gates
    grow = lax.broadcasted_iota(jnp.int32, (SUBLANES, tm), 0).astype(F32)
    og = grow == gsel
    ti = lax.broadcasted_iota(jnp.int32, (tm, tm), 0)
    tj = lax.broadcasted_iota(jnp.int32, (tm, tm), 1)
    before = jnp.where(ti < tj, 1.0, 0.0).astype(BF16)
    ogf = jnp.where(og, 1.0, 0.0)
    rank = _dot(ogf.astype(BF16), before)
    cnt = jnp.sum(ogf, axis=1, keepdims=True)
    nblk = jnp.zeros_like(cnt)
    for kb in range(-(-tm // MOE_ROWS)):
        nblk = nblk + jnp.where(cnt > float(kb * MOE_ROWS), 1.0, 0.0)
    seg = nblk * float(MOE_ROWS)
    block_start = lax.broadcasted_iota(jnp.int32, (1, LANES), 1).astype(F32) * float(MOE_ROWS)
    off = jnp.zeros((1, 1), F32)
    pos = jnp.zeros((1, tm), F32)
    tbl = jnp.zeros((1, LANES), F32)
    for g in range(N_GROUPS):
        pos = pos + jnp.where(gsel == float(g), off + rank[g:g + 1, :], 0.0)
        off = off + seg[g:g + 1, :]
        tbl = tbl + jnp.where(block_start >= off, 1.0, 0.0)
    posr_ref[...] = pos
    posc_ref[...] = jnp.sum(jnp.where(ti == tj, pos, 0.0), axis=1, keepdims=True)
    tbl_ref[...] = tbl.astype(jnp.int32)


def _expert_kernel(tbl_ref, x_ref, sc_ref, sh_ref, gt_ref, posr_ref, posc_ref, gates_ref, wgu_ref, wd_ref,
                   g_ref, b_ref, out_ref, hs_scr, gs_scr, ys_scr, *, nb):
    i = pl.program_id(0)
    tm = x_ref.shape[0]
    ns = hs_scr.shape[0]
    R = MOE_ROWS
    x = x_ref[...]
    hb = (x * (1.0 + sc_ref[...]) + sh_ref[...]).astype(BF16)
    slot = lax.broadcasted_iota(jnp.int32, (ns, tm), 0).astype(F32)
    P = jnp.where(slot == posr_ref[...], 1.0, 0.0).astype(BF16)
    hs_scr[...] = _dot(P, hb).astype(BF16)
    gates = gates_ref[...]
    g_hi = gates.astype(BF16)
    g_lo = (gates - g_hi.astype(F32)).astype(BF16)
    gs_scr[...] = _dot_nt(P, g_hi) + _dot_nt(P, g_lo)
    if nb * R < ns:
        ys_scr[nb * R:, :] = jnp.zeros((ns - nb * R, D_MODEL), BF16)

    for b in range(nb):
        rs = slice(b * R, (b + 1) * R)
        g = tbl_ref[i, b]

        @pl.when(g < N_GROUPS)
        def _():
            hsb = hs_scr[rs, :]
            gsb = gs_scr[rs, :]
            lane = lax.broadcasted_iota(jnp.int32, gsb.shape, 1)
            acc = jnp.zeros((R, D_MODEL), F32)
            for e in range(EXPERTS_PER_GROUP):
                eid = g * EXPERTS_PER_GROUP + e
                gu = _dot(hsb, wgu_ref[eid])
                gate = jnp.sum(jnp.where(lane == eid, gsb, 0.0), axis=1, keepdims=True)
                act = _silu(gu[:, :D_EXPERT]) * gu[:, D_EXPERT:] * gate
                acc = acc + _dot(act.astype(BF16), wd_ref[eid])
            ys_scr[rs, :] = acc.astype(BF16)

        @pl.when(g >= N_GROUPS)
        def _():
            ys_scr[rs, :] = jnp.zeros((R, D_MODEL), BF16)

    slot_t = lax.broadcasted_iota(jnp.int32, (tm, ns), 1).astype(F32)
    PT = jnp.where(slot_t == posc_ref[...], 1.0, 0.0).astype(BF16)
    y = _dot(PT, ys_scr[...])
    z = ALPHA * x + (1.0 + gt_ref[...]) * y
    out_ref[...] = _layer_norm(z, g_ref[...], b_ref[...])


def _moe_ln(geo, x, mod, w_router_t, b_router, w_gu, w_down, layer, ln_g, ln_b):
    T, tm = geo.T, geo.tm
    nt = T // tm
    nb, ns = _moe_blocks(tm)
    posr, posc, gates, tbl = pl.pallas_call(
        _router_kernel,
        grid=(nt,),
        in_specs=[
            pl.BlockSpec((tm, D_MODEL), lambda i: (i, 0)),
            _mod_spec(geo, 4),
            _mod_spec(geo, 3),
            pl.BlockSpec((N_EXPERTS, D_MODEL), lambda i: (0, 0)),
            pl.BlockSpec((N_EXPERTS, 1), lambda i: (0, 0)),
        ],
        out_specs=[
            pl.BlockSpec((None, 1, tm), lambda i: (i, 0, 0)),
            pl.BlockSpec((tm, 1), lambda i: (i, 0)),
            pl.BlockSpec((N_EXPERTS, tm), lambda i: (0, i)),
            pl.BlockSpec((None, 1, LANES), lambda i: (i, 0, 0)),
        ],
        out_shape=[
            jax.ShapeDtypeStruct((nt, 1, tm), F32),
            jax.ShapeDtypeStruct((T, 1), F32),
            jax.ShapeDtypeStruct((N_EXPERTS, T), F32),
            jax.ShapeDtypeStruct((nt, 1, LANES), jnp.int32),
        ],
        compiler_params=_cparams(("parallel",)),
        name="moe_router",
    )(x, mod, mod, w_router_t, b_router.reshape(N_EXPERTS, 1))

    vec = pl.BlockSpec((1, D_MODEL), lambda i, t: (0, 0))
    resident = dict(pipeline_mode=pl.Buffered(1))
    grid_spec = pltpu.PrefetchScalarGridSpec(
        num_scalar_prefetch=1,
        grid=(nt,),
        in_specs=[
            pl.BlockSpec((tm, D_MODEL), lambda i, t: (i, 0)),
            _mod_spec(geo, 4),
            _mod_spec(geo, 3),
            _mod_spec(geo, 5),
            pl.BlockSpec((None, 1, tm), lambda i, t: (i, 0, 0)),
            pl.BlockSpec((tm, 1), lambda i, t: (i, 0)),
            pl.BlockSpec((N_EXPERTS, tm), lambda i, t: (0, i)),
            pl.BlockSpec((None, N_EXPERTS, D_MODEL, 2 * D_EXPERT), lambda i, t: (layer, 0, 0, 0), **resident),
            pl.BlockSpec((None, N_EXPERTS, D_EXPERT, D_MODEL), lambda i, t: (layer, 0, 0, 0), **resident),
            vec, vec,
        ],
        out_specs=pl.BlockSpec((tm, D_MODEL), lambda i, t: (i, 0)),
        scratch_shapes=[
            pltpu.VMEM((ns, D_MODEL), BF16),
            pltpu.VMEM((ns, N_EXPERTS), F32),
            pltpu.VMEM((ns, D_MODEL), BF16),
        ],
    )
    return pl.pallas_call(
        functools.partial(_expert_kernel, nb=nb),
        grid_spec=grid_spec,
        out_shape=jax.ShapeDtypeStruct((T, D_MODEL), F32),
        compiler_params=_cparams(("parallel",)),
        name="moe_experts_ln",
    )(tbl.reshape(nt, LANES), x, mod, mod, mod, posr, posc, gates, w_gu, w_down,
      ln_g.reshape(1, D_MODEL), ln_b.reshape(1, D_MODEL))


def _trunk(geo, x, mods, s_ret, s_dn, s_conv, wts, ret_tl, dn_tl):
    (ln_g, ln_b, ret_w_in, ret_gn_g, ret_w_out, dn_w_main, dn_w_tail, dn_conv_w, dn_a_log, dn_dt_bias,
     dn_norm_g, dn_w_out, router_w, router_b, moe_w_gu, moe_w_down) = wts
    tables = _ret_tables(geo)
    new_ret, new_dn, new_conv = [], [], []
    for i in range(DEPTH):
        if geo.per_token:
            mod = jnp.repeat(mods[i], geo.L, axis=0)
        else:
            mod = mods[i].reshape(geo.B, 1, N_MOD * D_MODEL)
        j = i // 2
        if i % 2 == 0:
            p, _ = _inproj(geo, x, mod, ret_w_in[j], None, tn=1536)
            o, s = _retention(geo, p, tables, ret_gn_g[j], s_ret, j, ret_tl)
            new_ret.append(s)
            w_out = ret_w_out[j]
        else:
            p, ab = _inproj(geo, x, mod, dn_w_main[j], dn_w_tail[j], tn=1536)
            o, s, buf = _deltanet(geo, p, ab, dn_conv_w[j], dn_a_log[j], dn_dt_bias[j], dn_norm_g[j],
                                  s_dn, s_conv, j, dn_tl)
            new_dn.append(s)
            new_conv.append(buf)
            w_out = dn_w_out[j]
        x = _outproj_ln(geo, o, w_out, x, mod, ln_g[i, 0], ln_b[i, 0])
        x = _moe_ln(geo, x, mod, router_w, router_b, moe_w_gu, moe_w_down, i, ln_g[i, 1], ln_b[i, 1])
    return x, jnp.stack(new_ret), jnp.stack(new_dn), jnp.stack(new_conv)


def _run(x_prompt, x_sample, state_ret, state_dn, state_conv, c_prompt, c_sample, w_ada, b_ada, ln_g, ln_b,
         ret_w_in, ret_gn_g, ret_w_out, dn_w_in, dn_conv_w, dn_a_log, dn_dt_bias, dn_norm_g, dn_w_out,
         router_w, router_b, moe_w_gu, moe_w_down, *, past_len):
    Bp, Lp, _ = x_prompt.shape
    Bs, Ls, _ = x_sample.shape
    Ls_pad = -(-Ls // SUBLANES) * SUBLANES
    tm_p = min(512, Lp)
    geo_p = _Geo(Bp, Lp, Lp, 0, tm_p, False, BF16)
    Ts = Bs * Ls_pad
    geo_s = _Geo(Bs, Ls_pad, Ls, past_len, min(512, Ts), True, F32)

    dn_w_tail = jnp.pad(dn_w_in[:, :, DN_MAIN:], ((0, 0), (0, 0), (0, LANES - 2 * DN_V_HEADS))).astype(BF16)
    wts = (ln_g, ln_b, ret_w_in.astype(BF16), ret_gn_g, ret_w_out.astype(BF16),
           dn_w_in[:, :, :DN_MAIN].astype(BF16), dn_w_tail, dn_conv_w, dn_a_log, dn_dt_bias, dn_norm_g,
           dn_w_out.astype(BF16), router_w.T, router_b, moe_w_gu.astype(BF16), moe_w_down.astype(BF16))

    mods = _ada(jnp.concatenate([c_prompt, c_sample], axis=0), w_ada, b_ada)

    xp = x_prompt.reshape(Bp * Lp, D_MODEL)
    yp, ret_p, dn_p, conv_p = _trunk(geo_p, xp, mods[:, :Bp], None, None, None, wts,
                                     ret_tl=min(512, Lp), dn_tl=min(256, Lp))
    xs = jnp.pad(x_sample, ((0, 0), (0, Ls_pad - Ls), (0, 0))).reshape(Ts, D_MODEL)
    ys, ret_s, dn_s, conv_s = _trunk(geo_s, xs, mods[:, Bp:], state_ret, state_dn, state_conv, wts,
                                     ret_tl=Ls_pad, dn_tl=Ls_pad)
    y_prompt = yp.reshape(Bp, Lp, D_MODEL)
    y_sample = ys.reshape(Bs, Ls_pad, D_MODEL)[:, :Ls]
    return (y_prompt, y_sample, ret_p, ret_s, dn_p, dn_s, conv_p, conv_s)


def kernel(x_prompt, x_sample, state_ret, state_dn, state_conv, c_prompt, c_sample, w_ada, b_ada, ln_g, ln_b,
           ret_w_in, ret_gn_g, ret_w_out, dn_w_in, dn_conv_w, dn_a_log, dn_dt_bias, dn_norm_g, dn_w_out,
           router_w, router_b, moe_w_gu, moe_w_down):
    return _run(x_prompt, x_sample, state_ret, state_dn, state_conv, c_prompt, c_sample, w_ada, b_ada,
                ln_g, ln_b, ret_w_in, ret_gn_g, ret_w_out, dn_w_in, dn_conv_w, dn_a_log, dn_dt_bias,
                dn_norm_g, dn_w_out, router_w, router_b, moe_w_gu, moe_w_down, past_len=PAST_LEN)
```

```python
import functools
from typing import NamedTuple

import jax
import jax.numpy as jnp
from jax import lax
from jax.experimental import pallas as pl
from jax.experimental.pallas import tpu as pltpu

F32 = jnp.float32
BF16 = jnp.bfloat16

D_MODEL = 1024
DEPTH = 4
N_RET_LAYERS = (DEPTH + 1) // 2
N_DN_LAYERS = DEPTH // 2
PAST_LEN = 16384
RET_HEADS = 4
RET_DK = 256
RET_DV = 512
RET_QW = RET_HEADS * RET_DK
RET_VW = RET_HEADS * RET_DV
RET_CHUNK = 128
DN_QK_HEADS = 8
DN_V_HEADS = 16
DN_DK = 128
DN_DV = 128
DN_QW = DN_QK_HEADS * DN_DK
DN_VW = DN_V_HEADS * DN_DV
DN_CONV_CH = 2 * DN_QW + DN_VW
DN_CONV_W = 4
DN_CHUNK = 64
DN_MAIN = DN_CONV_CH + DN_VW
N_EXPERTS = 16
N_GROUPS = 4
EXPERTS_PER_GROUP = N_EXPERTS // N_GROUPS
D_EXPERT = 256
N_MOD = 6
ALPHA = (2.0 * DEPTH) ** 0.25
LN_EPS = 1e-5

LANES = 128
SUBLANES = 8
MXU_DIM = 256
VMEM_LIMIT = 56 * 1024 * 1024


class _Geo(NamedTuple):
    B: int
    L: int
    l_true: int
    pos0: int
    tm: int
    per_token: bool
    act_dtype: object

    @property
    def T(self):
        return self.B * self.L


def _cparams(sem):
    return pltpu.CompilerParams(dimension_semantics=sem, vmem_limit_bytes=VMEM_LIMIT)


def _mod_spec(geo, chunk):
    if geo.per_token:
        return pl.BlockSpec((geo.tm, D_MODEL), lambda *g: (g[0], chunk))
    tpr = geo.L // geo.tm
    return pl.BlockSpec((None, 1, D_MODEL), lambda *g: (g[0] // tpr, 0, chunk))


def _silu(x):
    hx = 0.5 * x
    return hx + hx * jnp.tanh(hx)


def _dot(a, b):
    return jnp.dot(a, b, preferred_element_type=F32)


def _dot_nt(a, b):
    return lax.dot_general(a, b, (((1,), (1,)), ((), ())), preferred_element_type=F32)


def _dot_tn(a, b):
    return lax.dot_general(a, b, (((0,), (0,)), ((), ())), preferred_element_type=F32)


def _layer_norm(z, g, b):
    mu = jnp.mean(z, axis=-1, keepdims=True)
    zc = z - mu
    var = jnp.mean(zc * zc, axis=-1, keepdims=True)
    return zc * lax.rsqrt(var + LN_EPS) * g + b


def _ada_kernel(c_ref, w_ref, b_ref, o_ref):
    cs = _silu(c_ref[...]).astype(BF16)
    o_ref[...] = _dot(cs, w_ref[...].astype(BF16)) + b_ref[...]


def _ada(c_all, w_ada, b_ada):
    R = c_all.shape[0]
    tn = 1024
    n_out = N_MOD * D_MODEL
    return pl.pallas_call(
        _ada_kernel,
        grid=(DEPTH, n_out // tn),
        in_specs=[
            pl.BlockSpec((R, D_MODEL), lambda l, j: (0, 0)),
            pl.BlockSpec((None, D_MODEL, tn), lambda l, j: (l, 0, j)),
            pl.BlockSpec((None, 1, tn), lambda l, j: (l, 0, j)),
        ],
        out_specs=pl.BlockSpec((None, R, tn), lambda l, j: (l, 0, j)),
        out_shape=jax.ShapeDtypeStruct((DEPTH, R, n_out), F32),
        compiler_params=_cparams(("parallel", "parallel")),
        name="ada_mod",
    )(c_all, w_ada, b_ada.reshape(DEPTH, 1, n_out))


def _inproj_kernel(*refs, has_tail, n_conv, tiles_per_row):
    refs = list(refs)
    x_ref, sc_ref, sh_ref, w_ref = refs[:4]
    del refs[:4]
    wt_ref = refs.pop(0) if has_tail else None
    cw_ref = refs.pop(0) if n_conv else None
    o_ref = refs.pop(0)
    ot_ref = refs.pop(0) if has_tail else None
    h_scr = refs.pop(0)
    i = pl.program_id(0)
    j = pl.program_id(1)

    @pl.when(j == 0)
    def _():
        h = x_ref[...] * (1.0 + sc_ref[...]) + sh_ref[...]
        hb = h.astype(BF16)
        h_scr[...] = hb
        if has_tail:
            ot_ref[...] = _dot(hb, wt_ref[...])

    if not n_conv:
        o_ref[...] = _dot(h_scr[...], w_ref[...]).astype(o_ref.dtype)
        return
    carry_scr, stage_scr = refs
    tm, tn = o_ref.shape
    W = DN_CONV_W
    P0 = DN_CONV_PAD
    cw = CONV_COL_CHUNK

    @pl.when(j < n_conv)
    def _():
        row_start = (i % tiles_per_row) == 0
        stage_scr[0:P0, :] = jnp.where(row_start, 0.0, carry_scr[j])
        for c0 in range(0, tn, cw):
            cs = slice(c0, c0 + cw)
            stage_scr[P0:P0 + tm, cs] = _dot(h_scr[...], w_ref[:, cs])
            y = stage_scr[P0 - (W - 1):P0 - (W - 1) + tm, cs] * cw_ref[0:1, cs]
            for t in range(1, W):
                y = y + stage_scr[P0 - (W - 1) + t:P0 - (W - 1) + t + tm, cs] * cw_ref[t:t + 1, cs]
            o_ref[:, cs] = _silu(y).astype(o_ref.dtype)
        carry_scr[j] = stage_scr[tm:tm + P0, :]

    @pl.when(j >= n_conv)
    def _():
        o_ref[...] = _dot(h_scr[...], w_ref[...]).astype(o_ref.dtype)


def _inproj(geo, x, mod, w, w_tail, tn, conv_w=None, conv_cols=0):
    T, tm = geo.T, geo.tm
    N = w.shape[1]
    has_tail = w_tail is not None
    n_conv = conv_cols // tn
    in_specs = [
        pl.BlockSpec((tm, D_MODEL), lambda i, j: (i, 0)),
        _mod_spec(geo, 1),
        _mod_spec(geo, 0),
        pl.BlockSpec((D_MODEL, tn), lambda i, j: (0, j)),
    ]
    args = [x, mod, mod, w]
    out_specs = [pl.BlockSpec((tm, tn), lambda i, j: (i, j))]
    out_shape = [jax.ShapeDtypeStruct((T, N), geo.act_dtype)]
    scratch = [pltpu.VMEM((tm, D_MODEL), BF16)]
    if has_tail:
        in_specs.append(pl.BlockSpec((D_MODEL, LANES), lambda i, j: (0, 0)))
        args.append(w_tail)
        out_specs.append(pl.BlockSpec((tm, LANES), lambda i, j: (i, 0)))
        out_shape.append(jax.ShapeDtypeStruct((T, LANES), F32))
    if n_conv:
        assert conv_cols == n_conv * tn and not geo.per_token
        in_specs.append(pl.BlockSpec((DN_CONV_W, tn), lambda i, j: (0, jnp.minimum(j, n_conv - 1))))
        args.append(conv_w)
        scratch += [pltpu.VMEM((n_conv, DN_CONV_PAD, tn), F32), pltpu.VMEM((DN_CONV_PAD + tm, tn), F32)]
    res = pl.pallas_call(
        functools.partial(_inproj_kernel, has_tail=has_tail, n_conv=n_conv, tiles_per_row=geo.L // tm),
        grid=(T // tm, N // tn),
        in_specs=in_specs,
        out_specs=out_specs,
        out_shape=out_shape,
        scratch_shapes=scratch,
        compiler_params=_cparams(("arbitrary", "arbitrary")),
        name="in_proj",
    )(*args)
    return res if has_tail else (res[0], None)


def _conv_tail_kernel(x_ref, sc_ref, sh_ref, w_ref, o_ref):
    h = x_ref[...] * (1.0 + sc_ref[...]) + sh_ref[...]
    o_ref[...] = _dot(h.astype(BF16), w_ref[...])


def _conv_tail(geo, x, mod, w, n_cols, tn):
    B, L = geo.B, geo.L
    blocks_per_row = L // SUBLANES
    return pl.pallas_call(
        _conv_tail_kernel,
        grid=(n_cols // tn, B),
        in_specs=[
            pl.BlockSpec((SUBLANES, D_MODEL), lambda j, b: (b * blocks_per_row + blocks_per_row - 1, 0)),
            pl.BlockSpec((None, 1, D_MODEL), lambda j, b: (b, 0, 1)),
            pl.BlockSpec((None, 1, D_MODEL), lambda j, b: (b, 0, 0)),
            pl.BlockSpec((D_MODEL, tn), lambda j, b: (0, j)),
        ],
        out_specs=pl.BlockSpec((None, SUBLANES, tn), lambda j, b: (b, 0, j)),
        out_shape=jax.ShapeDtypeStruct((B, SUBLANES, n_cols), F32),
        compiler_params=_cparams(("parallel", "parallel")),
        name="conv_tail",
    )(x, mod, mod, w)


def _call_stacked(kern, prev, first_stacked_out, *, grid, in_specs, args, out_specs, out_shape,
                  scratch_shapes, name):
    n_in = len(args)
    aliases = {}
    body = kern
    if prev is not None:
        n_prev = len(prev)
        in_specs = list(in_specs) + [pl.BlockSpec(memory_space=pl.ANY)] * n_prev
        args = list(args) + list(prev)
        aliases = {n_in + k: first_stacked_out + k for k in range(n_prev)}

        def body(*refs):
            kern(*refs[:n_in], *refs[n_in + n_prev:])

    return pl.pallas_call(
        body, grid=grid, in_specs=in_specs, out_specs=out_specs, out_shape=out_shape,
        scratch_shapes=scratch_shapes, input_output_aliases=aliases,
        compiler_params=_cparams(("parallel", "arbitrary")), name=name,
    )(*args)


def _pad_rows(x, rows):
    if x.shape[0] == rows:
        return x
    return jnp.concatenate([x, jnp.zeros((rows - x.shape[0], x.shape[1]), x.dtype)], axis=0)


def _ret_kernel(*refs, rows, nchunk, has_state):
    if has_state:
        (q_ref, k_ref, v_ref, g_ref, cos_ref, sin_ref, dintra_ref, qdec_ref, kdec_ref, sdec_ref, gn_ref,
         s0_ref, o_ref, s_out_ref, s_scr) = refs
    else:
        (q_ref, k_ref, v_ref, g_ref, cos_ref, sin_ref, dintra_ref, qdec_ref, kdec_ref, sdec_ref, gn_ref,
         o_ref, s_out_ref, s_scr) = refs
    i = pl.program_id(1)
    c = RET_CHUNK
    half = RET_DK // 2

    @pl.when(i == 0)
    def _():
        if has_state:
            s_scr[...] = s0_ref[...]
        else:
            s_scr[...] = jnp.zeros_like(s_scr)

    def rot(ref, r, h, cos, sin):
        x1 = ref[r, h * RET_DK:h * RET_DK + half].astype(F32)
        x2 = ref[r, h * RET_DK + half:(h + 1) * RET_DK].astype(F32)
        return jnp.concatenate([x1 * cos - x2 * sin, x2 * cos + x1 * sin], axis=1)

    for ci in range(nchunk):
        r = slice(ci * rows, (ci + 1) * rows)
        cos = cos_ref[r, :]
        sin = sin_ref[r, :]
        for h in range(RET_HEADS):
            vs = slice(h * RET_DV, (h + 1) * RET_DV)
            q = _pad_rows(rot(q_ref, r, h, cos, sin), c)
            k = _pad_rows(rot(k_ref, r, h, cos, sin), c) * (RET_DK ** -0.5)
            if rows == c:
                v = v_ref[r, vs].astype(BF16)
            else:
                v = _pad_rows(v_ref[r, vs].astype(F32), c).astype(BF16)
            s = s_scr[h]
            att = _dot_nt(q.astype(BF16), k.astype(BF16)) * dintra_ref[h]
            o = _dot(att.astype(BF16), v) + _dot((q * qdec_ref[h]).astype(BF16), s.astype(BF16))
            s_scr[h] = s * sdec_ref[h] + _dot_tn((k * kdec_ref[h]).astype(BF16), v)
            o = o[:rows]
            mu = jnp.mean(o, axis=-1, keepdims=True)
            oc = o - mu
            var = jnp.mean(oc * oc, axis=-1, keepdims=True)
            on = oc * lax.rsqrt(var + LN_EPS) * gn_ref[...]
            gate = g_ref[r, vs].astype(F32)
            o_ref[r, vs] = (on * _silu(gate)).astype(o_ref.dtype)

    @pl.when(i == pl.num_programs(1) - 1)
    def _():
        s_out_ref[...] = s_scr[...]


def _ret_tables(geo):
    c_true = min(RET_CHUNK, geo.l_true)
    half = RET_DK // 2
    inv_freq = 10000.0 ** (-jnp.linspace(0.0, 1.0, half, dtype=F32))
    pos = (geo.pos0 + jnp.arange(geo.L)).astype(F32)
    ang = pos[:, None] * inv_freq[None, :]
    lg = jnp.log(1.0 - 2.0 ** (-5.0 - jnp.arange(RET_HEADS, dtype=F32)))
    idx = jnp.arange(c_true, dtype=F32)
    diff = idx[:, None] - idx[None, :]
    dintra = jnp.exp(jnp.where(diff[None] >= 0, diff[None] * lg[:, None, None], -jnp.inf))
    qdec = jnp.exp((idx + 1.0)[None, :] * lg[:, None])
    kdec = jnp.exp((c_true - 1.0 - idx)[None, :] * lg[:, None])
    sdec = jnp.exp(c_true * lg)
    pad = RET_CHUNK - c_true
    dintra = jnp.pad(dintra, ((0, 0), (0, pad), (0, pad)))
    qdec = jnp.pad(qdec, ((0, 0), (0, pad)))[..., None]
    kdec = jnp.pad(kdec, ((0, 0), (0, pad)))[..., None]
    sdec = jnp.broadcast_to(sdec[:, None, None], (RET_HEADS, 1, RET_DV))
    return jnp.cos(ang), jnp.sin(ang), dintra, qdec, kdec, sdec


def _retention(geo, p, tables, gn_g, s_all, layer_j, prev, tl):
    B, L = geo.B, geo.L
    nblk = L // tl
    rows = min(tl, RET_CHUNK)
    nchunk = tl // rows
    has_state = s_all is not None
    cos, sin, dintra, qdec, kdec, sdec = tables
    rb = lambda b, i: b * nblk + i
    full3 = lambda b, i: (0, 0, 0)
    in_specs = [
        pl.BlockSpec((tl, RET_QW), lambda b, i: (rb(b, i), 0)),
        pl.BlockSpec((tl, RET_QW), lambda b, i: (rb(b, i), 1)),
        pl.BlockSpec((tl, RET_VW), lambda b, i: (rb(b, i), 1)),
        pl.BlockSpec((tl, RET_VW), lambda b, i: (rb(b, i), 2)),
        pl.BlockSpec((tl, RET_DK // 2), lambda b, i: (i, 0)),
        pl.BlockSpec((tl, RET_DK // 2), lambda b, i: (i, 0)),
        pl.BlockSpec((RET_HEADS, RET_CHUNK, RET_CHUNK), full3),
        pl.BlockSpec((RET_HEADS, RET_CHUNK, 1), full3),
        pl.BlockSpec((RET_HEADS, RET_CHUNK, 1), full3),
        pl.BlockSpec((RET_HEADS, 1, RET_DV), full3),
        pl.BlockSpec((1, RET_DV), lambda b, i: (0, 0)),
    ]
    args = [p, p, p, p, cos, sin, dintra, qdec, kdec, sdec, gn_g.reshape(1, RET_DV)]
    if has_state:
        in_specs.append(pl.BlockSpec((None, None, RET_HEADS, RET_DK, RET_DV),
                                     lambda b, i: (layer_j, b, 0, 0, 0)))
        args.append(s_all)
    o, s = _call_stacked(
        functools.partial(_ret_kernel, rows=rows, nchunk=nchunk, has_state=has_state),
        prev, first_stacked_out=1,
        grid=(B, nblk),
        in_specs=in_specs,
        args=args,
        out_specs=[
            pl.BlockSpec((tl, RET_VW), lambda b, i: (rb(b, i), 0)),
            pl.BlockSpec((None, None, RET_HEADS, RET_DK, RET_DV), lambda b, i: (layer_j, b, 0, 0, 0)),
        ],
        out_shape=[
            jax.ShapeDtypeStruct((geo.T, RET_VW), geo.act_dtype),
            jax.ShapeDtypeStruct((N_RET_LAYERS, B, RET_HEADS, RET_DK, RET_DV), F32),
        ],
        scratch_shapes=[pltpu.VMEM((RET_HEADS, RET_DK, RET_DV), F32)],
        name="retention",
    )
    return o, (s,)


DN_QUAD = MXU_DIM // DN_CHUNK
DN_CONV_PAD = SUBLANES
CONV_COL_CHUNK = 512


def _block_diag(x_cat, n, mask):
    return jnp.where(mask, jnp.concatenate([x_cat] * n, axis=0), 0.0)


def _dn_kernel(*refs, rows, nchunk, l_last, has_state):
    if has_state:
        (qkv_ref, z_ref, ab_ref, cw_ref, alog_ref, dtb_ref, ng_ref, s0_ref, buf_ref,
         o_ref, s_out_ref, buf_out_ref, s_scr, xp_scr, y_scr) = refs
    else:
        (qkv_ref, z_ref, ab_ref, alog_ref, dtb_ref, ng_ref, o_ref, s_out_ref, s_scr) = refs
    i = pl.program_id(1)
    c = DN_CHUNK
    tl = rows * nchunk
    npair = DN_V_HEADS // 2
    W = DN_CONV_W
    P0 = DN_CONV_PAD

    @pl.when(i == 0)
    def _():
        if has_state:
            for pr in range(npair):
                s_scr[pr] = jnp.concatenate([s0_ref[2 * pr], s0_ref[2 * pr + 1]], axis=1)
            xp_scr[P0 - (W - 1):P0, :] = buf_ref[...]
        else:
            s_scr[...] = jnp.zeros_like(s_scr)

    if has_state:
        xp_scr[P0:P0 + tl, :] = qkv_ref[...].astype(F32)
        y = xp_scr[P0 - (W - 1):P0 - (W - 1) + tl, :] * cw_ref[0:1, :]
        for t in range(1, W):
            y = y + xp_scr[P0 - (W - 1) + t:P0 - (W - 1) + t + tl, :] * cw_ref[t:t + 1, :]
        y_scr[0:tl, :] = _silu(y)
        if rows < c:
            y_scr[tl:, :] = jnp.zeros((y_scr.shape[0] - tl, DN_CONV_CH), F32)

        @pl.when(i == pl.num_programs(1) - 1)
        def _():
            buf_out_ref[...] = xp_scr[P0 + l_last - (W - 1):P0 + l_last, :]

        xp_scr[P0 - (W - 1):P0, :] = xp_scr[P0 + tl - (W - 1):P0 + tl, :]

        def ycols(r, lo, hi):
            return y_scr[r, lo:hi]
    else:
        def ycols(r, lo, hi):
            return qkv_ref[r, lo:hi].astype(F32)

    ri = lax.broadcasted_iota(jnp.int32, (c, c), 0)
    ci_ = lax.broadcasted_iota(jnp.int32, (c, c), 1)
    incl = ri >= ci_
    strict = ri > ci_
    eye = ri == ci_
    tri_f = incl.astype(F32)
    eye_cat = jnp.concatenate([eye.astype(F32)] * DN_QUAD, axis=1)
    bi = lax.broadcasted_iota(jnp.int32, (MXU_DIM, MXU_DIM), 0) // c
    bj = lax.broadcasted_iota(jnp.int32, (MXU_DIM, MXU_DIM), 1) // c
    bd_mask = bi == bj
    lane2 = lax.broadcasted_iota(jnp.int32, (DN_DK, 2 * DN_DV), 1)
    left = lane2 < DN_DV
    row_ok = lax.broadcasted_iota(jnp.int32, (c, 1), 0) < l_last if rows < c else None
    n_levels = c.bit_length() - 1

    nquad = DN_V_HEADS // DN_QUAD
    quads = [[DN_QUAD * q + t for t in range(DN_QUAD)] for q in range(nquad)]

    pre = []
    for ck in range(nchunk):
        r0 = ck * c if rows == c else 0
        r = slice(r0, r0 + c)
        ab = _pad_rows(ab_ref[ck * rows:(ck + 1) * rows, :], c)
        sp = ab + dtb_ref[...]
        g_full = -jnp.exp(alog_ref[...]) * (jnp.maximum(sp, 0.0) + jnp.log1p(jnp.exp(-jnp.abs(sp))))
        beta_full = jax.nn.sigmoid(ab)
        if row_ok is not None:
            g_full = jnp.where(row_ok, g_full, 0.0)
            beta_full = jnp.where(row_ok, beta_full, 0.0)
        G_full = jnp.dot(tri_f, g_full, preferred_element_type=F32, precision=lax.Precision.HIGHEST)

        kn, qn, KK, QK = {}, {}, {}, {}
        for j in range(DN_QK_HEADS):
            qh = ycols(r, j * DN_DK, (j + 1) * DN_DK)
            kh = ycols(r, DN_QW + j * DN_DK, DN_QW + (j + 1) * DN_DK)
            qn[j] = qh * lax.rsqrt(jnp.sum(qh * qh, axis=-1, keepdims=True) + 1e-6) * (DN_DK ** -0.5)
            kn[j] = kh * lax.rsqrt(jnp.sum(kh * kh, axis=-1, keepdims=True) + 1e-6)
        for j in range(DN_QK_HEADS):
            kb16 = kn[j].astype(BF16)
            kq = _dot_nt(jnp.concatenate([kb16, qn[j].astype(BF16)], axis=0), kb16)
            KK[j], QK[j] = kq[:c], kq[c:]

        A, att, eG, Gc, rhs = {}, {}, {}, {}, {}
        for h in range(DN_V_HEADS):
            j = h // 2
            Gc[h] = G_full[:, h:h + 1]
            bc = beta_full[:, DN_V_HEADS + h:DN_V_HEADS + h + 1]
            Gr = jnp.sum(jnp.where(eye, Gc[h], 0.0), axis=0, keepdims=True)
            dec = jnp.exp(jnp.where(incl, Gc[h] - Gr, -jnp.inf))
            A[h] = jnp.where(strict, bc * KK[j] * dec, 0.0)
            att[h] = QK[j] * dec
            eG[h] = jnp.exp(Gc[h])
            vh = ycols(r, 2 * DN_QW + h * DN_DV, 2 * DN_QW + (h + 1) * DN_DV)
            rhs[h] = jnp.concatenate([vh * bc, kn[j] * (bc * eG[h])], axis=1)

        X = [jnp.concatenate([A[h] for h in hs], axis=1) for hs in quads]
        Pm = [eye_cat - x for x in X]
        X = [_dot(x.astype(BF16), _block_diag(x, DN_QUAD, bd_mask).astype(BF16)) for x in X]
        for lvl in range(1, n_levels):
            Xw = [_block_diag(x, DN_QUAD, bd_mask).astype(BF16) for x in X]
            if lvl < n_levels - 1:
                both = [_dot(jnp.concatenate([X[q], Pm[q]], axis=0).astype(BF16), Xw[q]) for q in range(nquad)]
                X = [b[:c] for b in both]
                Pm = [Pm[q] + both[q][c:] for q in range(nquad)]
            else:
                Pm = [Pm[q] + _dot(Pm[q].astype(BF16), Xw[q]) for q in range(nquad)]
        sol = [_dot(_block_diag(Pm[q], DN_QUAD, bd_mask).astype(BF16),
                    jnp.concatenate([rhs[h] for h in quads[q]], axis=0).astype(BF16))
               for q in range(nquad)]
        att_bd = [_block_diag(jnp.concatenate([att[h] for h in quads[q]], axis=1), DN_QUAD, bd_mask).astype(BF16)
                  for q in range(nquad)]
        pre.append((kn, qn, Gc, eG, sol, att_bd))

    for ck in range(nchunk):
        kn, qn, Gc, eG, sol, att_bd = pre[ck]
        sol_h = {h: sol[h // DN_QUAD][c * (h % DN_QUAD):c * (h % DN_QUAD + 1)] for h in range(DN_V_HEADS)}
        u, oS = {}, {}
        for pr in range(npair):
            h0, h1 = 2 * pr, 2 * pr + 1
            S = s_scr[pr]
            Sbd = jnp.concatenate([jnp.where(left, S, 0.0), jnp.where(left, 0.0, S)], axis=0).astype(BF16)
            w_cat = jnp.concatenate([sol_h[h0][:, DN_DV:], sol_h[h1][:, DN_DV:]], axis=1)
            qe_cat = jnp.concatenate([qn[pr] * eG[h0], qn[pr] * eG[h1]], axis=1)
            R = _dot(jnp.concatenate([w_cat, qe_cat], axis=0).astype(BF16), Sbd)
            u[h0] = sol_h[h0][:, :DN_DV] - R[:c, :DN_DV]
            u[h1] = sol_h[h1][:, :DN_DV] - R[:c, DN_DV:]
            oS[h0] = R[c:, :DN_DV]
            oS[h1] = R[c:, DN_DV:]

        o_intra = [_dot(att_bd[q], jnp.concatenate([u[h] for h in quads[q]], axis=0).astype(BF16))
                   for q in range(nquad)]

        for pr in range(npair):
            h0, h1 = 2 * pr, 2 * pr + 1
            gl0 = Gc[h0][c - 1:c, :]
            gl1 = Gc[h1][c - 1:c, :]
            du = jnp.concatenate([u[h0] * jnp.exp(gl0 - Gc[h0]), u[h1] * jnp.exp(gl1 - Gc[h1])], axis=1)
            sdec = jnp.where(left[0:1, :], jnp.exp(gl0), jnp.exp(gl1))
            s_scr[pr] = s_scr[pr] * sdec + _dot_tn(kn[pr].astype(BF16), du.astype(BF16))

        zr = slice(ck * rows, (ck + 1) * rows)
        for h in range(DN_V_HEADS):
            t = h % DN_QUAD
            o = (oS[h] + o_intra[h // DN_QUAD][c * t:c * (t + 1)])[:rows]
            o = o * lax.rsqrt(jnp.mean(o * o, axis=-1, keepdims=True) + 1e-6) * ng_ref[...]
            hs = slice(h * DN_DV, (h + 1) * DN_DV)
            z = z_ref[zr, hs].astype(F32)
            o_ref[zr, hs] = (o * _silu(z)).astype(o_ref.dtype)

    @pl.when(i == pl.num_programs(1) - 1)
    def _():
        for pr in range(npair):
            s_out_ref[2 * pr] = s_scr[pr][:, :DN_DV]
            s_out_ref[2 * pr + 1] = s_scr[pr][:, DN_DV:]


def _deltanet(geo, p, ab, conv_w, a_log, dt_bias, norm_g, s_all, buf_all, layer_j, prev, tl):
    B, L = geo.B, geo.L
    nblk = L // tl
    rows = min(tl, DN_CHUNK)
    nchunk = tl // rows
    l_last = geo.l_true - (nblk - 1) * tl
    has_state = s_all is not None
    rb = lambda b, i: b * nblk + i
    pad16 = lambda v: jnp.pad(v.reshape(1, DN_V_HEADS), ((0, 0), (0, LANES - DN_V_HEADS)))
    row1 = lambda n: pl.BlockSpec((1, n), lambda b, i: (0, 0))
    in_specs = [
        pl.BlockSpec((tl, DN_CONV_CH), lambda b, i: (rb(b, i), 0)),
        pl.BlockSpec((tl, DN_VW), lambda b, i: (rb(b, i), DN_CONV_CH // DN_VW)),
        pl.BlockSpec((tl, LANES), lambda b, i: (rb(b, i), 0)),
    ]
    args = [p, p, ab]
    if has_state:
        in_specs.append(pl.BlockSpec((DN_CONV_W, DN_CONV_CH), lambda b, i: (0, 0)))
        args.append(conv_w)
    in_specs += [row1(LANES), row1(LANES), row1(DN_DV)]
    args += [pad16(a_log), pad16(dt_bias), norm_g.reshape(1, DN_DV)]
    out_specs = [
        pl.BlockSpec((tl, DN_VW), lambda b, i: (rb(b, i), 0)),
        pl.BlockSpec((None, None, DN_V_HEADS, DN_DK, DN_DV), lambda b, i: (layer_j, b, 0, 0, 0)),
    ]
    out_shape = [
        jax.ShapeDtypeStruct((geo.T, DN_VW), geo.act_dtype),
        jax.ShapeDtypeStruct((N_DN_LAYERS, B, DN_V_HEADS, DN_DK, DN_DV), F32),
    ]
    scratch = [pltpu.VMEM((DN_V_HEADS // 2, DN_DK, 2 * DN_DV), F32)]
    if has_state:
        in_specs += [
            pl.BlockSpec((None, None, DN_V_HEADS, DN_DK, DN_DV), lambda b, i: (layer_j, b, 0, 0, 0)),
            pl.BlockSpec((None, None, DN_CONV_W - 1, DN_CONV_CH), lambda b, i: (layer_j, b, 0, 0)),
        ]
        args += [s_all, buf_all]
        out_specs.append(pl.BlockSpec((None, None, DN_CONV_W - 1, DN_CONV_CH), lambda b, i: (layer_j, b, 0, 0)))
        out_shape.append(jax.ShapeDtypeStruct((N_DN_LAYERS, B, DN_CONV_W - 1, DN_CONV_CH), F32))
        scratch += [pltpu.VMEM((DN_CONV_PAD + tl, DN_CONV_CH), F32),
                    pltpu.VMEM((max(tl, DN_CHUNK), DN_CONV_CH), F32)]
    kern = functools.partial(_dn_kernel, rows=rows, nchunk=nchunk, l_last=l_last, has_state=has_state)
    res = _call_stacked(kern, prev, first_stacked_out=1, grid=(B, nblk), in_specs=in_specs, args=args,
                        out_specs=out_specs, out_shape=out_shape, scratch_shapes=scratch, name="deltanet")
    return res[0], tuple(res[1:])


def _outproj_kernel(o_ref, w_ref, x_ref, gate_ref, g_ref, b_ref, out_ref):
    y = _dot(o_ref[...].astype(BF16), w_ref[...])
    z = ALPHA * x_ref[...] + (1.0 + gate_ref[...]) * y
    out_ref[...] = _layer_norm(z, g_ref[...], b_ref[...])


def _outproj_ln(geo, o, w_out, x, mod, ln_g, ln_b):
    T, tm = geo.T, geo.tm
    K = w_out.shape[0]
    vec = pl.BlockSpec((1, D_MODEL), lambda i: (0, 0))
    return pl.pallas_call(
        _outproj_kernel,
        grid=(T // tm,),
        in_specs=[
            pl.BlockSpec((tm, K), lambda i: (i, 0)),
            pl.BlockSpec((K, D_MODEL), lambda i: (0, 0)),
            pl.BlockSpec((tm, D_MODEL), lambda i: (i, 0)),
            _mod_spec(geo, 2),
            vec, vec,
        ],
        out_specs=pl.BlockSpec((tm, D_MODEL), lambda i: (i, 0)),
        out_shape=jax.ShapeDtypeStruct((T, D_MODEL), F32),
        compiler_params=_cparams(("parallel",)),
        name="out_proj_ln",
    )(o, w_out, x, mod, ln_g.reshape(1, D_MODEL), ln_b.reshape(1, D_MODEL))


MOE_ROWS = 144


def _moe_blocks(tm):
    nb = (tm - N_GROUPS) // MOE_ROWS + N_GROUPS
    ns = -(-(nb * MOE_ROWS) // LANES) * LANES
    return nb, ns


def _first_max(v, row, valid):
    m = jnp.max(jnp.where(valid, v, -jnp.inf), axis=0, keepdims=True)
    idx = jnp.min(jnp.where(valid & (v == m), row, float(N_EXPERTS)), axis=0, keepdims=True)
    return m, idx


def _route(h, wrt, br):
    logits = lax.dot_general(wrt, h, (((1,), (1,)), ((), ())), preferred_element_type=F32,
                             precision=lax.Precision.HIGHEST)
    mx = jnp.max(logits, axis=0, keepdims=True)
    ex = jnp.exp(logits - mx)
    probs = ex / jnp.sum(ex, axis=0, keepdims=True)
    sel = probs + br
    row_i = lax.broadcasted_iota(jnp.int32, sel.shape, 0)
    grp = (row_i // EXPERTS_PER_GROUP).astype(F32)
    row = row_i.astype(F32)
    best = None
    gsel = None
    for g in range(N_GROUPS):
        in_g = grp == g
        m1, i1 = _first_max(sel, row, in_g)
        m2, _ = _first_max(sel, row, in_g & (row != i1))
        score = m1 + m2
        if g == 0:
            best, gsel = score, jnp.zeros_like(i1)
        else:
            better = score > best
            gsel = jnp.where(better, float(g), gsel)
            best = jnp.where(better, score, best)
    in_grp = grp == gsel
    _, i1 = _first_max(sel, row, in_grp)
    _, i2 = _first_max(sel, row, in_grp & (row != i1))
    w1 = jnp.sum(jnp.where(row == i1, probs, 0.0), axis=0, keepdims=True)
    w2 = jnp.sum(jnp.where(row == i2, probs, 0.0), axis=0, keepdims=True)
    wsum = w1 + w2
    gates = jnp.where(row == i1, w1 / wsum, 0.0) + jnp.where(row == i2, w2 / wsum, 0.0)
    return gsel, gates


def _router_kernel(x_ref, sc_ref, sh_ref, wrt_ref, br_ref, posr_ref, posc_ref, gates_ref, tbl_ref):
    tm = x_ref.shape[0]
    h = x_ref[...] * (1.0 + sc_ref[...]) + sh_ref[...]
    gsel, gates = _route(h, wrt_ref[...], br_ref[...])
    gates_ref[...] = gates
    grow = lax.broadcasted_iota(jnp.int32, (SUBLANES, tm), 0).astype(F32)
    og = grow == gsel
    ti = lax.broadcasted_iota(jnp.int32, (tm, tm), 0)
    tj = lax.broadcasted_iota(jnp.int32, (tm, tm), 1)
    before = jnp.where(ti < tj, 1.0, 0.0).astype(BF16)
    ogf = jnp.where(og, 1.0, 0.0)
    rank = _dot(ogf.astype(BF16), before)
    cnt = jnp.sum(ogf, axis=1, keepdims=True)
    nblk = jnp.zeros_like(cnt)
    for kb in range(-(-tm // MOE_ROWS)):
        nblk = nblk + jnp.where(cnt > float(kb * MOE_ROWS), 1.0, 0.0)
    seg = nblk * float(MOE_ROWS)
    block_start = lax.broadcasted_iota(jnp.int32, (1, LANES), 1).astype(F32) * float(MOE_ROWS)
    off = jnp.zeros((1, 1), F32)
    pos = jnp.zeros((1, tm), F32)
    tbl = jnp.zeros((1, LANES), F32)
    for g in range(N_GROUPS):
        pos = pos + jnp.where(gsel == float(g), off + rank[g:g + 1, :], 0.0)
        off = off + seg[g:g + 1, :]
        tbl = tbl + jnp.where(block_start >= off, 1.0, 0.0)
    posr_ref[...] = pos
    posc_ref[...] = jnp.sum(jnp.where(ti == tj, pos, 0.0), axis=1, keepdims=True)
    tbl_ref[...] = tbl.astype(jnp.int32)


def _expert_kernel(tbl_ref, x_ref, sc_ref, sh_ref, gt_ref, posr_ref, posc_ref, gates_ref, wgu_ref, wd_ref,
                   g_ref, b_ref, out_ref, hs_scr, gs_scr, ys_scr, *, nb):
    i = pl.program_id(0)
    tm = x_ref.shape[0]
    ns = hs_scr.shape[0]
    R = MOE_ROWS
    x = x_ref[...]
    hb = (x * (1.0 + sc_ref[...]) + sh_ref[...]).astype(BF16)
    slot = lax.broadcasted_iota(jnp.int32, (ns, tm), 0).astype(F32)
    P = jnp.where(slot == posr_ref[...], 1.0, 0.0).astype(BF16)
    hs_scr[...] = _dot(P, hb).astype(BF16)
    gates = gates_ref[...]
    g_hi = gates.astype(BF16)
    g_lo = (gates - g_hi.astype(F32)).astype(BF16)
    gs_scr[...] = _dot_nt(P, g_hi) + _dot_nt(P, g_lo)
    if nb * R < ns:
        ys_scr[nb * R:, :] = jnp.zeros((ns - nb * R, D_MODEL), BF16)

    for b in range(nb):
        rs = slice(b * R, (b + 1) * R)
        g = tbl_ref[i, b]

        @pl.when(g < N_GROUPS)
        def _():
            gsb = gs_scr[rs, :]
            lane = lax.broadcasted_iota(jnp.int32, gsb.shape, 1)
            gu = _dot(hs_scr[rs, :], wgu_ref[g])
            acts = []
            for e in range(EXPERTS_PER_GROUP):
                c0 = e * 2 * D_EXPERT
                gate = jnp.sum(jnp.where(lane == g * EXPERTS_PER_GROUP + e, gsb, 0.0), axis=1, keepdims=True)
                act = _silu(gu[:, c0:c0 + D_EXPERT]) * gu[:, c0 + D_EXPERT:c0 + 2 * D_EXPERT] * gate
                acts.append(act.astype(BF16))
            ys_scr[rs, :] = _dot(jnp.concatenate(acts, axis=1), wd_ref[g]).astype(BF16)

        @pl.when(g >= N_GROUPS)
        def _():
            ys_scr[rs, :] = jnp.zeros((R, D_MODEL), BF16)

    slot_t = lax.broadcasted_iota(jnp.int32, (tm, ns), 1).astype(F32)
    PT = jnp.where(slot_t == posc_ref[...], 1.0, 0.0).astype(BF16)
    y = _dot(PT, ys_scr[...])
    z = ALPHA * x + (1.0 + gt_ref[...]) * y
    out_ref[...] = _layer_norm(z, g_ref[...], b_ref[...])


def _moe_ln(geo, x, mod, w_router_t, b_router, w_gu, w_down, layer, ln_g, ln_b):
    T, tm = geo.T, geo.tm
    nt = T // tm
    nb, ns = _moe_blocks(tm)
    posr, posc, gates, tbl = pl.pallas_call(
        _router_kernel,
        grid=(nt,),
        in_specs=[
            pl.BlockSpec((tm, D_MODEL), lambda i: (i, 0)),
            _mod_spec(geo, 4),
            _mod_spec(geo, 3),
            pl.BlockSpec((N_EXPERTS, D_MODEL), lambda i: (0, 0)),
            pl.BlockSpec((N_EXPERTS, 1), lambda i: (0, 0)),
        ],
        out_specs=[
            pl.BlockSpec((None, 1, tm), lambda i: (i, 0, 0)),
            pl.BlockSpec((tm, 1), lambda i: (i, 0)),
            pl.BlockSpec((N_EXPERTS, tm), lambda i: (0, i)),
            pl.BlockSpec((None, 1, LANES), lambda i: (i, 0, 0)),
        ],
        out_shape=[
            jax.ShapeDtypeStruct((nt, 1, tm), F32),
            jax.ShapeDtypeStruct((T, 1), F32),
            jax.ShapeDtypeStruct((N_EXPERTS, T), F32),
            jax.ShapeDtypeStruct((nt, 1, LANES), jnp.int32),
        ],
        compiler_params=_cparams(("parallel",)),
        name="moe_router",
    )(x, mod, mod, w_router_t, b_router.reshape(N_EXPERTS, 1))

    vec = pl.BlockSpec((1, D_MODEL), lambda i, t: (0, 0))
    resident = dict(pipeline_mode=pl.Buffered(1))
    grid_spec = pltpu.PrefetchScalarGridSpec(
        num_scalar_prefetch=1,
        grid=(nt,),
        in_specs=[
            pl.BlockSpec((tm, D_MODEL), lambda i, t: (i, 0)),
            _mod_spec(geo, 4),
            _mod_spec(geo, 3),
            _mod_spec(geo, 5),
            pl.BlockSpec((None, 1, tm), lambda i, t: (i, 0, 0)),
            pl.BlockSpec((tm, 1), lambda i, t: (i, 0)),
            pl.BlockSpec((N_EXPERTS, tm), lambda i, t: (0, i)),
            pl.BlockSpec((None, N_GROUPS, D_MODEL, EXPERTS_PER_GROUP * 2 * D_EXPERT),
                         lambda i, t: (layer, 0, 0, 0), **resident),
            pl.BlockSpec((None, N_GROUPS, EXPERTS_PER_GROUP * D_EXPERT, D_MODEL),
                         lambda i, t: (layer, 0, 0, 0), **resident),
            vec, vec,
        ],
        out_specs=pl.BlockSpec((tm, D_MODEL), lambda i, t: (i, 0)),
        scratch_shapes=[
            pltpu.VMEM((ns, D_MODEL), BF16),
            pltpu.VMEM((ns, N_EXPERTS), F32),
            pltpu.VMEM((ns, D_MODEL), BF16),
        ],
    )
    return pl.pallas_call(
        functools.partial(_expert_kernel, nb=nb),
        grid_spec=grid_spec,
        out_shape=jax.ShapeDtypeStruct((T, D_MODEL), F32),
        compiler_params=_cparams(("parallel",)),
        name="moe_experts_ln",
    )(tbl.reshape(nt, LANES), x, mod, mod, mod, posr, posc, gates, w_gu, w_down,
      ln_g.reshape(1, D_MODEL), ln_b.reshape(1, D_MODEL))


def _trunk(geo, x, mods, s_ret, s_dn, s_conv, wts, ret_tl, dn_tl):
    (ln_g, ln_b, ret_w_in, ret_gn_g, ret_w_out, dn_w_main, dn_w_tail, dn_conv_w, dn_a_log, dn_dt_bias,
     dn_norm_g, dn_w_out, router_w, router_b, moe_w_gu, moe_w_down) = wts
    tables = _ret_tables(geo)
    decode = s_dn is not None
    ret_stack, dn_stack, conv_tails = None, None, []
    for i in range(DEPTH):
        if geo.per_token:
            mod = jnp.repeat(mods[i], geo.L, axis=0)
        else:
            mod = mods[i].reshape(geo.B, 1, N_MOD * D_MODEL)
        j = i // 2
        if i % 2 == 0:
            p, _ = _inproj(geo, x, mod, ret_w_in[j], None, tn=1536)
            o, ret_stack = _retention(geo, p, tables, ret_gn_g[j], s_ret, j, ret_stack, ret_tl)
            w_out = ret_w_out[j]
        else:
            if decode:
                p, ab = _inproj(geo, x, mod, dn_w_main[j], dn_w_tail[j], tn=1536)
            else:
                p, ab = _inproj(geo, x, mod, dn_w_main[j], dn_w_tail[j], tn=2048,
                                conv_w=dn_conv_w[j], conv_cols=DN_CONV_CH)
                conv_tails.append(_conv_tail(geo, x, mod, dn_w_main[j], DN_CONV_CH, tn=2048))
            o, dn_stack = _deltanet(geo, p, ab, dn_conv_w[j], dn_a_log[j], dn_dt_bias[j], dn_norm_g[j],
                                    s_dn, s_conv, j, dn_stack, dn_tl)
            w_out = dn_w_out[j]
        x = _outproj_ln(geo, o, w_out, x, mod, ln_g[i, 0], ln_b[i, 0])
        x = _moe_ln(geo, x, mod, router_w, router_b, moe_w_gu, moe_w_down, i, ln_g[i, 1], ln_b[i, 1])
    if decode:
        new_dn, new_conv = dn_stack
    else:
        new_dn = dn_stack[0]
        new_conv = jnp.stack(conv_tails)[:, :, SUBLANES - (DN_CONV_W - 1):]
    return x, ret_stack[0], new_dn, new_conv


def _run(x_prompt, x_sample, state_ret, state_dn, state_conv, c_prompt, c_sample, w_ada, b_ada, ln_g, ln_b,
         ret_w_in, ret_gn_g, ret_w_out, dn_w_in, dn_conv_w, dn_a_log, dn_dt_bias, dn_norm_g, dn_w_out,
         router_w, router_b, moe_w_gu, moe_w_down, *, past_len):
    Bp, Lp, _ = x_prompt.shape
    Bs, Ls, _ = x_sample.shape
    Ls_pad = -(-Ls // SUBLANES) * SUBLANES
    tm_p = min(512, Lp)
    geo_p = _Geo(Bp, Lp, Lp, 0, tm_p, False, BF16)
    Ts = Bs * Ls_pad
    geo_s = _Geo(Bs, Ls_pad, Ls, past_len, min(512, Ts), True, F32)

    dn_w_tail = jnp.pad(dn_w_in[:, :, DN_MAIN:], ((0, 0), (0, 0), (0, LANES - 2 * DN_V_HEADS))).astype(BF16)
    w_gu_grp = moe_w_gu.astype(BF16).reshape(DEPTH, N_GROUPS, EXPERTS_PER_GROUP, D_MODEL, 2 * D_EXPERT)
    w_gu_grp = jnp.transpose(w_gu_grp, (0, 1, 3, 2, 4)).reshape(
        DEPTH, N_GROUPS, D_MODEL, EXPERTS_PER_GROUP * 2 * D_EXPERT)
    w_down_grp = moe_w_down.astype(BF16).reshape(DEPTH, N_GROUPS, EXPERTS_PER_GROUP * D_EXPERT, D_MODEL)
    wts = (ln_g, ln_b, ret_w_in.astype(BF16), ret_gn_g, ret_w_out.astype(BF16),
           dn_w_in[:, :, :DN_MAIN].astype(BF16), dn_w_tail, dn_conv_w, dn_a_log, dn_dt_bias, dn_norm_g,
           dn_w_out.astype(BF16), router_w.T, router_b, w_gu_grp, w_down_grp)

    mods = _ada(jnp.concatenate([c_prompt, c_sample], axis=0), w_ada, b_ada)

    xp = x_prompt.reshape(Bp * Lp, D_MODEL)
    yp, ret_p, dn_p, conv_p = _trunk(geo_p, xp, mods[:, :Bp], None, None, None, wts,
                                     ret_tl=min(512, Lp), dn_tl=min(256, Lp))
    xs = jnp.pad(x_sample, ((0, 0), (0, Ls_pad - Ls), (0, 0))).reshape(Ts, D_MODEL)
    ys, ret_s, dn_s, conv_s = _trunk(geo_s, xs, mods[:, Bp:], state_ret, state_dn, state_conv, wts,
                                     ret_tl=Ls_pad, dn_tl=Ls_pad)
    y_prompt = yp.reshape(Bp, Lp, D_MODEL)
    y_sample = ys.reshape(Bs, Ls_pad, D_MODEL)[:, :Ls]
    return (y_prompt, y_sample, ret_p, ret_s, dn_p, dn_s, conv_p, conv_s)


def kernel(x_prompt, x_sample, state_ret, state_dn, state_conv, c_prompt, c_sample, w_ada, b_ada, ln_g, ln_b,
           ret_w_in, ret_gn_g, ret_w_out, dn_w_in, dn_conv_w, dn_a_log, dn_dt_bias, dn_norm_g, dn_w_out,
           router_w, router_b, moe_w_gu, moe_w_down):
    return _run(x_prompt, x_sample, state_ret, state_dn, state_conv, c_prompt, c_sample, w_ada, b_ada,
                ln_g, ln_b, ret_w_in, ret_gn_g, ret_w_out, dn_w_in, dn_conv_w, dn_a_log, dn_dt_bias,
                dn_norm_g, dn_w_out, router_w, router_b, moe_w_gu, moe_w_down, past_len=PAST_LEN)
```

```python
import functools
from typing import NamedTuple

import jax
import jax.numpy as jnp
from jax import lax
from jax.experimental import pallas as pl
from jax.experimental.pallas import tpu as pltpu

F32 = jnp.float32
BF16 = jnp.bfloat16

D_MODEL = 1024
DEPTH = 4
N_RET_LAYERS = (DEPTH + 1) // 2
N_DN_LAYERS = DEPTH // 2
PAST_LEN = 16384
RET_HEADS = 4
RET_DK = 256
RET_DV = 512
RET_QW = RET_HEADS * RET_DK
RET_VW = RET_HEADS * RET_DV
RET_CHUNK = 128
DN_QK_HEADS = 8
DN_V_HEADS = 16
DN_DK = 128
DN_DV = 128
DN_QW = DN_QK_HEADS * DN_DK
DN_VW = DN_V_HEADS * DN_DV
DN_CONV_CH = 2 * DN_QW + DN_VW
DN_CONV_W = 4
DN_CHUNK = 64
DN_DECODE_CHUNK = 16
DN_MAIN = DN_CONV_CH + DN_VW
N_EXPERTS = 16
N_GROUPS = 4
EXPERTS_PER_GROUP = N_EXPERTS // N_GROUPS
D_EXPERT = 256
N_MOD = 6
ALPHA = (2.0 * DEPTH) ** 0.25
LN_EPS = 1e-5

LANES = 128
SUBLANES = 8
MXU_DIM = 256
VMEM_LIMIT = 56 * 1024 * 1024


class _Geo(NamedTuple):
    B: int
    L: int
    l_true: int
    pos0: int
    tm: int
    per_token: bool
    act_dtype: object

    @property
    def T(self):
        return self.B * self.L


def _cparams(sem):
    return pltpu.CompilerParams(dimension_semantics=sem, vmem_limit_bytes=VMEM_LIMIT)


def _mod_spec(geo, chunk):
    if geo.per_token:
        return pl.BlockSpec((geo.tm, D_MODEL), lambda *g: (g[0], chunk))
    tpr = geo.L // geo.tm
    return pl.BlockSpec((None, 1, D_MODEL), lambda *g: (g[0] // tpr, 0, chunk))


def _silu(x):
    hx = 0.5 * x
    return hx + hx * jnp.tanh(hx)


def _dot(a, b):
    return jnp.dot(a, b, preferred_element_type=F32)


def _dot_nt(a, b):
    return lax.dot_general(a, b, (((1,), (1,)), ((), ())), preferred_element_type=F32)


def _dot_tn(a, b):
    return lax.dot_general(a, b, (((0,), (0,)), ((), ())), preferred_element_type=F32)


def _layer_norm(z, g, b):
    mu = jnp.mean(z, axis=-1, keepdims=True)
    zc = z - mu
    var = jnp.mean(zc * zc, axis=-1, keepdims=True)
    return zc * lax.rsqrt(var + LN_EPS) * g + b


def _ada_kernel(c_ref, w_ref, b_ref, o_ref):
    cs = _silu(c_ref[...]).astype(BF16)
    o_ref[...] = _dot(cs, w_ref[...].astype(BF16)) + b_ref[...]


def _ada(c_all, w_ada, b_ada):
    R = c_all.shape[0]
    tn = 1024
    n_out = N_MOD * D_MODEL
    return pl.pallas_call(
        _ada_kernel,
        grid=(DEPTH, n_out // tn),
        in_specs=[
            pl.BlockSpec((R, D_MODEL), lambda l, j: (0, 0)),
            pl.BlockSpec((None, D_MODEL, tn), lambda l, j: (l, 0, j)),
            pl.BlockSpec((None, 1, tn), lambda l, j: (l, 0, j)),
        ],
        out_specs=pl.BlockSpec((None, R, tn), lambda l, j: (l, 0, j)),
        out_shape=jax.ShapeDtypeStruct((DEPTH, R, n_out), F32),
        compiler_params=_cparams(("parallel", "parallel")),
        name="ada_mod",
    )(c_all, w_ada, b_ada.reshape(DEPTH, 1, n_out))


def _inproj_kernel(*refs, has_tail, n_conv, tiles_per_row):
    refs = list(refs)
    x_ref, sc_ref, sh_ref, w_ref = refs[:4]
    del refs[:4]
    wt_ref = refs.pop(0) if has_tail else None
    cw_ref = refs.pop(0) if n_conv else None
    o_ref = refs.pop(0)
    ot_ref = refs.pop(0) if has_tail else None
    h_scr = refs.pop(0)
    i = pl.program_id(0)
    j = pl.program_id(1)

    @pl.when(j == 0)
    def _():
        h = x_ref[...] * (1.0 + sc_ref[...]) + sh_ref[...]
        hb = h.astype(BF16)
        h_scr[...] = hb
        if has_tail:
            ot_ref[...] = _dot(hb, wt_ref[...])

    if not n_conv:
        o_ref[...] = _dot(h_scr[...], w_ref[...]).astype(o_ref.dtype)
        return
    carry_scr, stage_scr = refs
    tm, tn = o_ref.shape
    W = DN_CONV_W
    P0 = DN_CONV_PAD
    cw = CONV_COL_CHUNK

    @pl.when(j < n_conv)
    def _():
        row_start = (i % tiles_per_row) == 0
        stage_scr[0:P0, :] = jnp.where(row_start, 0.0, carry_scr[j])
        for c0 in range(0, tn, cw):
            cs = slice(c0, c0 + cw)
            stage_scr[P0:P0 + tm, cs] = _dot(h_scr[...], w_ref[:, cs])
            y = stage_scr[P0 - (W - 1):P0 - (W - 1) + tm, cs] * cw_ref[0:1, cs]
            for t in range(1, W):
                y = y + stage_scr[P0 - (W - 1) + t:P0 - (W - 1) + t + tm, cs] * cw_ref[t:t + 1, cs]
            o_ref[:, cs] = _silu(y).astype(o_ref.dtype)
        carry_scr[j] = stage_scr[tm:tm + P0, :]

    @pl.when(j >= n_conv)
    def _():
        o_ref[...] = _dot(h_scr[...], w_ref[...]).astype(o_ref.dtype)


def _inproj(geo, x, mod, w, w_tail, tn, conv_w=None, conv_cols=0, tm=None):
    if tm is not None:
        geo = geo._replace(tm=tm)
    T, tm = geo.T, geo.tm
    N = w.shape[1]
    has_tail = w_tail is not None
    n_conv = conv_cols // tn
    in_specs = [
        pl.BlockSpec((tm, D_MODEL), lambda i, j: (i, 0)),
        _mod_spec(geo, 1),
        _mod_spec(geo, 0),
        pl.BlockSpec((D_MODEL, tn), lambda i, j: (0, j)),
    ]
    args = [x, mod, mod, w]
    out_specs = [pl.BlockSpec((tm, tn), lambda i, j: (i, j))]
    out_shape = [jax.ShapeDtypeStruct((T, N), geo.act_dtype)]
    scratch = [pltpu.VMEM((tm, D_MODEL), BF16)]
    if has_tail:
        in_specs.append(pl.BlockSpec((D_MODEL, LANES), lambda i, j: (0, 0)))
        args.append(w_tail)
        out_specs.append(pl.BlockSpec((tm, LANES), lambda i, j: (i, 0)))
        out_shape.append(jax.ShapeDtypeStruct((T, LANES), F32))
    if n_conv:
        assert conv_cols == n_conv * tn and not geo.per_token
        in_specs.append(pl.BlockSpec((DN_CONV_W, tn), lambda i, j: (0, jnp.minimum(j, n_conv - 1))))
        args.append(conv_w)
        scratch += [pltpu.VMEM((n_conv, DN_CONV_PAD, tn), F32), pltpu.VMEM((DN_CONV_PAD + tm, tn), F32)]
    res = pl.pallas_call(
        functools.partial(_inproj_kernel, has_tail=has_tail, n_conv=n_conv, tiles_per_row=geo.L // tm),
        grid=(T // tm, N // tn),
        in_specs=in_specs,
        out_specs=out_specs,
        out_shape=out_shape,
        scratch_shapes=scratch,
        compiler_params=_cparams(("arbitrary", "arbitrary")),
        name="in_proj",
    )(*args)
    return res if has_tail else (res[0], None)


def _conv_tail_kernel(x_ref, sc_ref, sh_ref, w_ref, o_ref):
    h = x_ref[...] * (1.0 + sc_ref[...]) + sh_ref[...]
    o_ref[...] = _dot(h.astype(BF16), w_ref[...])


def _conv_tail(geo, x, mod, w, n_cols, tn):
    B, L = geo.B, geo.L
    blocks_per_row = L // SUBLANES
    return pl.pallas_call(
        _conv_tail_kernel,
        grid=(n_cols // tn, B),
        in_specs=[
            pl.BlockSpec((SUBLANES, D_MODEL), lambda j, b: (b * blocks_per_row + blocks_per_row - 1, 0)),
            pl.BlockSpec((None, 1, D_MODEL), lambda j, b: (b, 0, 1)),
            pl.BlockSpec((None, 1, D_MODEL), lambda j, b: (b, 0, 0)),
            pl.BlockSpec((D_MODEL, tn), lambda j, b: (0, j)),
        ],
        out_specs=pl.BlockSpec((None, SUBLANES, tn), lambda j, b: (b, 0, j)),
        out_shape=jax.ShapeDtypeStruct((B, SUBLANES, n_cols), F32),
        compiler_params=_cparams(("parallel", "parallel")),
        name="conv_tail",
    )(x, mod, mod, w)


def _call_stacked(kern, prev, first_stacked_out, *, grid, in_specs, args, out_specs, out_shape,
                  scratch_shapes, name):
    n_in = len(args)
    aliases = {}
    body = kern
    if prev is not None:
        n_prev = len(prev)
        in_specs = list(in_specs) + [pl.BlockSpec(memory_space=pl.ANY)] * n_prev
        args = list(args) + list(prev)
        aliases = {n_in + k: first_stacked_out + k for k in range(n_prev)}

        def body(*refs):
            kern(*refs[:n_in], *refs[n_in + n_prev:])

    return pl.pallas_call(
        body, grid=grid, in_specs=in_specs, out_specs=out_specs, out_shape=out_shape,
        scratch_shapes=scratch_shapes, input_output_aliases=aliases,
        compiler_params=_cparams(("parallel", "arbitrary")), name=name,
    )(*args)


def _pad_rows(x, rows):
    if x.shape[0] == rows:
        return x
    return jnp.concatenate([x, jnp.zeros((rows - x.shape[0], x.shape[1]), x.dtype)], axis=0)


def _ret_kernel(*refs, rows, nchunk, has_state):
    if has_state:
        (q_ref, k_ref, v_ref, g_ref, cos_ref, sin_ref, dintra_ref, qdec_ref, kdec_ref, sdec_ref, gn_ref,
         s0_ref, o_ref, s_out_ref, s_scr) = refs
    else:
        (q_ref, k_ref, v_ref, g_ref, cos_ref, sin_ref, dintra_ref, qdec_ref, kdec_ref, sdec_ref, gn_ref,
         o_ref, s_out_ref, s_scr) = refs
    i = pl.program_id(1)
    c = RET_CHUNK
    half = RET_DK // 2

    @pl.when(i == 0)
    def _():
        if has_state:
            s_scr[...] = s0_ref[...]
        else:
            s_scr[...] = jnp.zeros_like(s_scr)

    def rot(ref, r, h, cos, sin):
        x1 = ref[r, h * RET_DK:h * RET_DK + half].astype(F32)
        x2 = ref[r, h * RET_DK + half:(h + 1) * RET_DK].astype(F32)
        return jnp.concatenate([x1 * cos - x2 * sin, x2 * cos + x1 * sin], axis=1)

    for ci in range(nchunk):
        r = slice(ci * rows, (ci + 1) * rows)
        cos = cos_ref[r, :]
        sin = sin_ref[r, :]
        for h in range(RET_HEADS):
            vs = slice(h * RET_DV, (h + 1) * RET_DV)
            q = _pad_rows(rot(q_ref, r, h, cos, sin), c)
            k = _pad_rows(rot(k_ref, r, h, cos, sin), c) * (RET_DK ** -0.5)
            if rows == c:
                v = v_ref[r, vs].astype(BF16)
            else:
                v = _pad_rows(v_ref[r, vs].astype(F32), c).astype(BF16)
            s = s_scr[h]
            att = _dot_nt(q.astype(BF16), k.astype(BF16)) * dintra_ref[h]
            o = _dot(att.astype(BF16), v) + _dot((q * qdec_ref[h]).astype(BF16), s.astype(BF16))
            s_scr[h] = s * sdec_ref[h] + _dot_tn((k * kdec_ref[h]).astype(BF16), v)
            o = o[:rows]
            mu = jnp.mean(o, axis=-1, keepdims=True)
            oc = o - mu
            var = jnp.mean(oc * oc, axis=-1, keepdims=True)
            on = oc * lax.rsqrt(var + LN_EPS) * gn_ref[...]
            gate = g_ref[r, vs].astype(F32)
            o_ref[r, vs] = (on * _silu(gate)).astype(o_ref.dtype)

    @pl.when(i == pl.num_programs(1) - 1)
    def _():
        s_out_ref[...] = s_scr[...]


def _ret_tables(geo):
    c_true = min(RET_CHUNK, geo.l_true)
    half = RET_DK // 2
    inv_freq = 10000.0 ** (-jnp.linspace(0.0, 1.0, half, dtype=F32))
    pos = (geo.pos0 + jnp.arange(geo.L)).astype(F32)
    ang = pos[:, None] * inv_freq[None, :]
    lg = jnp.log(1.0 - 2.0 ** (-5.0 - jnp.arange(RET_HEADS, dtype=F32)))
    idx = jnp.arange(c_true, dtype=F32)
    diff = idx[:, None] - idx[None, :]
    dintra = jnp.exp(jnp.where(diff[None] >= 0, diff[None] * lg[:, None, None], -jnp.inf))
    qdec = jnp.exp((idx + 1.0)[None, :] * lg[:, None])
    kdec = jnp.exp((c_true - 1.0 - idx)[None, :] * lg[:, None])
    sdec = jnp.exp(c_true * lg)
    pad = RET_CHUNK - c_true
    dintra = jnp.pad(dintra, ((0, 0), (0, pad), (0, pad)))
    qdec = jnp.pad(qdec, ((0, 0), (0, pad)))[..., None]
    kdec = jnp.pad(kdec, ((0, 0), (0, pad)))[..., None]
    sdec = jnp.broadcast_to(sdec[:, None, None], (RET_HEADS, 1, RET_DV))
    return jnp.cos(ang), jnp.sin(ang), dintra, qdec, kdec, sdec


def _retention(geo, p, tables, gn_g, s_all, layer_j, prev, tl):
    B, L = geo.B, geo.L
    nblk = L // tl
    rows = min(tl, RET_CHUNK)
    nchunk = tl // rows
    has_state = s_all is not None
    cos, sin, dintra, qdec, kdec, sdec = tables
    rb = lambda b, i: b * nblk + i
    full3 = lambda b, i: (0, 0, 0)
    in_specs = [
        pl.BlockSpec((tl, RET_QW), lambda b, i: (rb(b, i), 0)),
        pl.BlockSpec((tl, RET_QW), lambda b, i: (rb(b, i), 1)),
        pl.BlockSpec((tl, RET_VW), lambda b, i: (rb(b, i), 1)),
        pl.BlockSpec((tl, RET_VW), lambda b, i: (rb(b, i), 2)),
        pl.BlockSpec((tl, RET_DK // 2), lambda b, i: (i, 0)),
        pl.BlockSpec((tl, RET_DK // 2), lambda b, i: (i, 0)),
        pl.BlockSpec((RET_HEADS, RET_CHUNK, RET_CHUNK), full3),
        pl.BlockSpec((RET_HEADS, RET_CHUNK, 1), full3),
        pl.BlockSpec((RET_HEADS, RET_CHUNK, 1), full3),
        pl.BlockSpec((RET_HEADS, 1, RET_DV), full3),
        pl.BlockSpec((1, RET_DV), lambda b, i: (0, 0)),
    ]
    args = [p, p, p, p, cos, sin, dintra, qdec, kdec, sdec, gn_g.reshape(1, RET_DV)]
    if has_state:
        in_specs.append(pl.BlockSpec((None, None, RET_HEADS, RET_DK, RET_DV),
                                     lambda b, i: (layer_j, b, 0, 0, 0)))
        args.append(s_all)
    o, s = _call_stacked(
        functools.partial(_ret_kernel, rows=rows, nchunk=nchunk, has_state=has_state),
        prev, first_stacked_out=1,
        grid=(B, nblk),
        in_specs=in_specs,
        args=args,
        out_specs=[
            pl.BlockSpec((tl, RET_VW), lambda b, i: (rb(b, i), 0)),
            pl.BlockSpec((None, None, RET_HEADS, RET_DK, RET_DV), lambda b, i: (layer_j, b, 0, 0, 0)),
        ],
        out_shape=[
            jax.ShapeDtypeStruct((geo.T, RET_VW), geo.act_dtype),
            jax.ShapeDtypeStruct((N_RET_LAYERS, B, RET_HEADS, RET_DK, RET_DV), F32),
        ],
        scratch_shapes=[pltpu.VMEM((RET_HEADS, RET_DK, RET_DV), F32)],
        name="retention",
    )
    return o, (s,)


DN_CONV_PAD = SUBLANES
CONV_COL_CHUNK = 512


def _block_diag(x_cat, n, mask):
    return jnp.where(mask, jnp.concatenate([x_cat] * n, axis=0), 0.0)


def _dn_kernel(*refs, c, rows, nchunk, l_last, has_state):
    if has_state:
        (qkv_ref, z_ref, ab_ref, cw_ref, alog_ref, dtb_ref, ng_ref, s0_ref, buf_ref,
         o_ref, s_out_ref, buf_out_ref, s_scr, xp_scr, y_scr) = refs
    else:
        (qkv_ref, z_ref, ab_ref, alog_ref, dtb_ref, ng_ref, o_ref, s_out_ref, s_scr) = refs
    i = pl.program_id(1)
    hq = min(DN_V_HEADS, MXU_DIM // c)
    tl = rows * nchunk
    npair = DN_V_HEADS // 2
    W = DN_CONV_W
    P0 = DN_CONV_PAD

    @pl.when(i == 0)
    def _():
        if has_state:
            for pr in range(npair):
                s_scr[pr] = jnp.concatenate([s0_ref[2 * pr], s0_ref[2 * pr + 1]], axis=1)
            xp_scr[P0 - (W - 1):P0, :] = buf_ref[...]
        else:
            s_scr[...] = jnp.zeros_like(s_scr)

    if has_state:
        xp_scr[P0:P0 + tl, :] = qkv_ref[...].astype(F32)
        y = xp_scr[P0 - (W - 1):P0 - (W - 1) + tl, :] * cw_ref[0:1, :]
        for t in range(1, W):
            y = y + xp_scr[P0 - (W - 1) + t:P0 - (W - 1) + t + tl, :] * cw_ref[t:t + 1, :]
        y_scr[0:tl, :] = _silu(y)
        if rows < c:
            y_scr[tl:, :] = jnp.zeros((y_scr.shape[0] - tl, DN_CONV_CH), F32)

        @pl.when(i == pl.num_programs(1) - 1)
        def _():
            buf_out_ref[...] = xp_scr[P0 + l_last - (W - 1):P0 + l_last, :]

        xp_scr[P0 - (W - 1):P0, :] = xp_scr[P0 + tl - (W - 1):P0 + tl, :]

        def ycols(r, lo, hi):
            return y_scr[r, lo:hi]
    else:
        def ycols(r, lo, hi):
            return qkv_ref[r, lo:hi].astype(F32)

    ri = lax.broadcasted_iota(jnp.int32, (c, c), 0)
    ci_ = lax.broadcasted_iota(jnp.int32, (c, c), 1)
    incl = ri >= ci_
    strict = ri > ci_
    eye = ri == ci_
    tri_f = incl.astype(F32)
    eye_cat = jnp.concatenate([eye.astype(F32)] * hq, axis=1)
    bi = lax.broadcasted_iota(jnp.int32, (hq * c, hq * c), 0) // c
    bj = lax.broadcasted_iota(jnp.int32, (hq * c, hq * c), 1) // c
    bd_mask = bi == bj
    lane2 = lax.broadcasted_iota(jnp.int32, (DN_DK, 2 * DN_DV), 1)
    left = lane2 < DN_DV
    padded = rows < c or l_last < rows
    row_ok = lax.broadcasted_iota(jnp.int32, (c, 1), 0) < l_last if padded else None
    n_levels = c.bit_length() - 1

    nquad = DN_V_HEADS // hq
    quads = [[hq * q + t for t in range(hq)] for q in range(nquad)]

    pre = []
    for ck in range(nchunk):
        r0 = ck * c if rows == c else 0
        r = slice(r0, r0 + c)
        ab = _pad_rows(ab_ref[ck * rows:(ck + 1) * rows, :], c)
        sp = ab + dtb_ref[...]
        g_full = -jnp.exp(alog_ref[...]) * (jnp.maximum(sp, 0.0) + jnp.log1p(jnp.exp(-jnp.abs(sp))))
        beta_full = jax.nn.sigmoid(ab)
        if row_ok is not None:
            g_full = jnp.where(row_ok, g_full, 0.0)
            beta_full = jnp.where(row_ok, beta_full, 0.0)
        G_full = jnp.dot(tri_f, g_full, preferred_element_type=F32, precision=lax.Precision.HIGHEST)

        kn, qn, KK, QK = {}, {}, {}, {}
        for j in range(DN_QK_HEADS):
            qh = ycols(r, j * DN_DK, (j + 1) * DN_DK)
            kh = ycols(r, DN_QW + j * DN_DK, DN_QW + (j + 1) * DN_DK)
            qn[j] = qh * lax.rsqrt(jnp.sum(qh * qh, axis=-1, keepdims=True) + 1e-6) * (DN_DK ** -0.5)
            kn[j] = kh * lax.rsqrt(jnp.sum(kh * kh, axis=-1, keepdims=True) + 1e-6)
        for j in range(DN_QK_HEADS):
            kb16 = kn[j].astype(BF16)
            kq = _dot_nt(jnp.concatenate([kb16, qn[j].astype(BF16)], axis=0), kb16)
            KK[j], QK[j] = kq[:c], kq[c:]

        A, att, eG, Gc, rhs = {}, {}, {}, {}, {}
        for h in range(DN_V_HEADS):
            j = h // 2
            Gc[h] = G_full[:, h:h + 1]
            bc = beta_full[:, DN_V_HEADS + h:DN_V_HEADS + h + 1]
            Gr = jnp.sum(jnp.where(eye, Gc[h], 0.0), axis=0, keepdims=True)
            dec = jnp.exp(jnp.where(incl, Gc[h] - Gr, -jnp.inf))
            A[h] = jnp.where(strict, bc * KK[j] * dec, 0.0)
            att[h] = QK[j] * dec
            eG[h] = jnp.exp(Gc[h])
            vh = ycols(r, 2 * DN_QW + h * DN_DV, 2 * DN_QW + (h + 1) * DN_DV)
            rhs[h] = jnp.concatenate([vh * bc, kn[j] * (bc * eG[h])], axis=1)

        X = [jnp.concatenate([A[h] for h in hs], axis=1) for hs in quads]
        Pm = [eye_cat - x for x in X]
        X = [_dot(x.astype(BF16), _block_diag(x, hq, bd_mask).astype(BF16)) for x in X]
        for lvl in range(1, n_levels):
            Xw = [_block_diag(x, hq, bd_mask).astype(BF16) for x in X]
            if lvl < n_levels - 1:
                both = [_dot(jnp.concatenate([X[q], Pm[q]], axis=0).astype(BF16), Xw[q]) for q in range(nquad)]
                X = [b[:c] for b in both]
                Pm = [Pm[q] + both[q][c:] for q in range(nquad)]
            else:
                Pm = [Pm[q] + _dot(Pm[q].astype(BF16), Xw[q]) for q in range(nquad)]
        sol = [_dot(_block_diag(Pm[q], hq, bd_mask).astype(BF16),
                    jnp.concatenate([rhs[h] for h in quads[q]], axis=0).astype(BF16))
               for q in range(nquad)]
        att_bd = [_block_diag(jnp.concatenate([att[h] for h in quads[q]], axis=1), hq, bd_mask).astype(BF16)
                  for q in range(nquad)]
        pre.append((kn, qn, Gc, eG, sol, att_bd))

    for ck in range(nchunk):
        kn, qn, Gc, eG, sol, att_bd = pre[ck]
        sol_h = {h: sol[h // hq][c * (h % hq):c * (h % hq + 1)] for h in range(DN_V_HEADS)}
        u, oS = {}, {}
        for pr in range(npair):
            h0, h1 = 2 * pr, 2 * pr + 1
            S = s_scr[pr]
            Sbd = jnp.concatenate([jnp.where(left, S, 0.0), jnp.where(left, 0.0, S)], axis=0).astype(BF16)
            w_cat = jnp.concatenate([sol_h[h0][:, DN_DV:], sol_h[h1][:, DN_DV:]], axis=1)
            qe_cat = jnp.concatenate([qn[pr] * eG[h0], qn[pr] * eG[h1]], axis=1)
            R = _dot(jnp.concatenate([w_cat, qe_cat], axis=0).astype(BF16), Sbd)
            u[h0] = sol_h[h0][:, :DN_DV] - R[:c, :DN_DV]
            u[h1] = sol_h[h1][:, :DN_DV] - R[:c, DN_DV:]
            oS[h0] = R[c:, :DN_DV]
            oS[h1] = R[c:, DN_DV:]

        o_intra = [_dot(att_bd[q], jnp.concatenate([u[h] for h in quads[q]], axis=0).astype(BF16))
                   for q in range(nquad)]

        for pr in range(npair):
            h0, h1 = 2 * pr, 2 * pr + 1
            gl0 = Gc[h0][c - 1:c, :]
            gl1 = Gc[h1][c - 1:c, :]
            du = jnp.concatenate([u[h0] * jnp.exp(gl0 - Gc[h0]), u[h1] * jnp.exp(gl1 - Gc[h1])], axis=1)
            sdec = jnp.where(left[0:1, :], jnp.exp(gl0), jnp.exp(gl1))
            s_scr[pr] = s_scr[pr] * sdec + _dot_tn(kn[pr].astype(BF16), du.astype(BF16))

        zr = slice(ck * rows, (ck + 1) * rows)
        for h in range(DN_V_HEADS):
            t = h % hq
            o = (oS[h] + o_intra[h // hq][c * t:c * (t + 1)])[:rows]
            o = o * lax.rsqrt(jnp.mean(o * o, axis=-1, keepdims=True) + 1e-6) * ng_ref[...]
            hs = slice(h * DN_DV, (h + 1) * DN_DV)
            z = z_ref[zr, hs].astype(F32)
            o_ref[zr, hs] = (o * _silu(z)).astype(o_ref.dtype)

    @pl.when(i == pl.num_programs(1) - 1)
    def _():
        for pr in range(npair):
            s_out_ref[2 * pr] = s_scr[pr][:, :DN_DV]
            s_out_ref[2 * pr + 1] = s_scr[pr][:, DN_DV:]


def _deltanet(geo, p, ab, conv_w, a_log, dt_bias, norm_g, s_all, buf_all, layer_j, prev, tl, chunk):
    B, L = geo.B, geo.L
    nblk = L // tl
    rows = min(tl, chunk)
    nchunk = tl // rows
    l_last = geo.l_true - (nblk - 1) * tl
    assert l_last == tl or (nblk == 1 and nchunk == 1)
    has_state = s_all is not None
    rb = lambda b, i: b * nblk + i
    pad16 = lambda v: jnp.pad(v.reshape(1, DN_V_HEADS), ((0, 0), (0, LANES - DN_V_HEADS)))
    row1 = lambda n: pl.BlockSpec((1, n), lambda b, i: (0, 0))
    in_specs = [
        pl.BlockSpec((tl, DN_CONV_CH), lambda b, i: (rb(b, i), 0)),
        pl.BlockSpec((tl, DN_VW), lambda b, i: (rb(b, i), DN_CONV_CH // DN_VW)),
        pl.BlockSpec((tl, LANES), lambda b, i: (rb(b, i), 0)),
    ]
    args = [p, p, ab]
    if has_state:
        in_specs.append(pl.BlockSpec((DN_CONV_W, DN_CONV_CH), lambda b, i: (0, 0)))
        args.append(conv_w)
    in_specs += [row1(LANES), row1(LANES), row1(DN_DV)]
    args += [pad16(a_log), pad16(dt_bias), norm_g.reshape(1, DN_DV)]
    out_specs = [
        pl.BlockSpec((tl, DN_VW), lambda b, i: (rb(b, i), 0)),
        pl.BlockSpec((None, None, DN_V_HEADS, DN_DK, DN_DV), lambda b, i: (layer_j, b, 0, 0, 0)),
    ]
    out_shape = [
        jax.ShapeDtypeStruct((geo.T, DN_VW), geo.act_dtype),
        jax.ShapeDtypeStruct((N_DN_LAYERS, B, DN_V_HEADS, DN_DK, DN_DV), F32),
    ]
    scratch = [pltpu.VMEM((DN_V_HEADS // 2, DN_DK, 2 * DN_DV), F32)]
    if has_state:
        in_specs += [
            pl.BlockSpec((None, None, DN_V_HEADS, DN_DK, DN_DV), lambda b, i: (layer_j, b, 0, 0, 0)),
            pl.BlockSpec((None, None, DN_CONV_W - 1, DN_CONV_CH), lambda b, i: (layer_j, b, 0, 0)),
        ]
        args += [s_all, buf_all]
        out_specs.append(pl.BlockSpec((None, None, DN_CONV_W - 1, DN_CONV_CH), lambda b, i: (layer_j, b, 0, 0)))
        out_shape.append(jax.ShapeDtypeStruct((N_DN_LAYERS, B, DN_CONV_W - 1, DN_CONV_CH), F32))
        scratch += [pltpu.VMEM((DN_CONV_PAD + tl, DN_CONV_CH), F32),
                    pltpu.VMEM((max(tl, chunk), DN_CONV_CH), F32)]
    kern = functools.partial(_dn_kernel, c=chunk, rows=rows, nchunk=nchunk, l_last=l_last, has_state=has_state)
    res = _call_stacked(kern, prev, first_stacked_out=1, grid=(B, nblk), in_specs=in_specs, args=args,
                        out_specs=out_specs, out_shape=out_shape, scratch_shapes=scratch, name="deltanet")
    return res[0], tuple(res[1:])


def _outproj_kernel(o_ref, w_ref, x_ref, gate_ref, g_ref, b_ref, out_ref):
    y = _dot(o_ref[...].astype(BF16), w_ref[...])
    z = ALPHA * x_ref[...] + (1.0 + gate_ref[...]) * y
    out_ref[...] = _layer_norm(z, g_ref[...], b_ref[...])


def _outproj_ln(geo, o, w_out, x, mod, ln_g, ln_b):
    T, tm = geo.T, geo.tm
    K = w_out.shape[0]
    vec = pl.BlockSpec((1, D_MODEL), lambda i: (0, 0))
    return pl.pallas_call(
        _outproj_kernel,
        grid=(T // tm,),
        in_specs=[
            pl.BlockSpec((tm, K), lambda i: (i, 0)),
            pl.BlockSpec((K, D_MODEL), lambda i: (0, 0)),
            pl.BlockSpec((tm, D_MODEL), lambda i: (i, 0)),
            _mod_spec(geo, 2),
            vec, vec,
        ],
        out_specs=pl.BlockSpec((tm, D_MODEL), lambda i: (i, 0)),
        out_shape=jax.ShapeDtypeStruct((T, D_MODEL), F32),
        compiler_params=_cparams(("parallel",)),
        name="out_proj_ln",
    )(o, w_out, x, mod, ln_g.reshape(1, D_MODEL), ln_b.reshape(1, D_MODEL))


MOE_ROWS = 144


def _moe_blocks(tm):
    nb = (tm - N_GROUPS) // MOE_ROWS + N_GROUPS
    ns = -(-(nb * MOE_ROWS) // LANES) * LANES
    return nb, ns


def _first_max(v, row, valid):
    m = jnp.max(jnp.where(valid, v, -jnp.inf), axis=0, keepdims=True)
    idx = jnp.min(jnp.where(valid & (v == m), row, float(N_EXPERTS)), axis=0, keepdims=True)
    return m, idx


def _route(h, wrt, br):
    logits = lax.dot_general(wrt, h, (((1,), (1,)), ((), ())), preferred_element_type=F32,
                             precision=lax.Precision.HIGHEST)
    mx = jnp.max(logits, axis=0, keepdims=True)
    ex = jnp.exp(logits - mx)
    probs = ex / jnp.sum(ex, axis=0, keepdims=True)
    sel = probs + br
    row_i = lax.broadcasted_iota(jnp.int32, sel.shape, 0)
    grp = (row_i // EXPERTS_PER_GROUP).astype(F32)
    row = row_i.astype(F32)
    best = None
    gsel = None
    for g in range(N_GROUPS):
        in_g = grp == g
        m1, i1 = _first_max(sel, row, in_g)
        m2, _ = _first_max(sel, row, in_g & (row != i1))
        score = m1 + m2
        if g == 0:
            best, gsel = score, jnp.zeros_like(i1)
        else:
            better = score > best
            gsel = jnp.where(better, float(g), gsel)
            best = jnp.where(better, score, best)
    in_grp = grp == gsel
    _, i1 = _first_max(sel, row, in_grp)
    _, i2 = _first_max(sel, row, in_grp & (row != i1))
    w1 = jnp.sum(jnp.where(row == i1, probs, 0.0), axis=0, keepdims=True)
    w2 = jnp.sum(jnp.where(row == i2, probs, 0.0), axis=0, keepdims=True)
    wsum = w1 + w2
    gates = jnp.where(row == i1, w1 / wsum, 0.0) + jnp.where(row == i2, w2 / wsum, 0.0)
    return gsel, gates


def _router_kernel(x_ref, sc_ref, sh_ref, wrt_ref, br_ref, posr_ref, posc_ref, gates_ref, tbl_ref):
    tm = x_ref.shape[0]
    h = x_ref[...] * (1.0 + sc_ref[...]) + sh_ref[...]
    gsel, gates = _route(h, wrt_ref[...], br_ref[...])
    gates_ref[...] = gates
    grow = lax.broadcasted_iota(jnp.int32, (SUBLANES, tm), 0).astype(F32)
    og = grow == gsel
    ti = lax.broadcasted_iota(jnp.int32, (tm, tm), 0)
    tj = lax.broadcasted_iota(jnp.int32, (tm, tm), 1)
    before = jnp.where(ti < tj, 1.0, 0.0).astype(BF16)
    ogf = jnp.where(og, 1.0, 0.0)
    rank = _dot(ogf.astype(BF16), before)
    cnt = jnp.sum(ogf, axis=1, keepdims=True)
    nblk = jnp.zeros_like(cnt)
    for kb in range(-(-tm // MOE_ROWS)):
        nblk = nblk + jnp.where(cnt > float(kb * MOE_ROWS), 1.0, 0.0)
    seg = nblk * float(MOE_ROWS)
    block_start = lax.broadcasted_iota(jnp.int32, (1, LANES), 1).astype(F32) * float(MOE_ROWS)
    off = jnp.zeros((1, 1), F32)
    pos = jnp.zeros((1, tm), F32)
    tbl = jnp.zeros((1, LANES), F32)
    for g in range(N_GROUPS):
        pos = pos + jnp.where(gsel == float(g), off + rank[g:g + 1, :], 0.0)
        off = off + seg[g:g + 1, :]
        tbl = tbl + jnp.where(block_start >= off, 1.0, 0.0)
    posr_ref[...] = pos
    posc_ref[...] = jnp.sum(jnp.where(ti == tj, pos, 0.0), axis=1, keepdims=True)
    tbl_ref[...] = tbl.astype(jnp.int32)


def _expert_kernel(tbl_ref, x_ref, sc_ref, sh_ref, gt_ref, posr_ref, posc_ref, gates_ref, wgu_ref, wd_ref,
                   g_ref, b_ref, out_ref, hs_scr, gs_scr, ys_scr, *, nb):
    i = pl.program_id(0)
    tm = x_ref.shape[0]
    ns = hs_scr.shape[0]
    R = MOE_ROWS
    x = x_ref[...]
    hb = (x * (1.0 + sc_ref[...]) + sh_ref[...]).astype(BF16)
    slot = lax.broadcasted_iota(jnp.int32, (ns, tm), 0).astype(F32)
    P = jnp.where(slot == posr_ref[...], 1.0, 0.0).astype(BF16)
    hs_scr[...] = _dot(P, hb).astype(BF16)
    gates = gates_ref[...]
    g_hi = gates.astype(BF16)
    g_lo = (gates - g_hi.astype(F32)).astype(BF16)
    gs_scr[...] = _dot_nt(P, g_hi) + _dot_nt(P, g_lo)
    if nb * R < ns:
        ys_scr[nb * R:, :] = jnp.zeros((ns - nb * R, D_MODEL), BF16)

    for b in range(nb):
        rs = slice(b * R, (b + 1) * R)
        g = tbl_ref[i, b]

        @pl.when(g < N_GROUPS)
        def _():
            gsb = gs_scr[rs, :]
            lane = lax.broadcasted_iota(jnp.int32, gsb.shape, 1)
            hsb = hs_scr[rs, :]
            acts = []
            for e in range(EXPERTS_PER_GROUP):
                gu = _dot(hsb, wgu_ref[g, e])
                gate = jnp.sum(jnp.where(lane == g * EXPERTS_PER_GROUP + e, gsb, 0.0), axis=1, keepdims=True)
                act = _silu(gu[:, :D_EXPERT]) * gu[:, D_EXPERT:] * gate
                acts.append(act.astype(BF16))
            ys_scr[rs, :] = _dot(jnp.concatenate(acts, axis=1), wd_ref[g]).astype(BF16)

        @pl.when(g >= N_GROUPS)
        def _():
            ys_scr[rs, :] = jnp.zeros((R, D_MODEL), BF16)

    slot_t = lax.broadcasted_iota(jnp.int32, (tm, ns), 1).astype(F32)
    PT = jnp.where(slot_t == posc_ref[...], 1.0, 0.0).astype(BF16)
    y = _dot(PT, ys_scr[...])
    z = ALPHA * x + (1.0 + gt_ref[...]) * y
    out_ref[...] = _layer_norm(z, g_ref[...], b_ref[...])


def _moe_ln(geo, x, mod, w_router_t, b_router, w_gu, w_down, layer, ln_g, ln_b):
    T, tm = geo.T, geo.tm
    nt = T // tm
    nb, ns = _moe_blocks(tm)
    posr, posc, gates, tbl = pl.pallas_call(
        _router_kernel,
        grid=(nt,),
        in_specs=[
            pl.BlockSpec((tm, D_MODEL), lambda i: (i, 0)),
            _mod_spec(geo, 4),
            _mod_spec(geo, 3),
            pl.BlockSpec((N_EXPERTS, D_MODEL), lambda i: (0, 0)),
            pl.BlockSpec((N_EXPERTS, 1), lambda i: (0, 0)),
        ],
        out_specs=[
            pl.BlockSpec((None, 1, tm), lambda i: (i, 0, 0)),
            pl.BlockSpec((tm, 1), lambda i: (i, 0)),
            pl.BlockSpec((N_EXPERTS, tm), lambda i: (0, i)),
            pl.BlockSpec((None, 1, LANES), lambda i: (i, 0, 0)),
        ],
        out_shape=[
            jax.ShapeDtypeStruct((nt, 1, tm), F32),
            jax.ShapeDtypeStruct((T, 1), F32),
            jax.ShapeDtypeStruct((N_EXPERTS, T), F32),
            jax.ShapeDtypeStruct((nt, 1, LANES), jnp.int32),
        ],
        compiler_params=_cparams(("parallel",)),
        name="moe_router",
    )(x, mod, mod, w_router_t, b_router.reshape(N_EXPERTS, 1))

    vec = pl.BlockSpec((1, D_MODEL), lambda i, t: (0, 0))
    resident = dict(pipeline_mode=pl.Buffered(1))
    grid_spec = pltpu.PrefetchScalarGridSpec(
        num_scalar_prefetch=1,
        grid=(nt,),
        in_specs=[
            pl.BlockSpec((tm, D_MODEL), lambda i, t: (i, 0)),
            _mod_spec(geo, 4),
            _mod_spec(geo, 3),
            _mod_spec(geo, 5),
            pl.BlockSpec((None, 1, tm), lambda i, t: (i, 0, 0)),
            pl.BlockSpec((tm, 1), lambda i, t: (i, 0)),
            pl.BlockSpec((N_EXPERTS, tm), lambda i, t: (0, i)),
            pl.BlockSpec((None, N_GROUPS, EXPERTS_PER_GROUP, D_MODEL, 2 * D_EXPERT),
                         lambda i, t: (layer, 0, 0, 0, 0), **resident),
            pl.BlockSpec((None, N_GROUPS, EXPERTS_PER_GROUP * D_EXPERT, D_MODEL),
                         lambda i, t: (layer, 0, 0, 0), **resident),
            vec, vec,
        ],
        out_specs=pl.BlockSpec((tm, D_MODEL), lambda i, t: (i, 0)),
        scratch_shapes=[
            pltpu.VMEM((ns, D_MODEL), BF16),
            pltpu.VMEM((ns, N_EXPERTS), F32),
            pltpu.VMEM((ns, D_MODEL), BF16),
        ],
    )
    return pl.pallas_call(
        functools.partial(_expert_kernel, nb=nb),
        grid_spec=grid_spec,
        out_shape=jax.ShapeDtypeStruct((T, D_MODEL), F32),
        compiler_params=_cparams(("parallel",)),
        name="moe_experts_ln",
    )(tbl.reshape(nt, LANES), x, mod, mod, mod, posr, posc, gates, w_gu, w_down,
      ln_g.reshape(1, D_MODEL), ln_b.reshape(1, D_MODEL))


def _trunk(geo, x, mods, s_ret, s_dn, s_conv, wts, ret_tl, dn_tl, dn_chunk):
    (ln_g, ln_b, ret_w_in, ret_gn_g, ret_w_out, dn_w_main, dn_w_tail, dn_conv_w, dn_a_log, dn_dt_bias,
     dn_norm_g, dn_w_out, router_w, router_b, moe_w_gu, moe_w_down) = wts
    tables = _ret_tables(geo)
    decode = s_dn is not None
    proj_tm = None if geo.per_token or geo.L % 1024 else 1024
    ret_stack, dn_stack, conv_tails = None, None, []
    for i in range(DEPTH):
        if geo.per_token:
            mod = jnp.repeat(mods[i], geo.L, axis=0)
        else:
            mod = mods[i].reshape(geo.B, 1, N_MOD * D_MODEL)
        j = i // 2
        if i % 2 == 0:
            p, _ = _inproj(geo, x, mod, ret_w_in[j], None, tn=1536, tm=proj_tm)
            o, ret_stack = _retention(geo, p, tables, ret_gn_g[j], s_ret, j, ret_stack, ret_tl)
            w_out = ret_w_out[j]
        else:
            if decode:
                p, ab = _inproj(geo, x, mod, dn_w_main[j], dn_w_tail[j], tn=1536)
            else:
                p, ab = _inproj(geo, x, mod, dn_w_main[j], dn_w_tail[j], tn=2048,
                                conv_w=dn_conv_w[j], conv_cols=DN_CONV_CH)
                conv_tails.append(_conv_tail(geo, x, mod, dn_w_main[j], DN_CONV_CH, tn=2048))
            o, dn_stack = _deltanet(geo, p, ab, dn_conv_w[j], dn_a_log[j], dn_dt_bias[j], dn_norm_g[j],
                                    s_dn, s_conv, j, dn_stack, dn_tl, dn_chunk)
            w_out = dn_w_out[j]
        x = _outproj_ln(geo, o, w_out, x, mod, ln_g[i, 0], ln_b[i, 0])
        x = _moe_ln(geo, x, mod, router_w, router_b, moe_w_gu, moe_w_down, i, ln_g[i, 1], ln_b[i, 1])
    if decode:
        new_dn, new_conv = dn_stack
    else:
        new_dn = dn_stack[0]
        new_conv = jnp.stack(conv_tails)[:, :, SUBLANES - (DN_CONV_W - 1):]
    return x, ret_stack[0], new_dn, new_conv


def _run(x_prompt, x_sample, state_ret, state_dn, state_conv, c_prompt, c_sample, w_ada, b_ada, ln_g, ln_b,
         ret_w_in, ret_gn_g, ret_w_out, dn_w_in, dn_conv_w, dn_a_log, dn_dt_bias, dn_norm_g, dn_w_out,
         router_w, router_b, moe_w_gu, moe_w_down, *, past_len):
    Bp, Lp, _ = x_prompt.shape
    Bs, Ls, _ = x_sample.shape
    Ls_pad = -(-Ls // SUBLANES) * SUBLANES
    tm_p = min(512, Lp)
    geo_p = _Geo(Bp, Lp, Lp, 0, tm_p, False, BF16)
    Ts = Bs * Ls_pad
    geo_s = _Geo(Bs, Ls_pad, Ls, past_len, min(512, Ts), True, F32)

    dn_w_tail = jnp.pad(dn_w_in[:, :, DN_MAIN:], ((0, 0), (0, 0), (0, LANES - 2 * DN_V_HEADS))).astype(BF16)
    w_gu_grp = moe_w_gu.astype(BF16).reshape(DEPTH, N_GROUPS, EXPERTS_PER_GROUP, D_MODEL, 2 * D_EXPERT)
    w_down_grp = moe_w_down.astype(BF16).reshape(DEPTH, N_GROUPS, EXPERTS_PER_GROUP * D_EXPERT, D_MODEL)
    wts = (ln_g, ln_b, ret_w_in.astype(BF16), ret_gn_g, ret_w_out.astype(BF16),
           dn_w_in[:, :, :DN_MAIN].astype(BF16), dn_w_tail, dn_conv_w, dn_a_log, dn_dt_bias, dn_norm_g,
           dn_w_out.astype(BF16), router_w.T, router_b, w_gu_grp, w_down_grp)

    mods = _ada(jnp.concatenate([c_prompt, c_sample], axis=0), w_ada, b_ada)

    xp = x_prompt.reshape(Bp * Lp, D_MODEL)
    yp, ret_p, dn_p, conv_p = _trunk(geo_p, xp, mods[:, :Bp], None, None, None, wts,
                                     ret_tl=min(512, Lp), dn_tl=min(256, Lp), dn_chunk=DN_CHUNK)
    xs = jnp.pad(x_sample, ((0, 0), (0, Ls_pad - Ls), (0, 0))).reshape(Ts, D_MODEL)
    ys, ret_s, dn_s, conv_s = _trunk(geo_s, xs, mods[:, Bp:], state_ret, state_dn, state_conv, wts,
                                     ret_tl=Ls_pad, dn_tl=Ls_pad, dn_chunk=max(Ls_pad, DN_DECODE_CHUNK))
    y_prompt = yp.reshape(Bp, Lp, D_MODEL)
    y_sample = ys.reshape(Bs, Ls_pad, D_MODEL)[:, :Ls]
    return (y_prompt, y_sample, ret_p, ret_s, dn_p, dn_s, conv_p, conv_s)


def kernel(x_prompt, x_sample, state_ret, state_dn, state_conv, c_prompt, c_sample, w_ada, b_ada, ln_g, ln_b,
           ret_w_in, ret_gn_g, ret_w_out, dn_w_in, dn_conv_w, dn_a_log, dn_dt_bias, dn_norm_g, dn_w_out,
           router_w, router_b, moe_w_gu, moe_w_down):
    return _run(x_prompt, x_sample, state_ret, state_dn, state_conv, c_prompt, c_sample, w_ada, b_ada,
                ln_g, ln_b, ret_w_in, ret_gn_g, ret_w_out, dn_w_in, dn_conv_w, dn_a_log, dn_dt_bias,
                dn_norm_g, dn_w_out, router_w, router_b, moe_w_gu, moe_w_down, past_len=PAST_LEN)
```

```python
import functools
from typing import NamedTuple

import jax
import jax.numpy as jnp
from jax import lax
from jax.experimental import pallas as pl
from jax.experimental.pallas import tpu as pltpu

F32 = jnp.float32
BF16 = jnp.bfloat16

D_MODEL = 1024
DEPTH = 4
N_RET_LAYERS = (DEPTH + 1) // 2
N_DN_LAYERS = DEPTH // 2
PAST_LEN = 16384
RET_HEADS = 4
RET_DK = 256
RET_DV = 512
RET_QW = RET_HEADS * RET_DK
RET_VW = RET_HEADS * RET_DV
RET_CHUNK = 128
DN_QK_HEADS = 8
DN_V_HEADS = 16
DN_DK = 128
DN_DV = 128
DN_QW = DN_QK_HEADS * DN_DK
DN_VW = DN_V_HEADS * DN_DV
DN_CONV_CH = 2 * DN_QW + DN_VW
DN_CONV_W = 4
DN_CHUNK = 64
DN_DECODE_CHUNK = 16
DN_MAIN = DN_CONV_CH + DN_VW
N_EXPERTS = 16
N_GROUPS = 4
EXPERTS_PER_GROUP = N_EXPERTS // N_GROUPS
D_EXPERT = 256
N_MOD = 6
ALPHA = (2.0 * DEPTH) ** 0.25
LN_EPS = 1e-5

LANES = 128
SUBLANES = 8
MXU_DIM = 256
VMEM_LIMIT = 56 * 1024 * 1024


class _Geo(NamedTuple):
    B: int
    L: int
    l_true: int
    pos0: int
    tm: int
    per_token: bool
    act_dtype: object

    @property
    def T(self):
        return self.B * self.L


def _cparams(sem):
    return pltpu.CompilerParams(dimension_semantics=sem, vmem_limit_bytes=VMEM_LIMIT)


def _mod_spec(geo, chunk):
    if geo.per_token:
        return pl.BlockSpec((geo.tm, D_MODEL), lambda *g: (g[0], chunk))
    tpr = geo.L // geo.tm
    return pl.BlockSpec((None, 1, D_MODEL), lambda *g: (g[0] // tpr, 0, chunk))


def _silu(x):
    hx = 0.5 * x
    return hx + hx * jnp.tanh(hx)


def _dot(a, b):
    return jnp.dot(a, b, preferred_element_type=F32)


def _dot_nt(a, b):
    return lax.dot_general(a, b, (((1,), (1,)), ((), ())), preferred_element_type=F32)


def _dot_tn(a, b):
    return lax.dot_general(a, b, (((0,), (0,)), ((), ())), preferred_element_type=F32)


def _layer_norm(z, g, b):
    mu = jnp.mean(z, axis=-1, keepdims=True)
    zc = z - mu
    var = jnp.mean(zc * zc, axis=-1, keepdims=True)
    return zc * lax.rsqrt(var + LN_EPS) * g + b


def _ada_kernel(c_ref, w_ref, b_ref, o_ref):
    cs = _silu(c_ref[...]).astype(BF16)
    o_ref[...] = _dot(cs, w_ref[...].astype(BF16)) + b_ref[...]


def _ada(c_all, w_ada, b_ada):
    R = c_all.shape[0]
    tn = 1024
    n_out = N_MOD * D_MODEL
    return pl.pallas_call(
        _ada_kernel,
        grid=(DEPTH, n_out // tn),
        in_specs=[
            pl.BlockSpec((R, D_MODEL), lambda l, j: (0, 0)),
            pl.BlockSpec((None, D_MODEL, tn), lambda l, j: (l, 0, j)),
            pl.BlockSpec((None, 1, tn), lambda l, j: (l, 0, j)),
        ],
        out_specs=pl.BlockSpec((None, R, tn), lambda l, j: (l, 0, j)),
        out_shape=jax.ShapeDtypeStruct((DEPTH, R, n_out), F32),
        compiler_params=_cparams(("parallel", "parallel")),
        name="ada_mod",
    )(c_all, w_ada, b_ada.reshape(DEPTH, 1, n_out))


def _inproj_kernel(*refs, has_tail, n_conv, tiles_per_row):
    refs = list(refs)
    x_ref, sc_ref, sh_ref, w_ref = refs[:4]
    del refs[:4]
    wt_ref = refs.pop(0) if has_tail else None
    cw_ref = refs.pop(0) if n_conv else None
    o_ref = refs.pop(0)
    ot_ref = refs.pop(0) if has_tail else None
    h_scr = refs.pop(0)
    i = pl.program_id(0)
    j = pl.program_id(1)

    @pl.when(j == 0)
    def _():
        h = x_ref[...] * (1.0 + sc_ref[...]) + sh_ref[...]
        hb = h.astype(BF16)
        h_scr[...] = hb
        if has_tail:
            ot_ref[...] = _dot(hb, wt_ref[...])

    if not n_conv:
        o_ref[...] = _dot(h_scr[...], w_ref[...]).astype(o_ref.dtype)
        return
    (carry_scr,) = refs
    tm, tn = o_ref.shape
    W = DN_CONV_W
    P0 = DN_CONV_PAD
    cw = CONV_COL_CHUNK

    @pl.when(j < n_conv)
    def _():
        row_start = (i % tiles_per_row) == 0
        prev = jnp.where(row_start, 0.0, carry_scr[j])
        for c0 in range(0, tn, cw):
            cs = slice(c0, c0 + cw)
            p = _dot(h_scr[...], w_ref[:, cs])
            y = p * cw_ref[W - 1:W, cs]
            for t in range(W - 1):
                y = y + pltpu.roll(p, W - 1 - t, 0) * cw_ref[t:t + 1, cs]
            o_ref[:, cs] = _silu(y).astype(o_ref.dtype)
            nfix = 2 * SUBLANES
            head = jnp.concatenate([prev[:, cs], p[:nfix]], axis=0)
            y0 = head[P0:] * cw_ref[W - 1:W, cs]
            for t in range(W - 1):
                y0 = y0 + head[P0 - (W - 1 - t):P0 + nfix - (W - 1 - t)] * cw_ref[t:t + 1, cs]
            o_ref[0:nfix, cs] = _silu(y0).astype(o_ref.dtype)
            carry_scr[j, :, cs] = p[tm - P0:]

    @pl.when(j >= n_conv)
    def _():
        o_ref[...] = _dot(h_scr[...], w_ref[...]).astype(o_ref.dtype)


def _inproj(geo, x, mod, w, w_tail, tn, conv_w=None, conv_cols=0, tm=None):
    if tm is not None:
        geo = geo._replace(tm=tm)
    T, tm = geo.T, geo.tm
    N = w.shape[1]
    has_tail = w_tail is not None
    n_conv = conv_cols // tn
    in_specs = [
        pl.BlockSpec((tm, D_MODEL), lambda i, j: (i, 0)),
        _mod_spec(geo, 1),
        _mod_spec(geo, 0),
        pl.BlockSpec((D_MODEL, tn), lambda i, j: (0, j)),
    ]
    args = [x, mod, mod, w]
    out_specs = [pl.BlockSpec((tm, tn), lambda i, j: (i, j))]
    out_shape = [jax.ShapeDtypeStruct((T, N), geo.act_dtype)]
    scratch = [pltpu.VMEM((tm, D_MODEL), BF16)]
    if has_tail:
        in_specs.append(pl.BlockSpec((D_MODEL, LANES), lambda i, j: (0, 0)))
        args.append(w_tail)
        out_specs.append(pl.BlockSpec((tm, LANES), lambda i, j: (i, 0)))
        out_shape.append(jax.ShapeDtypeStruct((T, LANES), F32))
    if n_conv:
        assert conv_cols == n_conv * tn and not geo.per_token
        in_specs.append(pl.BlockSpec((DN_CONV_W, tn), lambda i, j: (0, jnp.minimum(j, n_conv - 1))))
        args.append(conv_w)
        scratch += [pltpu.VMEM((n_conv, DN_CONV_PAD, tn), F32)]
    res = pl.pallas_call(
        functools.partial(_inproj_kernel, has_tail=has_tail, n_conv=n_conv, tiles_per_row=geo.L // tm),
        grid=(T // tm, N // tn),
        in_specs=in_specs,
        out_specs=out_specs,
        out_shape=out_shape,
        scratch_shapes=scratch,
        compiler_params=_cparams(("arbitrary", "arbitrary")),
        name="in_proj",
    )(*args)
    return res if has_tail else (res[0], None)


def _conv_tail_kernel(x_ref, sc_ref, sh_ref, w_ref, o_ref):
    h = x_ref[...] * (1.0 + sc_ref[...]) + sh_ref[...]
    o_ref[...] = _dot(h.astype(BF16), w_ref[...])


def _conv_tail(geo, x, mod, w, n_cols, tn):
    B, L = geo.B, geo.L
    blocks_per_row = L // SUBLANES
    return pl.pallas_call(
        _conv_tail_kernel,
        grid=(n_cols // tn, B),
        in_specs=[
            pl.BlockSpec((SUBLANES, D_MODEL), lambda j, b: (b * blocks_per_row + blocks_per_row - 1, 0)),
            pl.BlockSpec((None, 1, D_MODEL), lambda j, b: (b, 0, 1)),
            pl.BlockSpec((None, 1, D_MODEL), lambda j, b: (b, 0, 0)),
            pl.BlockSpec((D_MODEL, tn), lambda j, b: (0, j)),
        ],
        out_specs=pl.BlockSpec((None, SUBLANES, tn), lambda j, b: (b, 0, j)),
        out_shape=jax.ShapeDtypeStruct((B, SUBLANES, n_cols), F32),
        compiler_params=_cparams(("parallel", "parallel")),
        name="conv_tail",
    )(x, mod, mod, w)


def _call_stacked(kern, prev, first_stacked_out, *, grid, in_specs, args, out_specs, out_shape,
                  scratch_shapes, name):
    n_in = len(args)
    aliases = {}
    body = kern
    if prev is not None:
        n_prev = len(prev)
        in_specs = list(in_specs) + [pl.BlockSpec(memory_space=pl.ANY)] * n_prev
        args = list(args) + list(prev)
        aliases = {n_in + k: first_stacked_out + k for k in range(n_prev)}

        def body(*refs):
            kern(*refs[:n_in], *refs[n_in + n_prev:])

    return pl.pallas_call(
        body, grid=grid, in_specs=in_specs, out_specs=out_specs, out_shape=out_shape,
        scratch_shapes=scratch_shapes, input_output_aliases=aliases,
        compiler_params=_cparams(("parallel", "arbitrary")), name=name,
    )(*args)


def _stack_fill(prev, layer_j, n_layers):
    if prev is None:
        return (layer_j, n_layers), n_layers, 0
    return None, None, layer_j


def _pad_rows(x, rows):
    if x.shape[0] == rows:
        return x
    return jnp.concatenate([x, jnp.zeros((rows - x.shape[0], x.shape[1]), x.dtype)], axis=0)


def _ret_kernel(*refs, rows, nchunk, has_state, fill):
    if has_state:
        (q_ref, k_ref, v_ref, g_ref, cos_ref, sin_ref, dintra_ref, qdec_ref, kdec_ref, sdec_ref, gn_ref,
         s0_ref, o_ref, s_out_ref, s_scr) = refs
    else:
        (q_ref, k_ref, v_ref, g_ref, cos_ref, sin_ref, dintra_ref, qdec_ref, kdec_ref, sdec_ref, gn_ref,
         o_ref, s_out_ref, s_scr) = refs
    i = pl.program_id(1)
    c = RET_CHUNK
    half = RET_DK // 2

    @pl.when(i == 0)
    def _():
        if has_state:
            s_scr[...] = s0_ref[...]
        else:
            s_scr[...] = jnp.zeros_like(s_scr)

    def rot(ref, r, h, cos, sin):
        x1 = ref[r, h * RET_DK:h * RET_DK + half].astype(F32)
        x2 = ref[r, h * RET_DK + half:(h + 1) * RET_DK].astype(F32)
        return jnp.concatenate([x1 * cos - x2 * sin, x2 * cos + x1 * sin], axis=1)

    for ci in range(nchunk):
        r = slice(ci * rows, (ci + 1) * rows)
        cos = cos_ref[r, :]
        sin = sin_ref[r, :]
        for h in range(RET_HEADS):
            vs = slice(h * RET_DV, (h + 1) * RET_DV)
            q = _pad_rows(rot(q_ref, r, h, cos, sin), c)
            k = _pad_rows(rot(k_ref, r, h, cos, sin), c) * (RET_DK ** -0.5)
            if rows == c:
                v = v_ref[r, vs].astype(BF16)
            else:
                v = _pad_rows(v_ref[r, vs].astype(F32), c).astype(BF16)
            s = s_scr[h]
            att = _dot_nt(q.astype(BF16), k.astype(BF16)) * dintra_ref[h]
            o = _dot(att.astype(BF16), v) + _dot((q * qdec_ref[h]).astype(BF16), s.astype(BF16))
            s_scr[h] = s * sdec_ref[h] + _dot_tn((k * kdec_ref[h]).astype(BF16), v)
            o = o[:rows]
            mu = jnp.mean(o, axis=-1, keepdims=True)
            oc = o - mu
            var = jnp.mean(oc * oc, axis=-1, keepdims=True)
            on = oc * lax.rsqrt(var + LN_EPS) * gn_ref[...]
            gate = g_ref[r, vs].astype(F32)
            o_ref[r, vs] = (on * _silu(gate)).astype(o_ref.dtype)

    @pl.when(i == pl.num_programs(1) - 1)
    def _():
        if fill is None:
            s_out_ref[...] = s_scr[...]
        else:
            for l in range(fill[1]):
                s_out_ref[l] = s_scr[...] if l == fill[0] else jnp.zeros(s_scr.shape, F32)


def _ret_tables(geo):
    c_true = min(RET_CHUNK, geo.l_true)
    half = RET_DK // 2
    inv_freq = 10000.0 ** (-jnp.linspace(0.0, 1.0, half, dtype=F32))
    pos = (geo.pos0 + jnp.arange(geo.L)).astype(F32)
    ang = pos[:, None] * inv_freq[None, :]
    lg = jnp.log(1.0 - 2.0 ** (-5.0 - jnp.arange(RET_HEADS, dtype=F32)))
    idx = jnp.arange(c_true, dtype=F32)
    diff = idx[:, None] - idx[None, :]
    dintra = jnp.exp(jnp.where(diff[None] >= 0, diff[None] * lg[:, None, None], -jnp.inf))
    qdec = jnp.exp((idx + 1.0)[None, :] * lg[:, None])
    kdec = jnp.exp((c_true - 1.0 - idx)[None, :] * lg[:, None])
    sdec = jnp.exp(c_true * lg)
    pad = RET_CHUNK - c_true
    dintra = jnp.pad(dintra, ((0, 0), (0, pad), (0, pad)))
    qdec = jnp.pad(qdec, ((0, 0), (0, pad)))[..., None]
    kdec = jnp.pad(kdec, ((0, 0), (0, pad)))[..., None]
    sdec = jnp.broadcast_to(sdec[:, None, None], (RET_HEADS, 1, RET_DV))
    return jnp.cos(ang), jnp.sin(ang), dintra, qdec, kdec, sdec


def _retention(geo, p, tables, gn_g, s_all, layer_j, prev, tl):
    B, L = geo.B, geo.L
    nblk = L // tl
    rows = min(tl, RET_CHUNK)
    nchunk = tl // rows
    has_state = s_all is not None
    cos, sin, dintra, qdec, kdec, sdec = tables
    rb = lambda b, i: b * nblk + i
    full3 = lambda b, i: (0, 0, 0)
    in_specs = [
        pl.BlockSpec((tl, RET_QW), lambda b, i: (rb(b, i), 0)),
        pl.BlockSpec((tl, RET_QW), lambda b, i: (rb(b, i), 1)),
        pl.BlockSpec((tl, RET_VW), lambda b, i: (rb(b, i), 1)),
        pl.BlockSpec((tl, RET_VW), lambda b, i: (rb(b, i), 2)),
        pl.BlockSpec((tl, RET_DK // 2), lambda b, i: (i, 0)),
        pl.BlockSpec((tl, RET_DK // 2), lambda b, i: (i, 0)),
        pl.BlockSpec((RET_HEADS, RET_CHUNK, RET_CHUNK), full3),
        pl.BlockSpec((RET_HEADS, RET_CHUNK, 1), full3),
        pl.BlockSpec((RET_HEADS, RET_CHUNK, 1), full3),
        pl.BlockSpec((RET_HEADS, 1, RET_DV), full3),
        pl.BlockSpec((1, RET_DV), lambda b, i: (0, 0)),
    ]
    args = [p, p, p, p, cos, sin, dintra, qdec, kdec, sdec, gn_g.reshape(1, RET_DV)]
    if has_state:
        in_specs.append(pl.BlockSpec((None, None, RET_HEADS, RET_DK, RET_DV),
                                     lambda b, i: (layer_j, b, 0, 0, 0)))
        args.append(s_all)
    fill, lead, lidx = _stack_fill(prev, layer_j, N_RET_LAYERS)
    o, s = _call_stacked(
        functools.partial(_ret_kernel, rows=rows, nchunk=nchunk, has_state=has_state, fill=fill),
        prev, first_stacked_out=1,
        grid=(B, nblk),
        in_specs=in_specs,
        args=args,
        out_specs=[
            pl.BlockSpec((tl, RET_VW), lambda b, i: (rb(b, i), 0)),
            pl.BlockSpec((lead, None, RET_HEADS, RET_DK, RET_DV), lambda b, i: (lidx, b, 0, 0, 0)),
        ],
        out_shape=[
            jax.ShapeDtypeStruct((geo.T, RET_VW), geo.act_dtype),
            jax.ShapeDtypeStruct((N_RET_LAYERS, B, RET_HEADS, RET_DK, RET_DV), F32),
        ],
        scratch_shapes=[pltpu.VMEM((RET_HEADS, RET_DK, RET_DV), F32)],
        name="retention",
    )
    return o, (s,)


DN_CONV_PAD = SUBLANES
CONV_COL_CHUNK = 512


def _block_diag(x_cat, n, mask):
    return jnp.where(mask, jnp.concatenate([x_cat] * n, axis=0), 0.0)


def _dn_kernel(*refs, c, rows, nchunk, l_last, has_state, fill):
    if has_state:
        (qkv_ref, z_ref, ab_ref, cw_ref, alog_ref, dtb_ref, ng_ref, s0_ref, buf_ref,
         o_ref, s_out_ref, buf_out_ref, s_scr, xp_scr, y_scr) = refs
    else:
        (qkv_ref, z_ref, ab_ref, alog_ref, dtb_ref, ng_ref, o_ref, s_out_ref, s_scr) = refs
    i = pl.program_id(1)
    hq = min(DN_V_HEADS, MXU_DIM // c)
    tl = rows * nchunk
    npair = DN_V_HEADS // 2
    W = DN_CONV_W
    P0 = DN_CONV_PAD

    @pl.when(i == 0)
    def _():
        if has_state:
            for pr in range(npair):
                s_scr[pr] = jnp.concatenate([s0_ref[2 * pr], s0_ref[2 * pr + 1]], axis=1)
            xp_scr[P0 - (W - 1):P0, :] = buf_ref[...]
        else:
            s_scr[...] = jnp.zeros_like(s_scr)

    if has_state:
        xp_scr[P0:P0 + tl, :] = qkv_ref[...].astype(F32)
        y = xp_scr[P0 - (W - 1):P0 - (W - 1) + tl, :] * cw_ref[0:1, :]
        for t in range(1, W):
            y = y + xp_scr[P0 - (W - 1) + t:P0 - (W - 1) + t + tl, :] * cw_ref[t:t + 1, :]
        y_scr[0:tl, :] = _silu(y)
        if rows < c:
            y_scr[tl:, :] = jnp.zeros((y_scr.shape[0] - tl, DN_CONV_CH), F32)

        @pl.when(i == pl.num_programs(1) - 1)
        def _():
            tail = xp_scr[P0 + l_last - (W - 1):P0 + l_last, :]
            if fill is None:
                buf_out_ref[...] = tail
            else:
                for l in range(fill[1]):
                    buf_out_ref[l] = tail if l == fill[0] else jnp.zeros(tail.shape, F32)

        xp_scr[P0 - (W - 1):P0, :] = xp_scr[P0 + tl - (W - 1):P0 + tl, :]

        def ycols(r, lo, hi):
            return y_scr[r, lo:hi]
    else:
        def ycols(r, lo, hi):
            return qkv_ref[r, lo:hi].astype(F32)

    ri = lax.broadcasted_iota(jnp.int32, (c, c), 0)
    ci_ = lax.broadcasted_iota(jnp.int32, (c, c), 1)
    incl = ri >= ci_
    strict = ri > ci_
    eye = ri == ci_
    tri_f = incl.astype(F32)
    eye_cat = jnp.concatenate([eye.astype(F32)] * hq, axis=1)
    bi = lax.broadcasted_iota(jnp.int32, (hq * c, hq * c), 0) // c
    bj = lax.broadcasted_iota(jnp.int32, (hq * c, hq * c), 1) // c
    bd_mask = bi == bj
    lane2 = lax.broadcasted_iota(jnp.int32, (DN_DK, 2 * DN_DV), 1)
    left = lane2 < DN_DV
    padded = rows < c or l_last < rows
    row_ok = lax.broadcasted_iota(jnp.int32, (c, 1), 0) < l_last if padded else None
    n_levels = c.bit_length() - 1

    nquad = DN_V_HEADS // hq
    quads = [[hq * q + t for t in range(hq)] for q in range(nquad)]

    pre = []
    for ck in range(nchunk):
        r0 = ck * c if rows == c else 0
        r = slice(r0, r0 + c)
        ab = _pad_rows(ab_ref[ck * rows:(ck + 1) * rows, :], c)
        sp = ab + dtb_ref[...]
        g_full = -jnp.exp(alog_ref[...]) * (jnp.maximum(sp, 0.0) + jnp.log1p(jnp.exp(-jnp.abs(sp))))
        beta_full = jax.nn.sigmoid(ab)
        if row_ok is not None:
            g_full = jnp.where(row_ok, g_full, 0.0)
            beta_full = jnp.where(row_ok, beta_full, 0.0)
        G_full = jnp.dot(tri_f, g_full, preferred_element_type=F32, precision=lax.Precision.HIGHEST)

        kn, qn, KK, QK = {}, {}, {}, {}
        for j in range(DN_QK_HEADS):
            qh = ycols(r, j * DN_DK, (j + 1) * DN_DK)
            kh = ycols(r, DN_QW + j * DN_DK, DN_QW + (j + 1) * DN_DK)
            qn[j] = qh * lax.rsqrt(jnp.sum(qh * qh, axis=-1, keepdims=True) + 1e-6) * (DN_DK ** -0.5)
            kn[j] = kh * lax.rsqrt(jnp.sum(kh * kh, axis=-1, keepdims=True) + 1e-6)
        for j in range(DN_QK_HEADS):
            kb16 = kn[j].astype(BF16)
            kq = _dot_nt(jnp.concatenate([kb16, qn[j].astype(BF16)], axis=0), kb16)
            KK[j], QK[j] = kq[:c], kq[c:]

        A, att, eG, Gc, rhs = {}, {}, {}, {}, {}
        for h in range(DN_V_HEADS):
            j = h // 2
            Gc[h] = G_full[:, h:h + 1]
            bc = beta_full[:, DN_V_HEADS + h:DN_V_HEADS + h + 1]
            Gr = jnp.sum(jnp.where(eye, Gc[h], 0.0), axis=0, keepdims=True)
            dec = jnp.exp(jnp.where(incl, Gc[h] - Gr, -jnp.inf))
            A[h] = jnp.where(strict, bc * KK[j] * dec, 0.0)
            att[h] = QK[j] * dec
            eG[h] = jnp.exp(Gc[h])
            vh = ycols(r, 2 * DN_QW + h * DN_DV, 2 * DN_QW + (h + 1) * DN_DV)
            rhs[h] = jnp.concatenate([vh * bc, kn[j] * (bc * eG[h])], axis=1)

        X = [jnp.concatenate([A[h] for h in hs], axis=1) for hs in quads]
        Pm = [eye_cat - x for x in X]
        X = [_dot(x.astype(BF16), _block_diag(x, hq, bd_mask).astype(BF16)) for x in X]
        for lvl in range(1, n_levels):
            Xw = [_block_diag(x, hq, bd_mask).astype(BF16) for x in X]
            if lvl < n_levels - 1:
                both = [_dot(jnp.concatenate([X[q], Pm[q]], axis=0).astype(BF16), Xw[q]) for q in range(nquad)]
                X = [b[:c] for b in both]
                Pm = [Pm[q] + both[q][c:] for q in range(nquad)]
            else:
                Pm = [Pm[q] + _dot(Pm[q].astype(BF16), Xw[q]) for q in range(nquad)]
        sol = [_dot(_block_diag(Pm[q], hq, bd_mask).astype(BF16),
                    jnp.concatenate([rhs[h] for h in quads[q]], axis=0).astype(BF16))
               for q in range(nquad)]
        att_bd = [_block_diag(jnp.concatenate([att[h] for h in quads[q]], axis=1), hq, bd_mask).astype(BF16)
                  for q in range(nquad)]
        pre.append((kn, qn, Gc, eG, sol, att_bd))

    for ck in range(nchunk):
        kn, qn, Gc, eG, sol, att_bd = pre[ck]
        sol_h = {h: sol[h // hq][c * (h % hq):c * (h % hq + 1)] for h in range(DN_V_HEADS)}
        u, oS = {}, {}
        for pr in range(npair):
            h0, h1 = 2 * pr, 2 * pr + 1
            S = s_scr[pr]
            Sbd = jnp.concatenate([jnp.where(left, S, 0.0), jnp.where(left, 0.0, S)], axis=0).astype(BF16)
            w_cat = jnp.concatenate([sol_h[h0][:, DN_DV:], sol_h[h1][:, DN_DV:]], axis=1)
            qe_cat = jnp.concatenate([qn[pr] * eG[h0], qn[pr] * eG[h1]], axis=1)
            R = _dot(jnp.concatenate([w_cat, qe_cat], axis=0).astype(BF16), Sbd)
            u[h0] = sol_h[h0][:, :DN_DV] - R[:c, :DN_DV]
            u[h1] = sol_h[h1][:, :DN_DV] - R[:c, DN_DV:]
            oS[h0] = R[c:, :DN_DV]
            oS[h1] = R[c:, DN_DV:]

        o_intra = [_dot(att_bd[q], jnp.concatenate([u[h] for h in quads[q]], axis=0).astype(BF16))
                   for q in range(nquad)]

        for pr in range(npair):
            h0, h1 = 2 * pr, 2 * pr + 1
            gl0 = Gc[h0][c - 1:c, :]
            gl1 = Gc[h1][c - 1:c, :]
            du = jnp.concatenate([u[h0] * jnp.exp(gl0 - Gc[h0]), u[h1] * jnp.exp(gl1 - Gc[h1])], axis=1)
            sdec = jnp.where(left[0:1, :], jnp.exp(gl0), jnp.exp(gl1))
            s_scr[pr] = s_scr[pr] * sdec + _dot_tn(kn[pr].astype(BF16), du.astype(BF16))

        zr = slice(ck * rows, (ck + 1) * rows)
        for h in range(DN_V_HEADS):
            t = h % hq
            o = (oS[h] + o_intra[h // hq][c * t:c * (t + 1)])[:rows]
            o = o * lax.rsqrt(jnp.mean(o * o, axis=-1, keepdims=True) + 1e-6) * ng_ref[...]
            hs = slice(h * DN_DV, (h + 1) * DN_DV)
            z = z_ref[zr, hs].astype(F32)
            o_ref[zr, hs] = (o * _silu(z)).astype(o_ref.dtype)

    @pl.when(i == pl.num_programs(1) - 1)
    def _():
        for pr in range(npair):
            for l in range(1 if fill is None else fill[1]):
                mine = fill is None or l == fill[0]
                dst = s_out_ref if fill is None else s_out_ref.at[l]
                dst[2 * pr] = s_scr[pr][:, :DN_DV] if mine else jnp.zeros((DN_DK, DN_DV), F32)
                dst[2 * pr + 1] = s_scr[pr][:, DN_DV:] if mine else jnp.zeros((DN_DK, DN_DV), F32)


def _deltanet(geo, p, ab, conv_w, a_log, dt_bias, norm_g, s_all, buf_all, layer_j, prev, tl, chunk):
    B, L = geo.B, geo.L
    nblk = L // tl
    rows = min(tl, chunk)
    nchunk = tl // rows
    l_last = geo.l_true - (nblk - 1) * tl
    assert l_last == tl or (nblk == 1 and nchunk == 1)
    has_state = s_all is not None
    rb = lambda b, i: b * nblk + i
    pad16 = lambda v: jnp.pad(v.reshape(1, DN_V_HEADS), ((0, 0), (0, LANES - DN_V_HEADS)))
    row1 = lambda n: pl.BlockSpec((1, n), lambda b, i: (0, 0))
    in_specs = [
        pl.BlockSpec((tl, DN_CONV_CH), lambda b, i: (rb(b, i), 0)),
        pl.BlockSpec((tl, DN_VW), lambda b, i: (rb(b, i), DN_CONV_CH // DN_VW)),
        pl.BlockSpec((tl, LANES), lambda b, i: (rb(b, i), 0)),
    ]
    args = [p, p, ab]
    if has_state:
        in_specs.append(pl.BlockSpec((DN_CONV_W, DN_CONV_CH), lambda b, i: (0, 0)))
        args.append(conv_w)
    in_specs += [row1(LANES), row1(LANES), row1(DN_DV)]
    args += [pad16(a_log), pad16(dt_bias), norm_g.reshape(1, DN_DV)]
    fill, lead, lidx = _stack_fill(prev, layer_j, N_DN_LAYERS)
    out_specs = [
        pl.BlockSpec((tl, DN_VW), lambda b, i: (rb(b, i), 0)),
        pl.BlockSpec((lead, None, DN_V_HEADS, DN_DK, DN_DV), lambda b, i: (lidx, b, 0, 0, 0)),
    ]
    out_shape = [
        jax.ShapeDtypeStruct((geo.T, DN_VW), geo.act_dtype),
        jax.ShapeDtypeStruct((N_DN_LAYERS, B, DN_V_HEADS, DN_DK, DN_DV), F32),
    ]
    scratch = [pltpu.VMEM((DN_V_HEADS // 2, DN_DK, 2 * DN_DV), F32)]
    if has_state:
        in_specs += [
            pl.BlockSpec((None, None, DN_V_HEADS, DN_DK, DN_DV), lambda b, i: (layer_j, b, 0, 0, 0)),
            pl.BlockSpec((None, None, DN_CONV_W - 1, DN_CONV_CH), lambda b, i: (layer_j, b, 0, 0)),
        ]
        args += [s_all, buf_all]
        out_specs.append(pl.BlockSpec((lead, None, DN_CONV_W - 1, DN_CONV_CH), lambda b, i: (lidx, b, 0, 0)))
        out_shape.append(jax.ShapeDtypeStruct((N_DN_LAYERS, B, DN_CONV_W - 1, DN_CONV_CH), F32))
        scratch += [pltpu.VMEM((DN_CONV_PAD + tl, DN_CONV_CH), F32),
                    pltpu.VMEM((max(tl, chunk), DN_CONV_CH), F32)]
    kern = functools.partial(_dn_kernel, c=chunk, rows=rows, nchunk=nchunk, l_last=l_last, has_state=has_state,
                             fill=fill)
    res = _call_stacked(kern, prev, first_stacked_out=1, grid=(B, nblk), in_specs=in_specs, args=args,
                        out_specs=out_specs, out_shape=out_shape, scratch_shapes=scratch, name="deltanet")
    return res[0], tuple(res[1:])


def _outproj_kernel(o_ref, w_ref, x_ref, gate_ref, g_ref, b_ref, out_ref):
    y = _dot(o_ref[...].astype(BF16), w_ref[...])
    z = ALPHA * x_ref[...] + (1.0 + gate_ref[...]) * y
    out_ref[...] = _layer_norm(z, g_ref[...], b_ref[...])


def _outproj_ln(geo, o, w_out, x, mod, ln_g, ln_b):
    T, tm = geo.T, geo.tm
    K = w_out.shape[0]
    vec = pl.BlockSpec((1, D_MODEL), lambda i: (0, 0))
    return pl.pallas_call(
        _outproj_kernel,
        grid=(T // tm,),
        in_specs=[
            pl.BlockSpec((tm, K), lambda i: (i, 0)),
            pl.BlockSpec((K, D_MODEL), lambda i: (0, 0)),
            pl.BlockSpec((tm, D_MODEL), lambda i: (i, 0)),
            _mod_spec(geo, 2),
            vec, vec,
        ],
        out_specs=pl.BlockSpec((tm, D_MODEL), lambda i: (i, 0)),
        out_shape=jax.ShapeDtypeStruct((T, D_MODEL), F32),
        compiler_params=_cparams(("parallel",)),
        name="out_proj_ln",
    )(o, w_out, x, mod, ln_g.reshape(1, D_MODEL), ln_b.reshape(1, D_MODEL))


MOE_ROWS = 144


def _moe_blocks(tm):
    nb = (tm - N_GROUPS) // MOE_ROWS + N_GROUPS
    ns = -(-(nb * MOE_ROWS) // LANES) * LANES
    return nb, ns


def _first_max(v, row, valid):
    m = jnp.max(jnp.where(valid, v, -jnp.inf), axis=0, keepdims=True)
    idx = jnp.min(jnp.where(valid & (v == m), row, float(N_EXPERTS)), axis=0, keepdims=True)
    return m, idx


def _route(h, wrt, br):
    logits = lax.dot_general(wrt, h, (((1,), (1,)), ((), ())), preferred_element_type=F32,
                             precision=lax.Precision.HIGHEST)
    mx = jnp.max(logits, axis=0, keepdims=True)
    ex = jnp.exp(logits - mx)
    probs = ex / jnp.sum(ex, axis=0, keepdims=True)
    sel = probs + br
    row_i = lax.broadcasted_iota(jnp.int32, sel.shape, 0)
    grp = (row_i // EXPERTS_PER_GROUP).astype(F32)
    row = row_i.astype(F32)
    best = None
    gsel = None
    for g in range(N_GROUPS):
        in_g = grp == g
        m1, i1 = _first_max(sel, row, in_g)
        m2, _ = _first_max(sel, row, in_g & (row != i1))
        score = m1 + m2
        if g == 0:
            best, gsel = score, jnp.zeros_like(i1)
        else:
            better = score > best
            gsel = jnp.where(better, float(g), gsel)
            best = jnp.where(better, score, best)
    in_grp = grp == gsel
    _, i1 = _first_max(sel, row, in_grp)
    _, i2 = _first_max(sel, row, in_grp & (row != i1))
    w1 = jnp.sum(jnp.where(row == i1, probs, 0.0), axis=0, keepdims=True)
    w2 = jnp.sum(jnp.where(row == i2, probs, 0.0), axis=0, keepdims=True)
    wsum = w1 + w2
    gates = jnp.where(row == i1, w1 / wsum, 0.0) + jnp.where(row == i2, w2 / wsum, 0.0)
    return gsel, gates


def _router_kernel(x_ref, sc_ref, sh_ref, wrt_ref, br_ref, posr_ref, posc_ref, gates_ref, tbl_ref):
    tm = x_ref.shape[0]
    h = x_ref[...] * (1.0 + sc_ref[...]) + sh_ref[...]
    gsel, gates = _route(h, wrt_ref[...], br_ref[...])
    gates_ref[...] = gates
    grow = lax.broadcasted_iota(jnp.int32, (SUBLANES, tm), 0).astype(F32)
    og = grow == gsel
    ti = lax.broadcasted_iota(jnp.int32, (tm, tm), 0)
    tj = lax.broadcasted_iota(jnp.int32, (tm, tm), 1)
    before = jnp.where(ti < tj, 1.0, 0.0).astype(BF16)
    ogf = jnp.where(og, 1.0, 0.0)
    rank = _dot(ogf.astype(BF16), before)
    cnt = jnp.sum(ogf, axis=1, keepdims=True)
    nblk = jnp.zeros_like(cnt)
    for kb in range(-(-tm // MOE_ROWS)):
        nblk = nblk + jnp.where(cnt > float(kb * MOE_ROWS), 1.0, 0.0)
    seg = nblk * float(MOE_ROWS)
    block_start = lax.broadcasted_iota(jnp.int32, (1, LANES), 1).astype(F32) * float(MOE_ROWS)
    off = jnp.zeros((1, 1), F32)
    pos = jnp.zeros((1, tm), F32)
    tbl = jnp.zeros((1, LANES), F32)
    for g in range(N_GROUPS):
        pos = pos + jnp.where(gsel == float(g), off + rank[g:g + 1, :], 0.0)
        off = off + seg[g:g + 1, :]
        tbl = tbl + jnp.where(block_start >= off, 1.0, 0.0)
    posr_ref[...] = pos
    posc_ref[...] = jnp.sum(jnp.where(ti == tj, pos, 0.0), axis=1, keepdims=True)
    tbl_ref[...] = tbl.astype(jnp.int32)


def _expert_kernel(tbl_ref, x_ref, sc_ref, sh_ref, gt_ref, posr_ref, posc_ref, gates_ref, wgu_ref, wd_ref,
                   g_ref, b_ref, out_ref, hs_scr, gs_scr, ys_scr, *, nb):
    i = pl.program_id(0)
    tm = x_ref.shape[0]
    ns = hs_scr.shape[0]
    R = MOE_ROWS
    x = x_ref[...]
    hb = (x * (1.0 + sc_ref[...]) + sh_ref[...]).astype(BF16)
    slot = lax.broadcasted_iota(jnp.int32, (ns, tm), 0).astype(F32)
    P = jnp.where(slot == posr_ref[...], 1.0, 0.0).astype(BF16)
    hs_scr[...] = _dot(P, hb).astype(BF16)
    gates = gates_ref[...]
    g_hi = gates.astype(BF16)
    g_lo = (gates - g_hi.astype(F32)).astype(BF16)
    gs_scr[...] = _dot_nt(P, g_hi) + _dot_nt(P, g_lo)
    if nb * R < ns:
        ys_scr[nb * R:, :] = jnp.zeros((ns - nb * R, D_MODEL), BF16)

    for b in range(nb):
        rs = slice(b * R, (b + 1) * R)
        g = tbl_ref[i, b]

        @pl.when(g < N_GROUPS)
        def _():
            gsb = gs_scr[rs, :]
            lane = lax.broadcasted_iota(jnp.int32, gsb.shape, 1)
            hsb = hs_scr[rs, :]
            acts = []
            for e in range(EXPERTS_PER_GROUP):
                gu = _dot(hsb, wgu_ref[g, e])
                gate = jnp.sum(jnp.where(lane == g * EXPERTS_PER_GROUP + e, gsb, 0.0), axis=1, keepdims=True)
                act = _silu(gu[:, :D_EXPERT]) * gu[:, D_EXPERT:] * gate
                acts.append(act.astype(BF16))
            ys_scr[rs, :] = _dot(jnp.concatenate(acts, axis=1), wd_ref[g]).astype(BF16)

        @pl.when(g >= N_GROUPS)
        def _():
            ys_scr[rs, :] = jnp.zeros((R, D_MODEL), BF16)

    slot_t = lax.broadcasted_iota(jnp.int32, (tm, ns), 1).astype(F32)
    PT = jnp.where(slot_t == posc_ref[...], 1.0, 0.0).astype(BF16)
    y = _dot(PT, ys_scr[...])
    z = ALPHA * x + (1.0 + gt_ref[...]) * y
    out_ref[...] = _layer_norm(z, g_ref[...], b_ref[...])


def _moe_ln(geo, x, mod, w_router_t, b_router, w_gu, w_down, layer, ln_g, ln_b):
    T, tm = geo.T, geo.tm
    nt = T // tm
    nb, ns = _moe_blocks(tm)
    posr, posc, gates, tbl = pl.pallas_call(
        _router_kernel,
        grid=(nt,),
        in_specs=[
            pl.BlockSpec((tm, D_MODEL), lambda i: (i, 0)),
            _mod_spec(geo, 4),
            _mod_spec(geo, 3),
            pl.BlockSpec((N_EXPERTS, D_MODEL), lambda i: (0, 0)),
            pl.BlockSpec((N_EXPERTS, 1), lambda i: (0, 0)),
        ],
        out_specs=[
            pl.BlockSpec((None, 1, tm), lambda i: (i, 0, 0)),
            pl.BlockSpec((tm, 1), lambda i: (i, 0)),
            pl.BlockSpec((N_EXPERTS, tm), lambda i: (0, i)),
            pl.BlockSpec((None, 1, LANES), lambda i: (i, 0, 0)),
        ],
        out_shape=[
            jax.ShapeDtypeStruct((nt, 1, tm), F32),
            jax.ShapeDtypeStruct((T, 1), F32),
            jax.ShapeDtypeStruct((N_EXPERTS, T), F32),
            jax.ShapeDtypeStruct((nt, 1, LANES), jnp.int32),
        ],
        compiler_params=_cparams(("parallel",)),
        name="moe_router",
    )(x, mod, mod, w_router_t, b_router.reshape(N_EXPERTS, 1))

    vec = pl.BlockSpec((1, D_MODEL), lambda i, t: (0, 0))
    resident = dict(pipeline_mode=pl.Buffered(1))
    grid_spec = pltpu.PrefetchScalarGridSpec(
        num_scalar_prefetch=1,
        grid=(nt,),
        in_specs=[
            pl.BlockSpec((tm, D_MODEL), lambda i, t: (i, 0)),
            _mod_spec(geo, 4),
            _mod_spec(geo, 3),
            _mod_spec(geo, 5),
            pl.BlockSpec((None, 1, tm), lambda i, t: (i, 0, 0)),
            pl.BlockSpec((tm, 1), lambda i, t: (i, 0)),
            pl.BlockSpec((N_EXPERTS, tm), lambda i, t: (0, i)),
            pl.BlockSpec((None, N_GROUPS, EXPERTS_PER_GROUP, D_MODEL, 2 * D_EXPERT),
                         lambda i, t: (layer, 0, 0, 0, 0), **resident),
            pl.BlockSpec((None, N_GROUPS, EXPERTS_PER_GROUP * D_EXPERT, D_MODEL),
                         lambda i, t: (layer, 0, 0, 0), **resident),
            vec, vec,
        ],
        out_specs=pl.BlockSpec((tm, D_MODEL), lambda i, t: (i, 0)),
        scratch_shapes=[
            pltpu.VMEM((ns, D_MODEL), BF16),
            pltpu.VMEM((ns, N_EXPERTS), F32),
            pltpu.VMEM((ns, D_MODEL), BF16),
        ],
    )
    return pl.pallas_call(
        functools.partial(_expert_kernel, nb=nb),
        grid_spec=grid_spec,
        out_shape=jax.ShapeDtypeStruct((T, D_MODEL), F32),
        compiler_params=_cparams(("parallel",)),
        name="moe_experts_ln",
    )(tbl.reshape(nt, LANES), x, mod, mod, mod, posr, posc, gates, w_gu, w_down,
      ln_g.reshape(1, D_MODEL), ln_b.reshape(1, D_MODEL))


def _trunk(geo, x, mods, s_ret, s_dn, s_conv, wts, ret_tl, dn_tl, dn_chunk):
    (ln_g, ln_b, ret_w_in, ret_gn_g, ret_w_out, dn_w_main, dn_w_tail, dn_conv_w, dn_a_log, dn_dt_bias,
     dn_norm_g, dn_w_out, router_w, router_b, moe_w_gu, moe_w_down) = wts
    tables = _ret_tables(geo)
    decode = s_dn is not None
    proj_tm = None if geo.per_token or geo.L % 1024 else 1024
    ret_stack, dn_stack, conv_tails = None, None, []
    for i in range(DEPTH):
        if geo.per_token:
            mod = jnp.repeat(mods[i], geo.L, axis=0)
        else:
            mod = mods[i].reshape(geo.B, 1, N_MOD * D_MODEL)
        j = i // 2
        if i % 2 == 0:
            p, _ = _inproj(geo, x, mod, ret_w_in[j], None, tn=1536, tm=proj_tm)
            o, ret_stack = _retention(geo, p, tables, ret_gn_g[j], s_ret, j, ret_stack, ret_tl)
            w_out = ret_w_out[j]
        else:
            if decode:
                p, ab = _inproj(geo, x, mod, dn_w_main[j], dn_w_tail[j], tn=1536)
            else:
                p, ab = _inproj(geo, x, mod, dn_w_main[j], dn_w_tail[j], tn=2048,
                                conv_w=dn_conv_w[j], conv_cols=DN_CONV_CH)
                conv_tails.append(_conv_tail(geo, x, mod, dn_w_main[j], DN_CONV_CH, tn=2048))
            o, dn_stack = _deltanet(geo, p, ab, dn_conv_w[j], dn_a_log[j], dn_dt_bias[j], dn_norm_g[j],
                                    s_dn, s_conv, j, dn_stack, dn_tl, dn_chunk)
            w_out = dn_w_out[j]
        x = _outproj_ln(geo, o, w_out, x, mod, ln_g[i, 0], ln_b[i, 0])
        x = _moe_ln(geo, x, mod, router_w, router_b, moe_w_gu, moe_w_down, i, ln_g[i, 1], ln_b[i, 1])
    if decode:
        new_dn, new_conv = dn_stack
    else:
        new_dn = dn_stack[0]
        new_conv = jnp.stack(conv_tails)[:, :, SUBLANES - (DN_CONV_W - 1):]
    return x, ret_stack[0], new_dn, new_conv


def _run(x_prompt, x_sample, state_ret, state_dn, state_conv, c_prompt, c_sample, w_ada, b_ada, ln_g, ln_b,
         ret_w_in, ret_gn_g, ret_w_out, dn_w_in, dn_conv_w, dn_a_log, dn_dt_bias, dn_norm_g, dn_w_out,
         router_w, router_b, moe_w_gu, moe_w_down, *, past_len):
    Bp, Lp, _ = x_prompt.shape
    Bs, Ls, _ = x_sample.shape
    Ls_pad = -(-Ls // SUBLANES) * SUBLANES
    tm_p = min(512, Lp)
    geo_p = _Geo(Bp, Lp, Lp, 0, tm_p, False, BF16)
    Ts = Bs * Ls_pad
    geo_s = _Geo(Bs, Ls_pad, Ls, past_len, min(512, Ts), True, F32)

    dn_w_tail = jnp.pad(dn_w_in[:, :, DN_MAIN:], ((0, 0), (0, 0), (0, LANES - 2 * DN_V_HEADS))).astype(BF16)
    w_gu_grp = moe_w_gu.astype(BF16).reshape(DEPTH, N_GROUPS, EXPERTS_PER_GROUP, D_MODEL, 2 * D_EXPERT)
    w_down_grp = moe_w_down.astype(BF16).reshape(DEPTH, N_GROUPS, EXPERTS_PER_GROUP * D_EXPERT, D_MODEL)
    wts = (ln_g, ln_b, ret_w_in.astype(BF16), ret_gn_g, ret_w_out.astype(BF16),
           dn_w_in[:, :, :DN_MAIN].astype(BF16), dn_w_tail, dn_conv_w, dn_a_log, dn_dt_bias, dn_norm_g,
           dn_w_out.astype(BF16), router_w.T, router_b, w_gu_grp, w_down_grp)

    mods = _ada(jnp.concatenate([c_prompt, c_sample], axis=0), w_ada, b_ada)

    xp = x_prompt.reshape(Bp * Lp, D_MODEL)
    yp, ret_p, dn_p, conv_p = _trunk(geo_p, xp, mods[:, :Bp], None, None, None, wts,
                                     ret_tl=min(512, Lp), dn_tl=min(256, Lp), dn_chunk=DN_CHUNK)
    xs = jnp.pad(x_sample, ((0, 0), (0, Ls_pad - Ls), (0, 0))).reshape(Ts, D_MODEL)
    ys, ret_s, dn_s, conv_s = _trunk(geo_s, xs, mods[:, Bp:], state_ret, state_dn, state_conv, wts,
                                     ret_tl=Ls_pad, dn_tl=Ls_pad, dn_chunk=max(Ls_pad, DN_DECODE_CHUNK))
    y_prompt = yp.reshape(Bp, Lp, D_MODEL)
    y_sample = ys.reshape(Bs, Ls_pad, D_MODEL)[:, :Ls]
    return (y_prompt, y_sample, ret_p, ret_s, dn_p, dn_s, conv_p, conv_s)


def kernel(x_prompt, x_sample, state_ret, state_dn, state_conv, c_prompt, c_sample, w_ada, b_ada, ln_g, ln_b,
           ret_w_in, ret_gn_g, ret_w_out, dn_w_in, dn_conv_w, dn_a_log, dn_dt_bias, dn_norm_g, dn_w_out,
           router_w, router_b, moe_w_gu, moe_w_down):
    return _run(x_prompt, x_sample, state_ret, state_dn, state_conv, c_prompt, c_sample, w_ada, b_ada,
                ln_g, ln_b, ret_w_in, ret_gn_g, ret_w_out, dn_w_in, dn_conv_w, dn_a_log, dn_dt_bias,
                dn_norm_g, dn_w_out, router_w, router_b, moe_w_gu, moe_w_down, past_len=PAST_LEN)
```

```python
import functools
from typing import NamedTuple

import jax
import jax.numpy as jnp
from jax import lax
from jax.experimental import pallas as pl
from jax.experimental.pallas import tpu as pltpu

F32 = jnp.float32
BF16 = jnp.bfloat16

D_MODEL = 1024
DEPTH = 4
N_RET_LAYERS = (DEPTH + 1) // 2
N_DN_LAYERS = DEPTH // 2
PAST_LEN = 16384
RET_HEADS = 4
RET_DK = 256
RET_DV = 512
RET_QW = RET_HEADS * RET_DK
RET_VW = RET_HEADS * RET_DV
RET_CHUNK = 256
RET_DECODE_CHUNK = 16
DN_QK_HEADS = 8
DN_V_HEADS = 16
DN_DK = 128
DN_DV = 128
DN_QW = DN_QK_HEADS * DN_DK
DN_VW = DN_V_HEADS * DN_DV
DN_CONV_CH = 2 * DN_QW + DN_VW
DN_CONV_W = 4
DN_CHUNK = 64
DN_DECODE_CHUNK = 8
DN_LOCKSTEP = 2
DN_MAIN = DN_CONV_CH + DN_VW
N_EXPERTS = 16
N_GROUPS = 4
EXPERTS_PER_GROUP = N_EXPERTS // N_GROUPS
D_EXPERT = 256
N_MOD = 6
ALPHA = (2.0 * DEPTH) ** 0.25
LN_EPS = 1e-5

LANES = 128
SUBLANES = 8
MXU_DIM = 256
VMEM_LIMIT = 56 * 1024 * 1024


class _Geo(NamedTuple):
    B: int
    L: int
    l_true: int
    pos0: int
    tm: int
    per_token: bool
    act_dtype: object

    @property
    def T(self):
        return self.B * self.L


def _cparams(sem):
    return pltpu.CompilerParams(dimension_semantics=sem, vmem_limit_bytes=VMEM_LIMIT)


def _mod_spec(geo, chunk):
    if geo.per_token:
        return pl.BlockSpec((geo.tm, D_MODEL), lambda *g: (g[0], chunk))
    tpr = geo.L // geo.tm
    return pl.BlockSpec((None, 1, D_MODEL), lambda *g: (g[0] // tpr, 0, chunk))


def _silu(x):
    hx = 0.5 * x
    return hx + hx * jnp.tanh(hx)


def _dot(a, b):
    return jnp.dot(a, b, preferred_element_type=F32)


def _dot_nt(a, b):
    return lax.dot_general(a, b, (((1,), (1,)), ((), ())), preferred_element_type=F32)


def _dot_tn(a, b):
    return lax.dot_general(a, b, (((0,), (0,)), ((), ())), preferred_element_type=F32)


def _layer_norm(z, g, b):
    mu = jnp.mean(z, axis=-1, keepdims=True)
    zc = z - mu
    var = jnp.mean(zc * zc, axis=-1, keepdims=True)
    return zc * lax.rsqrt(var + LN_EPS) * g + b


def _ada_kernel(c_ref, w_ref, b_ref, o_ref):
    cs = _silu(c_ref[...]).astype(BF16)
    o_ref[...] = _dot(cs, w_ref[...].astype(BF16)) + b_ref[...]


def _ada(c_all, w_ada, b_ada):
    R = c_all.shape[0]
    tn = 1024
    n_out = N_MOD * D_MODEL
    return pl.pallas_call(
        _ada_kernel,
        grid=(DEPTH, n_out // tn),
        in_specs=[
            pl.BlockSpec((R, D_MODEL), lambda l, j: (0, 0)),
            pl.BlockSpec((None, D_MODEL, tn), lambda l, j: (l, 0, j)),
            pl.BlockSpec((None, 1, tn), lambda l, j: (l, 0, j)),
        ],
        out_specs=pl.BlockSpec((None, R, tn), lambda l, j: (l, 0, j)),
        out_shape=jax.ShapeDtypeStruct((DEPTH, R, n_out), F32),
        compiler_params=_cparams(("parallel", "parallel")),
        name="ada_mod",
    )(c_all, w_ada, b_ada.reshape(DEPTH, 1, n_out))


def _inproj_kernel(*refs, has_tail, n_conv, tiles_per_row):
    refs = list(refs)
    x_ref, sc_ref, sh_ref, w_ref = refs[:4]
    del refs[:4]
    wt_ref = refs.pop(0) if has_tail else None
    cw_ref = refs.pop(0) if n_conv else None
    o_ref = refs.pop(0)
    ot_ref = refs.pop(0) if has_tail else None
    h_scr = refs.pop(0)
    i = pl.program_id(0)
    j = pl.program_id(1)

    @pl.when(j == 0)
    def _():
        h = x_ref[...] * (1.0 + sc_ref[...]) + sh_ref[...]
        hb = h.astype(BF16)
        h_scr[...] = hb
        if has_tail:
            ot_ref[...] = _dot(hb, wt_ref[...])

    if not n_conv:
        o_ref[...] = _dot(h_scr[...], w_ref[...]).astype(o_ref.dtype)
        return
    (carry_scr,) = refs
    tm, tn = o_ref.shape
    W = DN_CONV_W
    P0 = DN_CONV_PAD
    cw = CONV_COL_CHUNK

    @pl.when(j < n_conv)
    def _():
        row_start = (i % tiles_per_row) == 0
        prev = jnp.where(row_start, 0.0, carry_scr[j])
        for c0 in range(0, tn, cw):
            cs = slice(c0, c0 + cw)
            p = _dot(h_scr[...], w_ref[:, cs])
            y = p * cw_ref[W - 1:W, cs]
            for t in range(W - 1):
                y = y + pltpu.roll(p, W - 1 - t, 0) * cw_ref[t:t + 1, cs]
            o_ref[:, cs] = _silu(y).astype(o_ref.dtype)
            nfix = 2 * SUBLANES
            head = jnp.concatenate([prev[:, cs], p[:nfix]], axis=0)
            y0 = head[P0:] * cw_ref[W - 1:W, cs]
            for t in range(W - 1):
                y0 = y0 + head[P0 - (W - 1 - t):P0 + nfix - (W - 1 - t)] * cw_ref[t:t + 1, cs]
            o_ref[0:nfix, cs] = _silu(y0).astype(o_ref.dtype)
            carry_scr[j, :, cs] = p[tm - P0:]

    @pl.when(j >= n_conv)
    def _():
        o_ref[...] = _dot(h_scr[...], w_ref[...]).astype(o_ref.dtype)


def _inproj(geo, x, mod, w, w_tail, tn, conv_w=None, conv_cols=0, tm=None):
    if tm is not None:
        geo = geo._replace(tm=tm)
    T, tm = geo.T, geo.tm
    N = w.shape[1]
    has_tail = w_tail is not None
    n_conv = conv_cols // tn
    in_specs = [
        pl.BlockSpec((tm, D_MODEL), lambda i, j: (i, 0)),
        _mod_spec(geo, 1),
        _mod_spec(geo, 0),
        pl.BlockSpec((D_MODEL, tn), lambda i, j: (0, j)),
    ]
    args = [x, mod, mod, w]
    out_specs = [pl.BlockSpec((tm, tn), lambda i, j: (i, j))]
    out_shape = [jax.ShapeDtypeStruct((T, N), geo.act_dtype)]
    scratch = [pltpu.VMEM((tm, D_MODEL), BF16)]
    if has_tail:
        in_specs.append(pl.BlockSpec((D_MODEL, LANES), lambda i, j: (0, 0)))
        args.append(w_tail)
        out_specs.append(pl.BlockSpec((tm, LANES), lambda i, j: (i, 0)))
        out_shape.append(jax.ShapeDtypeStruct((T, LANES), F32))
    if n_conv:
        assert conv_cols == n_conv * tn and not geo.per_token
        in_specs.append(pl.BlockSpec((DN_CONV_W, tn), lambda i, j: (0, jnp.minimum(j, n_conv - 1))))
        args.append(conv_w)
        scratch += [pltpu.VMEM((n_conv, DN_CONV_PAD, tn), F32)]
    res = pl.pallas_call(
        functools.partial(_inproj_kernel, has_tail=has_tail, n_conv=n_conv, tiles_per_row=geo.L // tm),
        grid=(T // tm, N // tn),
        in_specs=in_specs,
        out_specs=out_specs,
        out_shape=out_shape,
        scratch_shapes=scratch,
        compiler_params=_cparams(("arbitrary", "arbitrary")),
        name="in_proj",
    )(*args)
    return res if has_tail else (res[0], None)


def _conv_tail_kernel(x_ref, sc_ref, sh_ref, w_ref, o_ref):
    h = x_ref[...] * (1.0 + sc_ref[...]) + sh_ref[...]
    o_ref[...] = _dot(h.astype(BF16), w_ref[...])


def _conv_tail(geo, x, mod, w, n_cols, tn):
    B, L = geo.B, geo.L
    blocks_per_row = L // SUBLANES
    return pl.pallas_call(
        _conv_tail_kernel,
        grid=(n_cols // tn, B),
        in_specs=[
            pl.BlockSpec((SUBLANES, D_MODEL), lambda j, b: (b * blocks_per_row + blocks_per_row - 1, 0)),
            pl.BlockSpec((None, 1, D_MODEL), lambda j, b: (b, 0, 1)),
            pl.BlockSpec((None, 1, D_MODEL), lambda j, b: (b, 0, 0)),
            pl.BlockSpec((D_MODEL, tn), lambda j, b: (0, j)),
        ],
        out_specs=pl.BlockSpec((None, SUBLANES, tn), lambda j, b: (b, 0, j)),
        out_shape=jax.ShapeDtypeStruct((B, SUBLANES, n_cols), F32),
        compiler_params=_cparams(("parallel", "parallel")),
        name="conv_tail",
    )(x, mod, mod, w)


def _call_stacked(kern, prev, first_stacked_out, *, grid, in_specs, args, out_specs, out_shape,
                  scratch_shapes, name):
    n_in = len(args)
    aliases = {}
    body = kern
    if prev is not None:
        n_prev = len(prev)
        in_specs = list(in_specs) + [pl.BlockSpec(memory_space=pl.ANY)] * n_prev
        args = list(args) + list(prev)
        aliases = {n_in + k: first_stacked_out + k for k in range(n_prev)}

        def body(*refs):
            kern(*refs[:n_in], *refs[n_in + n_prev:])

    return pl.pallas_call(
        body, grid=grid, in_specs=in_specs, out_specs=out_specs, out_shape=out_shape,
        scratch_shapes=scratch_shapes, input_output_aliases=aliases,
        compiler_params=_cparams(("parallel", "arbitrary")), name=name,
    )(*args)


def _stack_fill(prev, layer_j, n_layers):
    if prev is None:
        return (layer_j, n_layers), n_layers, 0
    return None, None, layer_j


def _pad_rows(x, rows):
    if x.shape[0] == rows:
        return x
    return jnp.concatenate([x, jnp.zeros((rows - x.shape[0], x.shape[1]), x.dtype)], axis=0)


def _ret_kernel(*refs, c, rows, nchunk, has_state, fill):
    if has_state:
        (q_ref, k_ref, v_ref, g_ref, cos_ref, sin_ref, dintra_ref, qdec_ref, kdec_ref, sdec_ref, gn_ref,
         s0_ref, o_ref, s_out_ref, s_scr) = refs
    else:
        (q_ref, k_ref, v_ref, g_ref, cos_ref, sin_ref, dintra_ref, qdec_ref, kdec_ref, sdec_ref, gn_ref,
         o_ref, s_out_ref, s_scr) = refs
    i = pl.program_id(1)
    half = RET_DK // 2

    @pl.when(i == 0)
    def _():
        if has_state:
            s_scr[...] = s0_ref[...]
        else:
            s_scr[...] = jnp.zeros_like(s_scr)

    def rot(ref, r, h, cos, sin):
        x1 = ref[r, h * RET_DK:h * RET_DK + half].astype(F32)
        x2 = ref[r, h * RET_DK + half:(h + 1) * RET_DK].astype(F32)
        return jnp.concatenate([x1 * cos - x2 * sin, x2 * cos + x1 * sin], axis=1)

    for ci in range(nchunk):
        r = slice(ci * rows, (ci + 1) * rows)
        cos = cos_ref[r, :]
        sin = sin_ref[r, :]
        for h in range(RET_HEADS):
            vs = slice(h * RET_DV, (h + 1) * RET_DV)
            q = _pad_rows(rot(q_ref, r, h, cos, sin), c)
            k = _pad_rows(rot(k_ref, r, h, cos, sin), c) * (RET_DK ** -0.5)
            if rows == c:
                v = v_ref[r, vs].astype(BF16)
            else:
                v = _pad_rows(v_ref[r, vs].astype(F32), c).astype(BF16)
            s = s_scr[h]
            att = _dot_nt(q.astype(BF16), k.astype(BF16)) * dintra_ref[h]
            o = _dot(att.astype(BF16), v) + _dot((q * qdec_ref[h]).astype(BF16), s.astype(BF16))
            s_scr[h] = s * sdec_ref[h] + _dot_tn((k * kdec_ref[h]).astype(BF16), v)
            o = o[:rows]
            mu = jnp.mean(o, axis=-1, keepdims=True)
            oc = o - mu
            var = jnp.mean(oc * oc, axis=-1, keepdims=True)
            on = oc * lax.rsqrt(var + LN_EPS) * gn_ref[...]
            gate = g_ref[r, vs].astype(F32)
            o_ref[r, vs] = (on * _silu(gate)).astype(o_ref.dtype)

    @pl.when(i == pl.num_programs(1) - 1)
    def _():
        if fill is None:
            s_out_ref[...] = s_scr[...]
        else:
            for l in range(fill[1]):
                s_out_ref[l] = s_scr[...] if l == fill[0] else jnp.zeros(s_scr.shape, F32)


def _ret_tables(geo, chunk):
    c_true = min(chunk, geo.l_true)
    half = RET_DK // 2
    inv_freq = 10000.0 ** (-jnp.linspace(0.0, 1.0, half, dtype=F32))
    pos = (geo.pos0 + jnp.arange(geo.L)).astype(F32)
    ang = pos[:, None] * inv_freq[None, :]
    lg = jnp.log(1.0 - 2.0 ** (-5.0 - jnp.arange(RET_HEADS, dtype=F32)))
    idx = jnp.arange(c_true, dtype=F32)
    diff = idx[:, None] - idx[None, :]
    dintra = jnp.exp(jnp.where(diff[None] >= 0, diff[None] * lg[:, None, None], -jnp.inf))
    qdec = jnp.exp((idx + 1.0)[None, :] * lg[:, None])
    kdec = jnp.exp((c_true - 1.0 - idx)[None, :] * lg[:, None])
    sdec = jnp.exp(c_true * lg)
    pad = chunk - c_true
    dintra = jnp.pad(dintra, ((0, 0), (0, pad), (0, pad)))
    qdec = jnp.pad(qdec, ((0, 0), (0, pad)))[..., None]
    kdec = jnp.pad(kdec, ((0, 0), (0, pad)))[..., None]
    sdec = jnp.broadcast_to(sdec[:, None, None], (RET_HEADS, 1, RET_DV))
    return jnp.cos(ang), jnp.sin(ang), dintra, qdec, kdec, sdec


def _retention(geo, p, tables, gn_g, s_all, layer_j, prev, tl, chunk):
    B, L = geo.B, geo.L
    nblk = L // tl
    rows = min(tl, chunk)
    nchunk = tl // rows
    has_state = s_all is not None
    cos, sin, dintra, qdec, kdec, sdec = tables
    rb = lambda b, i: b * nblk + i
    full3 = lambda b, i: (0, 0, 0)
    in_specs = [
        pl.BlockSpec((tl, RET_QW), lambda b, i: (rb(b, i), 0)),
        pl.BlockSpec((tl, RET_QW), lambda b, i: (rb(b, i), 1)),
        pl.BlockSpec((tl, RET_VW), lambda b, i: (rb(b, i), 1)),
        pl.BlockSpec((tl, RET_VW), lambda b, i: (rb(b, i), 2)),
        pl.BlockSpec((tl, RET_DK // 2), lambda b, i: (i, 0)),
        pl.BlockSpec((tl, RET_DK // 2), lambda b, i: (i, 0)),
        pl.BlockSpec((RET_HEADS, chunk, chunk), full3),
        pl.BlockSpec((RET_HEADS, chunk, 1), full3),
        pl.BlockSpec((RET_HEADS, chunk, 1), full3),
        pl.BlockSpec((RET_HEADS, 1, RET_DV), full3),
        pl.BlockSpec((1, RET_DV), lambda b, i: (0, 0)),
    ]
    args = [p, p, p, p, cos, sin, dintra, qdec, kdec, sdec, gn_g.reshape(1, RET_DV)]
    if has_state:
        in_specs.append(pl.BlockSpec((None, None, RET_HEADS, RET_DK, RET_DV),
                                     lambda b, i: (layer_j, b, 0, 0, 0)))
        args.append(s_all)
    fill, lead, lidx = _stack_fill(prev, layer_j, N_RET_LAYERS)
    o, s = _call_stacked(
        functools.partial(_ret_kernel, c=chunk, rows=rows, nchunk=nchunk, has_state=has_state, fill=fill),
        prev, first_stacked_out=1,
        grid=(B, nblk),
        in_specs=in_specs,
        args=args,
        out_specs=[
            pl.BlockSpec((tl, RET_VW), lambda b, i: (rb(b, i), 0)),
            pl.BlockSpec((lead, None, RET_HEADS, RET_DK, RET_DV), lambda b, i: (lidx, b, 0, 0, 0)),
        ],
        out_shape=[
            jax.ShapeDtypeStruct((geo.T, RET_VW), geo.act_dtype),
            jax.ShapeDtypeStruct((N_RET_LAYERS, B, RET_HEADS, RET_DK, RET_DV), F32),
        ],
        scratch_shapes=[pltpu.VMEM((RET_HEADS, RET_DK, RET_DV), F32)],
        name="retention",
    )
    return o, (s,)


DN_CONV_PAD = SUBLANES
CONV_COL_CHUNK = 512


def _block_diag(x_cat, n, mask):
    return jnp.where(mask, jnp.concatenate([x_cat] * n, axis=0), jnp.zeros((), x_cat.dtype))


def _dn_kernel(*refs, c, rows, nchunk, l_last, has_state, fill):
    if has_state:
        (qkv_ref, z_ref, ab_ref, cw_ref, alog_ref, dtb_ref, ng_ref, s0_ref, buf_ref,
         o_ref, s_out_ref, buf_out_ref, s_scr, xp_scr, y_scr) = refs
    else:
        (qkv_ref, z_ref, ab_ref, alog_ref, dtb_ref, ng_ref, o_ref, s_out_ref, s_scr) = refs
    i = pl.program_id(1)
    hq = min(DN_V_HEADS, MXU_DIM // c)
    tl = rows * nchunk
    npair = DN_V_HEADS // 2
    W = DN_CONV_W
    P0 = DN_CONV_PAD

    @pl.when(i == 0)
    def _():
        if has_state:
            for pr in range(npair):
                s_scr[pr] = jnp.concatenate([s0_ref[2 * pr], s0_ref[2 * pr + 1]], axis=1)
            xp_scr[P0 - (W - 1):P0, :] = buf_ref[...]
        else:
            s_scr[...] = jnp.zeros_like(s_scr)

    if has_state:
        xp_scr[P0:P0 + tl, :] = qkv_ref[...].astype(F32)
        y = xp_scr[P0 - (W - 1):P0 - (W - 1) + tl, :] * cw_ref[0:1, :]
        for t in range(1, W):
            y = y + xp_scr[P0 - (W - 1) + t:P0 - (W - 1) + t + tl, :] * cw_ref[t:t + 1, :]
        y_scr[0:tl, :] = _silu(y)
        if rows < c:
            y_scr[tl:, :] = jnp.zeros((y_scr.shape[0] - tl, DN_CONV_CH), F32)

        @pl.when(i == pl.num_programs(1) - 1)
        def _():
            tail = xp_scr[P0 + l_last - (W - 1):P0 + l_last, :]
            if fill is None:
                buf_out_ref[...] = tail
            else:
                for l in range(fill[1]):
                    buf_out_ref[l] = tail if l == fill[0] else jnp.zeros(tail.shape, F32)

        xp_scr[P0 - (W - 1):P0, :] = xp_scr[P0 + tl - (W - 1):P0 + tl, :]

        def ycols(r, lo, hi):
            return y_scr[r, lo:hi]
    else:
        def ycols(r, lo, hi):
            return qkv_ref[r, lo:hi].astype(F32)

    ri = lax.broadcasted_iota(jnp.int32, (c, c), 0)
    ci_ = lax.broadcasted_iota(jnp.int32, (c, c), 1)
    incl = ri >= ci_
    strict = ri > ci_
    eye = ri == ci_
    tri_f = incl.astype(F32)
    eye_cat = jnp.concatenate([eye.astype(F32)] * hq, axis=1)
    bi = lax.broadcasted_iota(jnp.int32, (hq * c, hq * c), 0) // c
    bj = lax.broadcasted_iota(jnp.int32, (hq * c, hq * c), 1) // c
    bd_mask = bi == bj
    lane2 = lax.broadcasted_iota(jnp.int32, (DN_DK, 2 * DN_DV), 1)
    left = lane2 < DN_DV
    padded = rows < c or l_last < rows
    row_ok = lax.broadcasted_iota(jnp.int32, (c, 1), 0) < l_last if padded else None
    n_levels = c.bit_length() - 1

    ones_w = jnp.ones((DN_DK, DN_DK), BF16)

    def sumsq(x):
        if has_state:
            return jnp.sum(x * x, axis=-1, keepdims=True)
        return _dot((x * x).astype(BF16), ones_w)

    nquad = DN_V_HEADS // hq
    quads = [[hq * q + t for t in range(hq)] for q in range(nquad)]

    def phase1(ck):
        r0 = ck * c if rows == c else 0
        r = slice(r0, r0 + c)
        ab = _pad_rows(ab_ref[ck * rows:(ck + 1) * rows, :], c)
        sp = ab + dtb_ref[...]
        g_full = -jnp.exp(alog_ref[...]) * (jnp.maximum(sp, 0.0) + jnp.log1p(jnp.exp(-jnp.abs(sp))))
        beta_full = jax.nn.sigmoid(ab)
        if row_ok is not None:
            g_full = jnp.where(row_ok, g_full, 0.0)
            beta_full = jnp.where(row_ok, beta_full, 0.0)
        G_full = jnp.dot(tri_f, g_full, preferred_element_type=F32, precision=lax.Precision.HIGHEST)
        yield

        kn, qn, KK, QK = {}, {}, {}, {}
        for j in range(DN_QK_HEADS):
            qh = ycols(r, j * DN_DK, (j + 1) * DN_DK)
            kh = ycols(r, DN_QW + j * DN_DK, DN_QW + (j + 1) * DN_DK)
            qn[j] = qh * lax.rsqrt(sumsq(qh) + 1e-6) * (DN_DK ** -0.5)
            kn[j] = kh * lax.rsqrt(sumsq(kh) + 1e-6)
        yield
        for j in range(DN_QK_HEADS):
            kb16 = kn[j].astype(BF16)
            kq = _dot_nt(jnp.concatenate([kb16, qn[j].astype(BF16)], axis=0), kb16)
            KK[j], QK[j] = kq[:c], kq[c:]
        yield

        A, att, eG, Gc, rhs = {}, {}, {}, {}, {}
        for h in range(DN_V_HEADS):
            j = h // 2
            Gc[h] = jnp.broadcast_to(G_full[:, h:h + 1], (c, DN_DV))
            bc = jnp.broadcast_to(beta_full[:, DN_V_HEADS + h:DN_V_HEADS + h + 1], (c, DN_DV))
            Gcc = Gc[h][:, :c]
            Gr = jnp.sum(jnp.where(eye, Gcc, 0.0), axis=0, keepdims=True)
            dec = jnp.exp(jnp.where(incl, Gcc - Gr, -jnp.inf))
            A[h] = jnp.where(strict, bc[:, :c] * KK[j] * dec, 0.0)
            att[h] = QK[j] * dec
            eG[h] = jnp.exp(Gc[h])
            vh = ycols(r, 2 * DN_QW + h * DN_DV, 2 * DN_QW + (h + 1) * DN_DV)
            rhs[h] = jnp.concatenate([vh * bc, kn[j] * (bc * eG[h])], axis=1)
        yield

        X = [jnp.concatenate([A[h] for h in hs], axis=1) for hs in quads]
        Pm = [eye_cat - x for x in X]
        X = [_dot(x.astype(BF16), _block_diag(x, hq, bd_mask).astype(BF16)) for x in X]
        yield
        for lvl in range(1, n_levels):
            Xw = [_block_diag(x, hq, bd_mask).astype(BF16) for x in X]
            if lvl < n_levels - 1:
                both = [_dot(jnp.concatenate([X[q], Pm[q]], axis=0).astype(BF16), Xw[q]) for q in range(nquad)]
                X = [b[:c] for b in both]
                Pm = [Pm[q] + both[q][c:] for q in range(nquad)]
            else:
                Pm = [Pm[q] + _dot(Pm[q].astype(BF16), Xw[q]) for q in range(nquad)]
            yield
        sol = [_dot(_block_diag(Pm[q], hq, bd_mask).astype(BF16),
                    jnp.concatenate([rhs[h] for h in quads[q]], axis=0).astype(BF16))
               for q in range(nquad)]
        att_bd = [_block_diag(jnp.concatenate([att[h] for h in quads[q]], axis=1), hq, bd_mask).astype(BF16)
                  for q in range(nquad)]
        return kn, qn, Gc, eG, sol, att_bd

    def phase2(ck, pre):
        kn, qn, Gc, eG, sol, att_bd = pre
        sol_h = {h: sol[h // hq][c * (h % hq):c * (h % hq + 1)] for h in range(DN_V_HEADS)}
        u, oS = {}, {}
        for pr in range(npair):
            h0, h1 = 2 * pr, 2 * pr + 1
            S = s_scr[pr]
            Sbd = jnp.concatenate([jnp.where(left, S, 0.0), jnp.where(left, 0.0, S)], axis=0).astype(BF16)
            w_cat = jnp.concatenate([sol_h[h0][:, DN_DV:], sol_h[h1][:, DN_DV:]], axis=1)
            qe_cat = jnp.concatenate([qn[pr] * eG[h0], qn[pr] * eG[h1]], axis=1)
            R = _dot(jnp.concatenate([w_cat, qe_cat], axis=0).astype(BF16), Sbd)
            u[h0] = sol_h[h0][:, :DN_DV] - R[:c, :DN_DV]
            u[h1] = sol_h[h1][:, :DN_DV] - R[:c, DN_DV:]
            oS[h0] = R[c:, :DN_DV]
            oS[h1] = R[c:, DN_DV:]

        o_intra = [_dot(att_bd[q], jnp.concatenate([u[h] for h in quads[q]], axis=0).astype(BF16))
                   for q in range(nquad)]

        for pr in range(npair):
            h0, h1 = 2 * pr, 2 * pr + 1
            gl0 = Gc[h0][c - 1:c, :]
            gl1 = Gc[h1][c - 1:c, :]
            du = jnp.concatenate([u[h0] * jnp.exp(gl0 - Gc[h0]), u[h1] * jnp.exp(gl1 - Gc[h1])], axis=1)
            sdec = jnp.concatenate([jnp.exp(gl0), jnp.exp(gl1)], axis=1)
            s_scr[pr] = s_scr[pr] * sdec + _dot_tn(kn[pr].astype(BF16), du.astype(BF16))

        zr = slice(ck * rows, (ck + 1) * rows)
        for h in range(DN_V_HEADS):
            t = h % hq
            o = (oS[h] + o_intra[h // hq][c * t:c * (t + 1)])[:rows]
            o = o * lax.rsqrt(sumsq(o) * (1.0 / DN_DV) + 1e-6) * ng_ref[...]
            hs = slice(h * DN_DV, (h + 1) * DN_DV)
            z = z_ref[zr, hs].astype(F32)
            o_ref[zr, hs] = (o * _silu(z)).astype(o_ref.dtype)

    for g0 in range(0, nchunk, DN_LOCKSTEP):
        group = list(range(g0, min(g0 + DN_LOCKSTEP, nchunk)))
        gens = {ck: phase1(ck) for ck in group}
        pre = {}
        while gens:
            for ck in list(gens):
                try:
                    next(gens[ck])
                except StopIteration as done:
                    pre[ck] = done.value
                    del gens[ck]
        for ck in group:
            phase2(ck, pre[ck])

    @pl.when(i == pl.num_programs(1) - 1)
    def _():
        for pr in range(npair):
            for l in range(1 if fill is None else fill[1]):
                mine = fill is None or l == fill[0]
                dst = s_out_ref if fill is None else s_out_ref.at[l]
                dst[2 * pr] = s_scr[pr][:, :DN_DV] if mine else jnp.zeros((DN_DK, DN_DV), F32)
                dst[2 * pr + 1] = s_scr[pr][:, DN_DV:] if mine else jnp.zeros((DN_DK, DN_DV), F32)


def _deltanet(geo, p, ab, conv_w, a_log, dt_bias, norm_g, s_all, buf_all, layer_j, prev, tl, chunk):
    B, L = geo.B, geo.L
    nblk = L // tl
    rows = min(tl, chunk)
    nchunk = tl // rows
    l_last = geo.l_true - (nblk - 1) * tl
    assert l_last == tl or (nblk == 1 and nchunk == 1)
    has_state = s_all is not None
    rb = lambda b, i: b * nblk + i
    pad16 = lambda v: jnp.pad(v.reshape(1, DN_V_HEADS), ((0, 0), (0, LANES - DN_V_HEADS)))
    row1 = lambda n: pl.BlockSpec((1, n), lambda b, i: (0, 0))
    in_specs = [
        pl.BlockSpec((tl, DN_CONV_CH), lambda b, i: (rb(b, i), 0)),
        pl.BlockSpec((tl, DN_VW), lambda b, i: (rb(b, i), DN_CONV_CH // DN_VW)),
        pl.BlockSpec((tl, LANES), lambda b, i: (rb(b, i), 0)),
    ]
    args = [p, p, ab]
    if has_state:
        in_specs.append(pl.BlockSpec((DN_CONV_W, DN_CONV_CH), lambda b, i: (0, 0)))
        args.append(conv_w)
    in_specs += [row1(LANES), row1(LANES), row1(DN_DV)]
    args += [pad16(a_log), pad16(dt_bias), norm_g.reshape(1, DN_DV)]
    fill, lead, lidx = _stack_fill(prev, layer_j, N_DN_LAYERS)
    out_specs = [
        pl.BlockSpec((tl, DN_VW), lambda b, i: (rb(b, i), 0)),
        pl.BlockSpec((lead, None, DN_V_HEADS, DN_DK, DN_DV), lambda b, i: (lidx, b, 0, 0, 0)),
    ]
    out_shape = [
        jax.ShapeDtypeStruct((geo.T, DN_VW), geo.act_dtype),
        jax.ShapeDtypeStruct((N_DN_LAYERS, B, DN_V_HEADS, DN_DK, DN_DV), F32),
    ]
    scratch = [pltpu.VMEM((DN_V_HEADS // 2, DN_DK, 2 * DN_DV), F32)]
    if has_state:
        in_specs += [
            pl.BlockSpec((None, None, DN_V_HEADS, DN_DK, DN_DV), lambda b, i: (layer_j, b, 0, 0, 0)),
            pl.BlockSpec((None, None, DN_CONV_W - 1, DN_CONV_CH), lambda b, i: (layer_j, b, 0, 0)),
        ]
        args += [s_all, buf_all]
        out_specs.append(pl.BlockSpec((lead, None, DN_CONV_W - 1, DN_CONV_CH), lambda b, i: (lidx, b, 0, 0)))
        out_shape.append(jax.ShapeDtypeStruct((N_DN_LAYERS, B, DN_CONV_W - 1, DN_CONV_CH), F32))
        scratch += [pltpu.VMEM((DN_CONV_PAD + tl, DN_CONV_CH), F32),
                    pltpu.VMEM((max(tl, chunk), DN_CONV_CH), F32)]
    kern = functools.partial(_dn_kernel, c=chunk, rows=rows, nchunk=nchunk, l_last=l_last, has_state=has_state,
                             fill=fill)
    res = _call_stacked(kern, prev, first_stacked_out=1, grid=(B, nblk), in_specs=in_specs, args=args,
                        out_specs=out_specs, out_shape=out_shape, scratch_shapes=scratch, name="deltanet")
    return res[0], tuple(res[1:])


def _outproj_kernel(o_ref, w_ref, x_ref, gate_ref, g_ref, b_ref, out_ref):
    y = _dot(o_ref[...].astype(BF16), w_ref[...])
    z = ALPHA * x_ref[...] + (1.0 + gate_ref[...]) * y
    out_ref[...] = _layer_norm(z, g_ref[...], b_ref[...])


def _outproj_ln(geo, o, w_out, x, mod, ln_g, ln_b):
    T, tm = geo.T, geo.tm
    K = w_out.shape[0]
    vec = pl.BlockSpec((1, D_MODEL), lambda i: (0, 0))
    return pl.pallas_call(
        _outproj_kernel,
        grid=(T // tm,),
        in_specs=[
            pl.BlockSpec((tm, K), lambda i: (i, 0)),
            pl.BlockSpec((K, D_MODEL), lambda i: (0, 0)),
            pl.BlockSpec((tm, D_MODEL), lambda i: (i, 0)),
            _mod_spec(geo, 2),
            vec, vec,
        ],
        out_specs=pl.BlockSpec((tm, D_MODEL), lambda i: (i, 0)),
        out_shape=jax.ShapeDtypeStruct((T, D_MODEL), F32),
        compiler_params=_cparams(("parallel",)),
        name="out_proj_ln",
    )(o, w_out, x, mod, ln_g.reshape(1, D_MODEL), ln_b.reshape(1, D_MODEL))


MOE_ROWS = 144


def _moe_blocks(tm):
    nb = (tm - N_GROUPS) // MOE_ROWS + N_GROUPS
    ns = -(-(nb * MOE_ROWS) // LANES) * LANES
    return nb, ns


def _first_max(v, row, valid):
    m = jnp.max(jnp.where(valid, v, -jnp.inf), axis=0, keepdims=True)
    idx = jnp.min(jnp.where(valid & (v == m), row, float(N_EXPERTS)), axis=0, keepdims=True)
    return m, idx


def _route(h, wrt, br):
    logits = lax.dot_general(wrt, h, (((1,), (1,)), ((), ())), preferred_element_type=F32,
                             precision=lax.Precision.HIGHEST)
    mx = jnp.max(logits, axis=0, keepdims=True)
    ex = jnp.exp(logits - mx)
    probs = ex / jnp.sum(ex, axis=0, keepdims=True)
    sel = probs + br
    row_i = lax.broadcasted_iota(jnp.int32, sel.shape, 0)
    grp = (row_i // EXPERTS_PER_GROUP).astype(F32)
    row = row_i.astype(F32)
    best = None
    gsel = None
    for g in range(N_GROUPS):
        in_g = grp == g
        m1, i1 = _first_max(sel, row, in_g)
        m2, _ = _first_max(sel, row, in_g & (row != i1))
        score = m1 + m2
        if g == 0:
            best, gsel = score, jnp.zeros_like(i1)
        else:
            better = score > best
            gsel = jnp.where(better, float(g), gsel)
            best = jnp.where(better, score, best)
    in_grp = grp == gsel
    _, i1 = _first_max(sel, row, in_grp)
    _, i2 = _first_max(sel, row, in_grp & (row != i1))
    w1 = jnp.sum(jnp.where(row == i1, probs, 0.0), axis=0, keepdims=True)
    w2 = jnp.sum(jnp.where(row == i2, probs, 0.0), axis=0, keepdims=True)
    wsum = w1 + w2
    gates = jnp.where(row == i1, w1 / wsum, 0.0) + jnp.where(row == i2, w2 / wsum, 0.0)
    return gsel, gates


def _router_kernel(x_ref, sc_ref, sh_ref, wrt_ref, br_ref, posr_ref, posc_ref, gates_ref, tbl_ref):
    tm = x_ref.shape[0]
    h = x_ref[...] * (1.0 + sc_ref[...]) + sh_ref[...]
    gsel, gates = _route(h, wrt_ref[...], br_ref[...])
    gates_ref[...] = gates
    grow = lax.broadcasted_iota(jnp.int32, (SUBLANES, tm), 0).astype(F32)
    og = grow == gsel
    ti = lax.broadcasted_iota(jnp.int32, (tm, tm), 0)
    tj = lax.broadcasted_iota(jnp.int32, (tm, tm), 1)
    before = jnp.where(ti < tj, 1.0, 0.0).astype(BF16)
    ogf = jnp.where(og, 1.0, 0.0)
    rank = _dot(ogf.astype(BF16), before)
    cnt = jnp.sum(ogf, axis=1, keepdims=True)
    nblk = jnp.zeros_like(cnt)
    for kb in range(-(-tm // MOE_ROWS)):
        nblk = nblk + jnp.where(cnt > float(kb * MOE_ROWS), 1.0, 0.0)
    seg = nblk * float(MOE_ROWS)
    block_start = lax.broadcasted_iota(jnp.int32, (1, LANES), 1).astype(F32) * float(MOE_ROWS)
    off = jnp.zeros((1, 1), F32)
    pos = jnp.zeros((1, tm), F32)
    tbl = jnp.zeros((1, LANES), F32)
    for g in range(N_GROUPS):
        pos = pos + jnp.where(gsel == float(g), off + rank[g:g + 1, :], 0.0)
        off = off + seg[g:g + 1, :]
        tbl = tbl + jnp.where(block_start >= off, 1.0, 0.0)
    posr_ref[...] = pos
    posc_ref[...] = jnp.sum(jnp.where(ti == tj, pos, 0.0), axis=1, keepdims=True)
    tbl_ref[...] = tbl.astype(jnp.int32)


def _expert_kernel(tbl_ref, x_ref, sc_ref, sh_ref, gt_ref, posr_ref, posc_ref, gates_ref, wgu_ref, wd_ref,
                   g_ref, b_ref, out_ref, hs_scr, gs_scr, ys_scr, *, nb):
    i = pl.program_id(0)
    tm = x_ref.shape[0]
    ns = hs_scr.shape[0]
    R = MOE_ROWS
    x = x_ref[...]
    hb = (x * (1.0 + sc_ref[...]) + sh_ref[...]).astype(BF16)
    slot = lax.broadcasted_iota(jnp.int32, (ns, tm), 0).astype(F32)
    P = jnp.where(slot == posr_ref[...], 1.0, 0.0).astype(BF16)
    hs_scr[...] = _dot(P, hb).astype(BF16)
    gates = gates_ref[...]
    g_hi = gates.astype(BF16)
    g_lo = (gates - g_hi.astype(F32)).astype(BF16)
    gs_scr[...] = _dot_nt(P, g_hi) + _dot_nt(P, g_lo)
    if nb * R < ns:
        ys_scr[nb * R:, :] = jnp.zeros((ns - nb * R, D_MODEL), BF16)

    for b in range(nb):
        rs = slice(b * R, (b + 1) * R)
        g = tbl_ref[i, b]

        @pl.when(g < N_GROUPS)
        def _():
            gsb = gs_scr[rs, :]
            lane = lax.broadcasted_iota(jnp.int32, gsb.shape, 1)
            hsb = hs_scr[rs, :]
            acts = []
            for e in range(EXPERTS_PER_GROUP):
                gu = _dot(hsb, wgu_ref[g, e])
                gate = jnp.sum(jnp.where(lane == g * EXPERTS_PER_GROUP + e, gsb, 0.0), axis=1, keepdims=True)
                act = _silu(gu[:, :D_EXPERT]) * gu[:, D_EXPERT:] * gate
                acts.append(act.astype(BF16))
            ys_scr[rs, :] = _dot(jnp.concatenate(acts, axis=1), wd_ref[g]).astype(BF16)

        @pl.when(g >= N_GROUPS)
        def _():
            ys_scr[rs, :] = jnp.zeros((R, D_MODEL), BF16)

    slot_t = lax.broadcasted_iota(jnp.int32, (tm, ns), 1).astype(F32)
    PT = jnp.where(slot_t == posc_ref[...], 1.0, 0.0).astype(BF16)
    y = _dot(PT, ys_scr[...])
    z = ALPHA * x + (1.0 + gt_ref[...]) * y
    out_ref[...] = _layer_norm(z, g_ref[...], b_ref[...])


def _moe_ln(geo, x, mod, w_router_t, b_router, w_gu, w_down, layer, ln_g, ln_b):
    T, tm = geo.T, geo.tm
    nt = T // tm
    nb, ns = _moe_blocks(tm)
    posr, posc, gates, tbl = pl.pallas_call(
        _router_kernel,
        grid=(nt,),
        in_specs=[
            pl.BlockSpec((tm, D_MODEL), lambda i: (i, 0)),
            _mod_spec(geo, 4),
            _mod_spec(geo, 3),
            pl.BlockSpec((N_EXPERTS, D_MODEL), lambda i: (0, 0)),
            pl.BlockSpec((N_EXPERTS, 1), lambda i: (0, 0)),
        ],
        out_specs=[
            pl.BlockSpec((None, 1, tm), lambda i: (i, 0, 0)),
            pl.BlockSpec((tm, 1), lambda i: (i, 0)),
            pl.BlockSpec((N_EXPERTS, tm), lambda i: (0, i)),
            pl.BlockSpec((None, 1, LANES), lambda i: (i, 0, 0)),
        ],
        out_shape=[
            jax.ShapeDtypeStruct((nt, 1, tm), F32),
            jax.ShapeDtypeStruct((T, 1), F32),
            jax.ShapeDtypeStruct((N_EXPERTS, T), F32),
            jax.ShapeDtypeStruct((nt, 1, LANES), jnp.int32),
        ],
        compiler_params=_cparams(("parallel",)),
        name="moe_router",
    )(x, mod, mod, w_router_t, b_router.reshape(N_EXPERTS, 1))

    vec = pl.BlockSpec((1, D_MODEL), lambda i, t: (0, 0))
    resident = dict(pipeline_mode=pl.Buffered(1))
    grid_spec = pltpu.PrefetchScalarGridSpec(
        num_scalar_prefetch=1,
        grid=(nt,),
        in_specs=[
            pl.BlockSpec((tm, D_MODEL), lambda i, t: (i, 0)),
            _mod_spec(geo, 4),
            _mod_spec(geo, 3),
            _mod_spec(geo, 5),
            pl.BlockSpec((None, 1, tm), lambda i, t: (i, 0, 0)),
            pl.BlockSpec((tm, 1), lambda i, t: (i, 0)),
            pl.BlockSpec((N_EXPERTS, tm), lambda i, t: (0, i)),
            pl.BlockSpec((None, N_GROUPS, EXPERTS_PER_GROUP, D_MODEL, 2 * D_EXPERT),
                         lambda i, t: (layer, 0, 0, 0, 0), **resident),
            pl.BlockSpec((None, N_GROUPS, EXPERTS_PER_GROUP * D_EXPERT, D_MODEL),
                         lambda i, t: (layer, 0, 0, 0), **resident),
            vec, vec,
        ],
        out_specs=pl.BlockSpec((tm, D_MODEL), lambda i, t: (i, 0)),
        scratch_shapes=[
            pltpu.VMEM((ns, D_MODEL), BF16),
            pltpu.VMEM((ns, N_EXPERTS), F32),
            pltpu.VMEM((ns, D_MODEL), BF16),
        ],
    )
    return pl.pallas_call(
        functools.partial(_expert_kernel, nb=nb),
        grid_spec=grid_spec,
        out_shape=jax.ShapeDtypeStruct((T, D_MODEL), F32),
        compiler_params=_cparams(("parallel",)),
        name="moe_experts_ln",
    )(tbl.reshape(nt, LANES), x, mod, mod, mod, posr, posc, gates, w_gu, w_down,
      ln_g.reshape(1, D_MODEL), ln_b.reshape(1, D_MODEL))


def _trunk(geo, x, mods, s_ret, s_dn, s_conv, wts, ret_tl, ret_chunk, dn_tl, dn_chunk):
    (ln_g, ln_b, ret_w_in, ret_gn_g, ret_w_out, dn_w_main, dn_w_tail, dn_conv_w, dn_a_log, dn_dt_bias,
     dn_norm_g, dn_w_out, router_w, router_b, moe_w_gu, moe_w_down) = wts
    tables = _ret_tables(geo, ret_chunk)
    decode = s_dn is not None
    proj_tm = None if geo.per_token or geo.L % 1024 else 1024
    ret_stack, dn_stack, conv_tails = None, None, []
    for i in range(DEPTH):
        if geo.per_token:
            mod = jnp.repeat(mods[i], geo.L, axis=0)
        else:
            mod = mods[i].reshape(geo.B, 1, N_MOD * D_MODEL)
        j = i // 2
        if i % 2 == 0:
            p, _ = _inproj(geo, x, mod, ret_w_in[j], None, tn=1536, tm=proj_tm)
            o, ret_stack = _retention(geo, p, tables, ret_gn_g[j], s_ret, j, ret_stack, ret_tl, ret_chunk)
            w_out = ret_w_out[j]
        else:
            if decode:
                p, ab = _inproj(geo, x, mod, dn_w_main[j], dn_w_tail[j], tn=1536)
            else:
                p, ab = _inproj(geo, x, mod, dn_w_main[j], dn_w_tail[j], tn=2048,
                                conv_w=dn_conv_w[j], conv_cols=DN_CONV_CH)
                conv_tails.append(_conv_tail(geo, x, mod, dn_w_main[j], DN_CONV_CH, tn=2048))
            o, dn_stack = _deltanet(geo, p, ab, dn_conv_w[j], dn_a_log[j], dn_dt_bias[j], dn_norm_g[j],
                                    s_dn, s_conv, j, dn_stack, dn_tl, dn_chunk)
            w_out = dn_w_out[j]
        x = _outproj_ln(geo, o, w_out, x, mod, ln_g[i, 0], ln_b[i, 0])
        x = _moe_ln(geo, x, mod, router_w, router_b, moe_w_gu, moe_w_down, i, ln_g[i, 1], ln_b[i, 1])
    if decode:
        new_dn, new_conv = dn_stack
    else:
        new_dn = dn_stack[0]
        new_conv = jnp.stack(conv_tails)[:, :, SUBLANES - (DN_CONV_W - 1):]
    return x, ret_stack[0], new_dn, new_conv


def _run(x_prompt, x_sample, state_ret, state_dn, state_conv, c_prompt, c_sample, w_ada, b_ada, ln_g, ln_b,
         ret_w_in, ret_gn_g, ret_w_out, dn_w_in, dn_conv_w, dn_a_log, dn_dt_bias, dn_norm_g, dn_w_out,
         router_w, router_b, moe_w_gu, moe_w_down, *, past_len):
    Bp, Lp, _ = x_prompt.shape
    Bs, Ls, _ = x_sample.shape
    Ls_pad = -(-Ls // SUBLANES) * SUBLANES
    tm_p = min(512, Lp)
    geo_p = _Geo(Bp, Lp, Lp, 0, tm_p, False, BF16)
    Ts = Bs * Ls_pad
    geo_s = _Geo(Bs, Ls_pad, Ls, past_len, min(512, Ts), True, F32)

    dn_w_tail = jnp.pad(dn_w_in[:, :, DN_MAIN:], ((0, 0), (0, 0), (0, LANES - 2 * DN_V_HEADS))).astype(BF16)
    w_gu_grp = moe_w_gu.astype(BF16).reshape(DEPTH, N_GROUPS, EXPERTS_PER_GROUP, D_MODEL, 2 * D_EXPERT)
    w_down_grp = moe_w_down.astype(BF16).reshape(DEPTH, N_GROUPS, EXPERTS_PER_GROUP * D_EXPERT, D_MODEL)
    wts = (ln_g, ln_b, ret_w_in.astype(BF16), ret_gn_g, ret_w_out.astype(BF16),
           dn_w_in[:, :, :DN_MAIN].astype(BF16), dn_w_tail, dn_conv_w, dn_a_log, dn_dt_bias, dn_norm_g,
           dn_w_out.astype(BF16), router_w.T, router_b, w_gu_grp, w_down_grp)

    mods = _ada(jnp.concatenate([c_prompt, c_sample], axis=0), w_ada, b_ada)

    xp = x_prompt.reshape(Bp * Lp, D_MODEL)
    yp, ret_p, dn_p, conv_p = _trunk(geo_p, xp, mods[:, :Bp], None, None, None, wts,
                                     ret_tl=min(512, Lp), ret_chunk=min(RET_CHUNK, Lp),
                                     dn_tl=min(256, Lp), dn_chunk=DN_CHUNK)
    xs = jnp.pad(x_sample, ((0, 0), (0, Ls_pad - Ls), (0, 0))).reshape(Ts, D_MODEL)
    ys, ret_s, dn_s, conv_s = _trunk(geo_s, xs, mods[:, Bp:], state_ret, state_dn, state_conv, wts,
                                     ret_tl=Ls_pad, ret_chunk=max(Ls_pad, RET_DECODE_CHUNK),
                                     dn_tl=Ls_pad, dn_chunk=max(Ls_pad, DN_DECODE_CHUNK))
    y_prompt = yp.reshape(Bp, Lp, D_MODEL)
    y_sample = ys.reshape(Bs, Ls_pad, D_MODEL)[:, :Ls]
    return (y_prompt, y_sample, ret_p, ret_s, dn_p, dn_s, conv_p, conv_s)


def kernel(x_prompt, x_sample, state_ret, state_dn, state_conv, c_prompt, c_sample, w_ada, b_ada, ln_g, ln_b,
           ret_w_in, ret_gn_g, ret_w_out, dn_w_in, dn_conv_w, dn_a_log, dn_dt_bias, dn_norm_g, dn_w_out,
           router_w, router_b, moe_w_gu, moe_w_down):
    return _run(x_prompt, x_sample, state_ret, state_dn, state_conv, c_prompt, c_sample, w_ada, b_ada,
                ln_g, ln_b, ret_w_in, ret_gn_g, ret_w_out, dn_w_in, dn_conv_w, dn_a_log, dn_dt_bias,
                dn_norm_g, dn_w_out, router_w, router_b, moe_w_gu, moe_w_down, past_len=PAST_LEN)
```

```python
import functools
from typing import NamedTuple

import jax
import jax.numpy as jnp
from jax import lax
from jax.experimental import pallas as pl
from jax.experimental.pallas import tpu as pltpu

F32 = jnp.float32
BF16 = jnp.bfloat16

D_MODEL = 1024
DEPTH = 4
N_RET_LAYERS = (DEPTH + 1) // 2
N_DN_LAYERS = DEPTH // 2
PAST_LEN = 16384
RET_HEADS = 4
RET_DK = 256
RET_DV = 512
RET_QW = RET_HEADS * RET_DK
RET_VW = RET_HEADS * RET_DV
RET_CHUNK = 256
RET_DECODE_CHUNK = 16
DN_QK_HEADS = 8
DN_V_HEADS = 16
DN_DK = 128
DN_DV = 128
DN_QW = DN_QK_HEADS * DN_DK
DN_VW = DN_V_HEADS * DN_DV
DN_CONV_CH = 2 * DN_QW + DN_VW
DN_CONV_W = 4
DN_CHUNK = 64
DN_DECODE_CHUNK = 8
DN_LOCKSTEP = 2
DN_MAIN = DN_CONV_CH + DN_VW
N_EXPERTS = 16
N_GROUPS = 4
EXPERTS_PER_GROUP = N_EXPERTS // N_GROUPS
D_EXPERT = 256
N_MOD = 6
ALPHA = (2.0 * DEPTH) ** 0.25
LN_EPS = 1e-5

LANES = 128
SUBLANES = 8
MXU_DIM = 256
VMEM_LIMIT = 56 * 1024 * 1024


class _Geo(NamedTuple):
    B: int
    L: int
    l_true: int
    pos0: int
    tm: int
    per_token: bool
    act_dtype: object

    @property
    def T(self):
        return self.B * self.L


def _cparams(sem):
    return pltpu.CompilerParams(dimension_semantics=sem, vmem_limit_bytes=VMEM_LIMIT)


def _mod_spec(geo, chunk):
    if geo.per_token:
        return pl.BlockSpec((geo.tm, D_MODEL), lambda *g: (g[0], chunk))
    tpr = geo.L // geo.tm
    return pl.BlockSpec((None, 1, D_MODEL), lambda *g: (g[0] // tpr, 0, chunk))


def _silu(x):
    hx = 0.5 * x
    return hx + hx * jnp.tanh(hx)


def _dot(a, b):
    return jnp.dot(a, b, preferred_element_type=F32)


def _dot_nt(a, b):
    return lax.dot_general(a, b, (((1,), (1,)), ((), ())), preferred_element_type=F32)


def _dot_tn(a, b):
    return lax.dot_general(a, b, (((0,), (0,)), ((), ())), preferred_element_type=F32)


def _layer_norm(z, g, b):
    mu = jnp.mean(z, axis=-1, keepdims=True)
    zc = z - mu
    var = jnp.mean(zc * zc, axis=-1, keepdims=True)
    return zc * lax.rsqrt(var + LN_EPS) * g + b


def _ada_kernel(c_ref, w_ref, b_ref, o_ref):
    cs = _silu(c_ref[...]).astype(BF16)
    o_ref[...] = _dot(cs, w_ref[...].astype(BF16)) + b_ref[...]


def _ada(c_all, w_ada, b_ada):
    R = c_all.shape[0]
    tn = 1024
    n_out = N_MOD * D_MODEL
    return pl.pallas_call(
        _ada_kernel,
        grid=(DEPTH, n_out // tn),
        in_specs=[
            pl.BlockSpec((R, D_MODEL), lambda l, j: (0, 0)),
            pl.BlockSpec((None, D_MODEL, tn), lambda l, j: (l, 0, j)),
            pl.BlockSpec((None, 1, tn), lambda l, j: (l, 0, j)),
        ],
        out_specs=pl.BlockSpec((None, R, tn), lambda l, j: (l, 0, j)),
        out_shape=jax.ShapeDtypeStruct((DEPTH, R, n_out), F32),
        compiler_params=_cparams(("parallel", "parallel")),
        name="ada_mod",
    )(c_all, w_ada, b_ada.reshape(DEPTH, 1, n_out))


def _inproj_kernel(*refs, has_tail, n_conv, tiles_per_row):
    refs = list(refs)
    x_ref, sc_ref, sh_ref, w_ref = refs[:4]
    del refs[:4]
    wt_ref = refs.pop(0) if has_tail else None
    cw_ref = refs.pop(0) if n_conv else None
    o_ref = refs.pop(0)
    ot_ref = refs.pop(0) if has_tail else None
    h_scr = refs.pop(0)
    i = pl.program_id(0)
    j = pl.program_id(1)

    @pl.when(j == 0)
    def _():
        h = x_ref[...] * (1.0 + sc_ref[...]) + sh_ref[...]
        hb = h.astype(BF16)
        h_scr[...] = hb
        if has_tail:
            ot_ref[...] = _dot(hb, wt_ref[...])

    if not n_conv:
        o_ref[...] = _dot(h_scr[...], w_ref[...]).astype(o_ref.dtype)
        return
    carry_scr, p_scr = refs
    tm, tn = o_ref.shape
    W = DN_CONV_W
    P0 = DN_CONV_PAD
    cw = CONV_COL_CHUNK

    def conv_tile(jj):
        row_start = (i % tiles_per_row) == 0
        prev = jnp.where(row_start, 0.0, carry_scr[jj])
        tails = []
        chunks = list(range(0, tn, cw))
        p_scr[0] = _dot(h_scr[...], w_ref[:, 0:cw])
        for k, c0 in enumerate(chunks):
            cs = slice(c0, c0 + cw)
            if k + 1 < len(chunks):
                p_scr[(k + 1) % 2] = _dot(h_scr[...], w_ref[:, c0 + cw:c0 + 2 * cw])
            p = p_scr[k % 2]
            y = p * cw_ref[W - 1:W, cs]
            for t in range(W - 1):
                y = y + pltpu.roll(p, W - 1 - t, 0) * cw_ref[t:t + 1, cs]
            o_ref[:, cs] = _silu(y).astype(o_ref.dtype)
            nfix = 2 * SUBLANES
            head = jnp.concatenate([prev[:, cs], p[:nfix]], axis=0)
            y0 = head[P0:] * cw_ref[W - 1:W, cs]
            for t in range(W - 1):
                y0 = y0 + head[P0 - (W - 1 - t):P0 + nfix - (W - 1 - t)] * cw_ref[t:t + 1, cs]
            o_ref[0:nfix, cs] = _silu(y0).astype(o_ref.dtype)
            tails.append(p[tm - P0:])
        carry_scr[jj] = jnp.concatenate(tails, axis=1)

    for jj in range(n_conv):
        pl.when(j == jj)(functools.partial(conv_tile, jj))

    @pl.when(j >= n_conv)
    def _():
        o_ref[...] = _dot(h_scr[...], w_ref[...]).astype(o_ref.dtype)


def _inproj(geo, x, mod, w, w_tail, tn, conv_w=None, conv_cols=0, tm=None):
    if tm is not None:
        geo = geo._replace(tm=tm)
    T, tm = geo.T, geo.tm
    N = w.shape[1]
    has_tail = w_tail is not None
    n_conv = conv_cols // tn
    in_specs = [
        pl.BlockSpec((tm, D_MODEL), lambda i, j: (i, 0)),
        _mod_spec(geo, 1),
        _mod_spec(geo, 0),
        pl.BlockSpec((D_MODEL, tn), lambda i, j: (0, j)),
    ]
    args = [x, mod, mod, w]
    out_specs = [pl.BlockSpec((tm, tn), lambda i, j: (i, j))]
    out_shape = [jax.ShapeDtypeStruct((T, N), geo.act_dtype)]
    scratch = [pltpu.VMEM((tm, D_MODEL), BF16)]
    if has_tail:
        in_specs.append(pl.BlockSpec((D_MODEL, LANES), lambda i, j: (0, 0)))
        args.append(w_tail)
        out_specs.append(pl.BlockSpec((tm, LANES), lambda i, j: (i, 0)))
        out_shape.append(jax.ShapeDtypeStruct((T, LANES), F32))
    if n_conv:
        assert conv_cols == n_conv * tn and not geo.per_token
        in_specs.append(pl.BlockSpec((DN_CONV_W, tn), lambda i, j: (0, jnp.minimum(j, n_conv - 1))))
        args.append(conv_w)
        scratch += [pltpu.VMEM((n_conv, DN_CONV_PAD, tn), F32), pltpu.VMEM((2, tm, CONV_COL_CHUNK), F32)]
    res = pl.pallas_call(
        functools.partial(_inproj_kernel, has_tail=has_tail, n_conv=n_conv, tiles_per_row=geo.L // tm),
        grid=(T // tm, N // tn),
        in_specs=in_specs,
        out_specs=out_specs,
        out_shape=out_shape,
        scratch_shapes=scratch,
        compiler_params=_cparams(("arbitrary", "arbitrary")),
        name="in_proj",
    )(*args)
    return res if has_tail else (res[0], None)


def _conv_tail_kernel(x_ref, sc_ref, sh_ref, w_ref, o_ref):
    h = x_ref[...] * (1.0 + sc_ref[...]) + sh_ref[...]
    o_ref[...] = _dot(h.astype(BF16), w_ref[...])


def _conv_tail(geo, x, mod, w, n_cols, tn):
    B, L = geo.B, geo.L
    blocks_per_row = L // SUBLANES
    return pl.pallas_call(
        _conv_tail_kernel,
        grid=(n_cols // tn, B),
        in_specs=[
            pl.BlockSpec((SUBLANES, D_MODEL), lambda j, b: (b * blocks_per_row + blocks_per_row - 1, 0)),
            pl.BlockSpec((None, 1, D_MODEL), lambda j, b: (b, 0, 1)),
            pl.BlockSpec((None, 1, D_MODEL), lambda j, b: (b, 0, 0)),
            pl.BlockSpec((D_MODEL, tn), lambda j, b: (0, j)),
        ],
        out_specs=pl.BlockSpec((None, SUBLANES, tn), lambda j, b: (b, 0, j)),
        out_shape=jax.ShapeDtypeStruct((B, SUBLANES, n_cols), F32),
        compiler_params=_cparams(("parallel", "parallel")),
        name="conv_tail",
    )(x, mod, mod, w)


def _call_stacked(kern, prev, first_stacked_out, *, grid, in_specs, args, out_specs, out_shape,
                  scratch_shapes, name):
    n_in = len(args)
    aliases = {}
    body = kern
    if prev is not None:
        n_prev = len(prev)
        in_specs = list(in_specs) + [pl.BlockSpec(memory_space=pl.ANY)] * n_prev
        args = list(args) + list(prev)
        aliases = {n_in + k: first_stacked_out + k for k in range(n_prev)}

        def body(*refs):
            kern(*refs[:n_in], *refs[n_in + n_prev:])

    return pl.pallas_call(
        body, grid=grid, in_specs=in_specs, out_specs=out_specs, out_shape=out_shape,
        scratch_shapes=scratch_shapes, input_output_aliases=aliases,
        compiler_params=_cparams(("parallel", "arbitrary")), name=name,
    )(*args)


def _stack_fill(prev, layer_j, n_layers):
    if prev is None:
        return (layer_j, n_layers), n_layers, 0
    return None, None, layer_j


def _pad_rows(x, rows):
    if x.shape[0] == rows:
        return x
    return jnp.concatenate([x, jnp.zeros((rows - x.shape[0], x.shape[1]), x.dtype)], axis=0)


def _ret_kernel(*refs, c, rows, nchunk, has_state, fill):
    if has_state:
        (q_ref, k_ref, v_ref, g_ref, cos_ref, sin_ref, dintra_ref, qdec_ref, kdec_ref, sdec_ref, gn_ref,
         s0_ref, o_ref, s_out_ref, s_scr) = refs
    else:
        (q_ref, k_ref, v_ref, g_ref, cos_ref, sin_ref, dintra_ref, qdec_ref, kdec_ref, sdec_ref, gn_ref,
         o_ref, s_out_ref, s_scr) = refs
    i = pl.program_id(1)
    half = RET_DK // 2

    @pl.when(i == 0)
    def _():
        if has_state:
            s_scr[...] = s0_ref[...]
        else:
            s_scr[...] = jnp.zeros_like(s_scr)

    def rot(ref, r, h, cos, sin):
        x1 = ref[r, h * RET_DK:h * RET_DK + half].astype(F32)
        x2 = ref[r, h * RET_DK + half:(h + 1) * RET_DK].astype(F32)
        return jnp.concatenate([x1 * cos - x2 * sin, x2 * cos + x1 * sin], axis=1)

    for ci in range(nchunk):
        r = slice(ci * rows, (ci + 1) * rows)
        cos = cos_ref[r, :]
        sin = sin_ref[r, :]
        for h in range(RET_HEADS):
            vs = slice(h * RET_DV, (h + 1) * RET_DV)
            q = _pad_rows(rot(q_ref, r, h, cos, sin), c)
            k = _pad_rows(rot(k_ref, r, h, cos, sin), c) * (RET_DK ** -0.5)
            if rows == c:
                v = v_ref[r, vs].astype(BF16)
            else:
                v = _pad_rows(v_ref[r, vs].astype(F32), c).astype(BF16)
            s = s_scr[h]
            att = _dot_nt(q.astype(BF16), k.astype(BF16)) * dintra_ref[h]
            o = _dot(att.astype(BF16), v) + _dot((q * qdec_ref[h]).astype(BF16), s.astype(BF16))
            s_scr[h] = s * sdec_ref[h] + _dot_tn((k * kdec_ref[h]).astype(BF16), v)
            o = o[:rows]
            mu = jnp.mean(o, axis=-1, keepdims=True)
            oc = o - mu
            var = jnp.mean(oc * oc, axis=-1, keepdims=True)
            on = oc * lax.rsqrt(var + LN_EPS) * gn_ref[...]
            gate = g_ref[r, vs].astype(F32)
            o_ref[r, vs] = (on * _silu(gate)).astype(o_ref.dtype)

    @pl.when(i == pl.num_programs(1) - 1)
    def _():
        if fill is None:
            s_out_ref[...] = s_scr[...]
        else:
            for l in range(fill[1]):
                s_out_ref[l] = s_scr[...] if l == fill[0] else jnp.zeros(s_scr.shape, F32)


def _ret_tables(geo, chunk):
    c_true = min(chunk, geo.l_true)
    half = RET_DK // 2
    inv_freq = 10000.0 ** (-jnp.linspace(0.0, 1.0, half, dtype=F32))
    pos = (geo.pos0 + jnp.arange(geo.L)).astype(F32)
    ang = pos[:, None] * inv_freq[None, :]
    lg = jnp.log(1.0 - 2.0 ** (-5.0 - jnp.arange(RET_HEADS, dtype=F32)))
    idx = jnp.arange(c_true, dtype=F32)
    diff = idx[:, None] - idx[None, :]
    dintra = jnp.exp(jnp.where(diff[None] >= 0, diff[None] * lg[:, None, None], -jnp.inf))
    qdec = jnp.exp((idx + 1.0)[None, :] * lg[:, None])
    kdec = jnp.exp((c_true - 1.0 - idx)[None, :] * lg[:, None])
    sdec = jnp.exp(c_true * lg)
    pad = chunk - c_true
    dintra = jnp.pad(dintra, ((0, 0), (0, pad), (0, pad)))
    qdec = jnp.pad(qdec, ((0, 0), (0, pad)))[..., None]
    kdec = jnp.pad(kdec, ((0, 0), (0, pad)))[..., None]
    sdec = jnp.broadcast_to(sdec[:, None, None], (RET_HEADS, 1, RET_DV))
    return jnp.cos(ang), jnp.sin(ang), dintra, qdec, kdec, sdec


def _retention(geo, p, tables, gn_g, s_all, layer_j, prev, tl, chunk):
    B, L = geo.B, geo.L
    nblk = L // tl
    rows = min(tl, chunk)
    nchunk = tl // rows
    has_state = s_all is not None
    cos, sin, dintra, qdec, kdec, sdec = tables
    rb = lambda b, i: b * nblk + i
    full3 = lambda b, i: (0, 0, 0)
    in_specs = [
        pl.BlockSpec((tl, RET_QW), lambda b, i: (rb(b, i), 0)),
        pl.BlockSpec((tl, RET_QW), lambda b, i: (rb(b, i), 1)),
        pl.BlockSpec((tl, RET_VW), lambda b, i: (rb(b, i), 1)),
        pl.BlockSpec((tl, RET_VW), lambda b, i: (rb(b, i), 2)),
        pl.BlockSpec((tl, RET_DK // 2), lambda b, i: (i, 0)),
        pl.BlockSpec((tl, RET_DK // 2), lambda b, i: (i, 0)),
        pl.BlockSpec((RET_HEADS, chunk, chunk), full3),
        pl.BlockSpec((RET_HEADS, chunk, 1), full3),
        pl.BlockSpec((RET_HEADS, chunk, 1), full3),
        pl.BlockSpec((RET_HEADS, 1, RET_DV), full3),
        pl.BlockSpec((1, RET_DV), lambda b, i: (0, 0)),
    ]
    args = [p, p, p, p, cos, sin, dintra, qdec, kdec, sdec, gn_g.reshape(1, RET_DV)]
    if has_state:
        in_specs.append(pl.BlockSpec((None, None, RET_HEADS, RET_DK, RET_DV),
                                     lambda b, i: (layer_j, b, 0, 0, 0)))
        args.append(s_all)
    fill, lead, lidx = _stack_fill(prev, layer_j, N_RET_LAYERS)
    o, s = _call_stacked(
        functools.partial(_ret_kernel, c=chunk, rows=rows, nchunk=nchunk, has_state=has_state, fill=fill),
        prev, first_stacked_out=1,
        grid=(B, nblk),
        in_specs=in_specs,
        args=args,
        out_specs=[
            pl.BlockSpec((tl, RET_VW), lambda b, i: (rb(b, i), 0)),
            pl.BlockSpec((lead, None, RET_HEADS, RET_DK, RET_DV), lambda b, i: (lidx, b, 0, 0, 0)),
        ],
        out_shape=[
            jax.ShapeDtypeStruct((geo.T, RET_VW), geo.act_dtype),
            jax.ShapeDtypeStruct((N_RET_LAYERS, B, RET_HEADS, RET_DK, RET_DV), F32),
        ],
        scratch_shapes=[pltpu.VMEM((RET_HEADS, RET_DK, RET_DV), F32)],
        name="retention",
    )
    return o, (s,)


DN_CONV_PAD = SUBLANES
CONV_COL_CHUNK = 256


def _block_diag(x_cat, n, mask):
    return jnp.where(mask, jnp.concatenate([x_cat] * n, axis=0), jnp.zeros((), x_cat.dtype))


def _dn_kernel(*refs, c, rows, nchunk, l_last, has_state, fill):
    if has_state:
        (qkv_ref, z_ref, ab_ref, cw_ref, alog_ref, dtb_ref, ng_ref, s0_ref, buf_ref,
         o_ref, s_out_ref, buf_out_ref, s_scr, xp_scr, y_scr) = refs
    else:
        (qkv_ref, z_ref, ab_ref, alog_ref, dtb_ref, ng_ref, o_ref, s_out_ref, s_scr) = refs
    i = pl.program_id(1)
    hq = min(DN_V_HEADS, MXU_DIM // c)
    tl = rows * nchunk
    npair = DN_V_HEADS // 2
    W = DN_CONV_W
    P0 = DN_CONV_PAD

    @pl.when(i == 0)
    def _():
        if has_state:
            for pr in range(npair):
                s_scr[pr] = jnp.concatenate([s0_ref[2 * pr], s0_ref[2 * pr + 1]], axis=1)
            xp_scr[P0 - (W - 1):P0, :] = buf_ref[...]
        else:
            s_scr[...] = jnp.zeros_like(s_scr)

    if has_state:
        xp_scr[P0:P0 + tl, :] = qkv_ref[...].astype(F32)
        y = xp_scr[P0 - (W - 1):P0 - (W - 1) + tl, :] * cw_ref[0:1, :]
        for t in range(1, W):
            y = y + xp_scr[P0 - (W - 1) + t:P0 - (W - 1) + t + tl, :] * cw_ref[t:t + 1, :]
        y_scr[0:tl, :] = _silu(y)
        if rows < c:
            y_scr[tl:, :] = jnp.zeros((y_scr.shape[0] - tl, DN_CONV_CH), F32)

        @pl.when(i == pl.num_programs(1) - 1)
        def _():
            tail = xp_scr[P0 + l_last - (W - 1):P0 + l_last, :]
            if fill is None:
                buf_out_ref[...] = tail
            else:
                for l in range(fill[1]):
                    buf_out_ref[l] = tail if l == fill[0] else jnp.zeros(tail.shape, F32)

        xp_scr[P0 - (W - 1):P0, :] = xp_scr[P0 + tl - (W - 1):P0 + tl, :]

        def ycols(r, lo, hi):
            return y_scr[r, lo:hi]
    else:
        def ycols(r, lo, hi):
            return qkv_ref[r, lo:hi].astype(F32)

    ri = lax.broadcasted_iota(jnp.int32, (c, c), 0)
    ci_ = lax.broadcasted_iota(jnp.int32, (c, c), 1)
    incl = ri >= ci_
    strict = ri > ci_
    eye = ri == ci_
    tri_f = incl.astype(F32)
    eye_cat = jnp.concatenate([eye.astype(F32)] * hq, axis=1)
    bi = lax.broadcasted_iota(jnp.int32, (hq * c, hq * c), 0) // c
    bj = lax.broadcasted_iota(jnp.int32, (hq * c, hq * c), 1) // c
    bd_mask = bi == bj
    lane2 = lax.broadcasted_iota(jnp.int32, (DN_DK, 2 * DN_DV), 1)
    left = lane2 < DN_DV
    padded = rows < c or l_last < rows
    row_ok = lax.broadcasted_iota(jnp.int32, (c, 1), 0) < l_last if padded else None
    n_levels = c.bit_length() - 1

    ones_w = jnp.ones((DN_DK, DN_DK), BF16)

    def sumsq(x):
        if has_state:
            return jnp.sum(x * x, axis=-1, keepdims=True)
        return _dot((x * x).astype(BF16), ones_w)

    nquad = DN_V_HEADS // hq
    quads = [[hq * q + t for t in range(hq)] for q in range(nquad)]

    def phase1(ck):
        r0 = ck * c if rows == c else 0
        r = slice(r0, r0 + c)
        ab = _pad_rows(ab_ref[ck * rows:(ck + 1) * rows, :], c)
        sp = ab + dtb_ref[...]
        g_full = -jnp.exp(alog_ref[...]) * (jnp.maximum(sp, 0.0) + jnp.log1p(jnp.exp(-jnp.abs(sp))))
        beta_full = jax.nn.sigmoid(ab)
        if row_ok is not None:
            g_full = jnp.where(row_ok, g_full, 0.0)
            beta_full = jnp.where(row_ok, beta_full, 0.0)
        G_full = jnp.dot(tri_f, g_full, preferred_element_type=F32, precision=lax.Precision.HIGHEST)
        yield

        kn, qn, KK, QK = {}, {}, {}, {}
        for j in range(DN_QK_HEADS):
            qh = ycols(r, j * DN_DK, (j + 1) * DN_DK)
            kh = ycols(r, DN_QW + j * DN_DK, DN_QW + (j + 1) * DN_DK)
            qn[j] = qh * lax.rsqrt(sumsq(qh) + 1e-6) * (DN_DK ** -0.5)
            kn[j] = kh * lax.rsqrt(sumsq(kh) + 1e-6)
        yield
        for j in range(DN_QK_HEADS):
            kb16 = kn[j].astype(BF16)
            kq = _dot_nt(jnp.concatenate([kb16, qn[j].astype(BF16)], axis=0), kb16)
            KK[j], QK[j] = kq[:c], kq[c:]
        yield

        A, att, eG, Gc, rhs = {}, {}, {}, {}, {}
        for h in range(DN_V_HEADS):
            j = h // 2
            Gc[h] = jnp.broadcast_to(G_full[:, h:h + 1], (c, DN_DV))
            bc = jnp.broadcast_to(beta_full[:, DN_V_HEADS + h:DN_V_HEADS + h + 1], (c, DN_DV))
            Gcc = Gc[h][:, :c]
            Gr = jnp.sum(jnp.where(eye, Gcc, 0.0), axis=0, keepdims=True)
            dec = jnp.exp(jnp.where(incl, Gcc - Gr, -jnp.inf))
            A[h] = jnp.where(strict, bc[:, :c] * KK[j] * dec, 0.0)
            att[h] = QK[j] * dec
            eG[h] = jnp.exp(Gc[h])
            vh = ycols(r, 2 * DN_QW + h * DN_DV, 2 * DN_QW + (h + 1) * DN_DV)
            rhs[h] = jnp.concatenate([vh * bc, kn[j] * (bc * eG[h])], axis=1)
        yield

        X = [jnp.concatenate([A[h] for h in hs], axis=1) for hs in quads]
        Pm = [eye_cat - x for x in X]
        X = [_dot(x.astype(BF16), _block_diag(x, hq, bd_mask).astype(BF16)) for x in X]
        yield
        for lvl in range(1, n_levels):
            Xw = [_block_diag(x, hq, bd_mask).astype(BF16) for x in X]
            if lvl < n_levels - 1:
                both = [_dot(jnp.concatenate([X[q], Pm[q]], axis=0).astype(BF16), Xw[q]) for q in range(nquad)]
                X = [b[:c] for b in both]
                Pm = [Pm[q] + both[q][c:] for q in range(nquad)]
            else:
                Pm = [Pm[q] + _dot(Pm[q].astype(BF16), Xw[q]) for q in range(nquad)]
            yield
        sol = [_dot(_block_diag(Pm[q], hq, bd_mask).astype(BF16),
                    jnp.concatenate([rhs[h] for h in quads[q]], axis=0).astype(BF16))
               for q in range(nquad)]
        att_bd = [_block_diag(jnp.concatenate([att[h] for h in quads[q]], axis=1), hq, bd_mask).astype(BF16)
                  for q in range(nquad)]
        return kn, qn, Gc, eG, sol, att_bd

    def phase2(ck, pre):
        kn, qn, Gc, eG, sol, att_bd = pre
        sol_h = {h: sol[h // hq][c * (h % hq):c * (h % hq + 1)] for h in range(DN_V_HEADS)}
        u, oS = {}, {}
        for pr in range(npair):
            h0, h1 = 2 * pr, 2 * pr + 1
            S = s_scr[pr]
            Sbd = jnp.concatenate([jnp.where(left, S, 0.0), jnp.where(left, 0.0, S)], axis=0).astype(BF16)
            w_cat = jnp.concatenate([sol_h[h0][:, DN_DV:], sol_h[h1][:, DN_DV:]], axis=1)
            qe_cat = jnp.concatenate([qn[pr] * eG[h0], qn[pr] * eG[h1]], axis=1)
            R = _dot(jnp.concatenate([w_cat, qe_cat], axis=0).astype(BF16), Sbd)
            u[h0] = sol_h[h0][:, :DN_DV] - R[:c, :DN_DV]
            u[h1] = sol_h[h1][:, :DN_DV] - R[:c, DN_DV:]
            oS[h0] = R[c:, :DN_DV]
            oS[h1] = R[c:, DN_DV:]

        o_intra = [_dot(att_bd[q], jnp.concatenate([u[h] for h in quads[q]], axis=0).astype(BF16))
                   for q in range(nquad)]

        for pr in range(npair):
            h0, h1 = 2 * pr, 2 * pr + 1
            gl0 = Gc[h0][c - 1:c, :]
            gl1 = Gc[h1][c - 1:c, :]
            du = jnp.concatenate([u[h0] * jnp.exp(gl0 - Gc[h0]), u[h1] * jnp.exp(gl1 - Gc[h1])], axis=1)
            sdec = jnp.concatenate([jnp.exp(gl0), jnp.exp(gl1)], axis=1)
            s_scr[pr] = s_scr[pr] * sdec + _dot_tn(kn[pr].astype(BF16), du.astype(BF16))

        zr = slice(ck * rows, (ck + 1) * rows)
        for h in range(DN_V_HEADS):
            t = h % hq
            o = (oS[h] + o_intra[h // hq][c * t:c * (t + 1)])[:rows]
            o = o * lax.rsqrt(sumsq(o) * (1.0 / DN_DV) + 1e-6) * ng_ref[...]
            hs = slice(h * DN_DV, (h + 1) * DN_DV)
            z = z_ref[zr, hs].astype(F32)
            o_ref[zr, hs] = (o * _silu(z)).astype(o_ref.dtype)

    for g0 in range(0, nchunk, DN_LOCKSTEP):
        group = list(range(g0, min(g0 + DN_LOCKSTEP, nchunk)))
        gens = {ck: phase1(ck) for ck in group}
        pre = {}
        while gens:
            for ck in list(gens):
                try:
                    next(gens[ck])
                except StopIteration as done:
                    pre[ck] = done.value
                    del gens[ck]
        for ck in group:
            phase2(ck, pre[ck])

    @pl.when(i == pl.num_programs(1) - 1)
    def _():
        for pr in range(npair):
            for l in range(1 if fill is None else fill[1]):
                mine = fill is None or l == fill[0]
                dst = s_out_ref if fill is None else s_out_ref.at[l]
                dst[2 * pr] = s_scr[pr][:, :DN_DV] if mine else jnp.zeros((DN_DK, DN_DV), F32)
                dst[2 * pr + 1] = s_scr[pr][:, DN_DV:] if mine else jnp.zeros((DN_DK, DN_DV), F32)


def _deltanet(geo, p, ab, conv_w, a_log, dt_bias, norm_g, s_all, buf_all, layer_j, prev, tl, chunk):
    B, L = geo.B, geo.L
    nblk = L // tl
    rows = min(tl, chunk)
    nchunk = tl // rows
    l_last = geo.l_true - (nblk - 1) * tl
    assert l_last == tl or (nblk == 1 and nchunk == 1)
    has_state = s_all is not None
    rb = lambda b, i: b * nblk + i
    pad16 = lambda v: jnp.pad(v.reshape(1, DN_V_HEADS), ((0, 0), (0, LANES - DN_V_HEADS)))
    row1 = lambda n: pl.BlockSpec((1, n), lambda b, i: (0, 0))
    in_specs = [
        pl.BlockSpec((tl, DN_CONV_CH), lambda b, i: (rb(b, i), 0)),
        pl.BlockSpec((tl, DN_VW), lambda b, i: (rb(b, i), DN_CONV_CH // DN_VW)),
        pl.BlockSpec((tl, LANES), lambda b, i: (rb(b, i), 0)),
    ]
    args = [p, p, ab]
    if has_state:
        in_specs.append(pl.BlockSpec((DN_CONV_W, DN_CONV_CH), lambda b, i: (0, 0)))
        args.append(conv_w)
    in_specs += [row1(LANES), row1(LANES), row1(DN_DV)]
    args += [pad16(a_log), pad16(dt_bias), norm_g.reshape(1, DN_DV)]
    fill, lead, lidx = _stack_fill(prev, layer_j, N_DN_LAYERS)
    out_specs = [
        pl.BlockSpec((tl, DN_VW), lambda b, i: (rb(b, i), 0)),
        pl.BlockSpec((lead, None, DN_V_HEADS, DN_DK, DN_DV), lambda b, i: (lidx, b, 0, 0, 0)),
    ]
    out_shape = [
        jax.ShapeDtypeStruct((geo.T, DN_VW), geo.act_dtype),
        jax.ShapeDtypeStruct((N_DN_LAYERS, B, DN_V_HEADS, DN_DK, DN_DV), F32),
    ]
    scratch = [pltpu.VMEM((DN_V_HEADS // 2, DN_DK, 2 * DN_DV), F32)]
    if has_state:
        in_specs += [
            pl.BlockSpec((None, None, DN_V_HEADS, DN_DK, DN_DV), lambda b, i: (layer_j, b, 0, 0, 0)),
            pl.BlockSpec((None, None, DN_CONV_W - 1, DN_CONV_CH), lambda b, i: (layer_j, b, 0, 0)),
        ]
        args += [s_all, buf_all]
        out_specs.append(pl.BlockSpec((lead, None, DN_CONV_W - 1, DN_CONV_CH), lambda b, i: (lidx, b, 0, 0)))
        out_shape.append(jax.ShapeDtypeStruct((N_DN_LAYERS, B, DN_CONV_W - 1, DN_CONV_CH), F32))
        scratch += [pltpu.VMEM((DN_CONV_PAD + tl, DN_CONV_CH), F32),
                    pltpu.VMEM((max(tl, chunk), DN_CONV_CH), F32)]
    kern = functools.partial(_dn_kernel, c=chunk, rows=rows, nchunk=nchunk, l_last=l_last, has_state=has_state,
                             fill=fill)
    res = _call_stacked(kern, prev, first_stacked_out=1, grid=(B, nblk), in_specs=in_specs, args=args,
                        out_specs=out_specs, out_shape=out_shape, scratch_shapes=scratch, name="deltanet")
    return res[0], tuple(res[1:])


def _outproj_kernel(o_ref, w_ref, x_ref, gate_ref, g_ref, b_ref, out_ref):
    y = _dot(o_ref[...].astype(BF16), w_ref[...])
    z = ALPHA * x_ref[...] + (1.0 + gate_ref[...]) * y
    out_ref[...] = _layer_norm(z, g_ref[...], b_ref[...])


def _outproj_ln(geo, o, w_out, x, mod, ln_g, ln_b):
    T, tm = geo.T, geo.tm
    K = w_out.shape[0]
    vec = pl.BlockSpec((1, D_MODEL), lambda i: (0, 0))
    return pl.pallas_call(
        _outproj_kernel,
        grid=(T // tm,),
        in_specs=[
            pl.BlockSpec((tm, K), lambda i: (i, 0)),
            pl.BlockSpec((K, D_MODEL), lambda i: (0, 0)),
            pl.BlockSpec((tm, D_MODEL), lambda i: (i, 0)),
            _mod_spec(geo, 2),
            vec, vec,
        ],
        out_specs=pl.BlockSpec((tm, D_MODEL), lambda i: (i, 0)),
        out_shape=jax.ShapeDtypeStruct((T, D_MODEL), F32),
        compiler_params=_cparams(("parallel",)),
        name="out_proj_ln",
    )(o, w_out, x, mod, ln_g.reshape(1, D_MODEL), ln_b.reshape(1, D_MODEL))


MOE_ROWS = 144


def _moe_blocks(tm):
    nb = (tm - N_GROUPS) // MOE_ROWS + N_GROUPS
    ns = -(-(nb * MOE_ROWS) // LANES) * LANES
    return nb, ns


def _first_max(v, row, valid):
    m = jnp.max(jnp.where(valid, v, -jnp.inf), axis=0, keepdims=True)
    idx = jnp.min(jnp.where(valid & (v == m), row, float(N_EXPERTS)), axis=0, keepdims=True)
    return m, idx


def _route(h, wrt, br):
    logits = lax.dot_general(wrt, h, (((1,), (1,)), ((), ())), preferred_element_type=F32,
                             precision=lax.Precision.HIGHEST)
    mx = jnp.max(logits, axis=0, keepdims=True)
    ex = jnp.exp(logits - mx)
    probs = ex / jnp.sum(ex, axis=0, keepdims=True)
    sel = probs + br
    row_i = lax.broadcasted_iota(jnp.int32, sel.shape, 0)
    grp = (row_i // EXPERTS_PER_GROUP).astype(F32)
    row = row_i.astype(F32)
    best = None
    gsel = None
    for g in range(N_GROUPS):
        in_g = grp == g
        m1, i1 = _first_max(sel, row, in_g)
        m2, _ = _first_max(sel, row, in_g & (row != i1))
        score = m1 + m2
        if g == 0:
            best, gsel = score, jnp.zeros_like(i1)
        else:
            better = score > best
            gsel = jnp.where(better, float(g), gsel)
            best = jnp.where(better, score, best)
    in_grp = grp == gsel
    _, i1 = _first_max(sel, row, in_grp)
    _, i2 = _first_max(sel, row, in_grp & (row != i1))
    w1 = jnp.sum(jnp.where(row == i1, probs, 0.0), axis=0, keepdims=True)
    w2 = jnp.sum(jnp.where(row == i2, probs, 0.0), axis=0, keepdims=True)
    wsum = w1 + w2
    gates = jnp.where(row == i1, w1 / wsum, 0.0) + jnp.where(row == i2, w2 / wsum, 0.0)
    return gsel, gates


def _router_kernel(x_ref, sc_ref, sh_ref, wrt_ref, br_ref, posr_ref, gates_ref, tbl_ref):
    tm = x_ref.shape[0]
    h = x_ref[...] * (1.0 + sc_ref[...]) + sh_ref[...]
    gsel, gates = _route(h, wrt_ref[...], br_ref[...])
    gates_ref[...] = gates
    grow = lax.broadcasted_iota(jnp.int32, (SUBLANES, tm), 0).astype(F32)
    og = grow == gsel
    ti = lax.broadcasted_iota(jnp.int32, (tm, tm), 0)
    tj = lax.broadcasted_iota(jnp.int32, (tm, tm), 1)
    before = jnp.where(ti < tj, 1.0, 0.0).astype(BF16)
    ogf = jnp.where(og, 1.0, 0.0)
    rank = _dot(ogf.astype(BF16), before)
    cnt = jnp.sum(ogf, axis=1, keepdims=True)
    nblk = jnp.zeros_like(cnt)
    for kb in range(-(-tm // MOE_ROWS)):
        nblk = nblk + jnp.where(cnt > float(kb * MOE_ROWS), 1.0, 0.0)
    seg = nblk * float(MOE_ROWS)
    block_start = lax.broadcasted_iota(jnp.int32, (1, LANES), 1).astype(F32) * float(MOE_ROWS)
    off = jnp.zeros((1, 1), F32)
    pos = jnp.zeros((1, tm), F32)
    tbl = jnp.zeros((1, LANES), F32)
    for g in range(N_GROUPS):
        pos = pos + jnp.where(gsel == float(g), off + rank[g:g + 1, :], 0.0)
        off = off + seg[g:g + 1, :]
        tbl = tbl + jnp.where(block_start >= off, 1.0, 0.0)
    posr_ref[...] = pos
    tbl_ref[...] = tbl.astype(jnp.int32)


def _expert_kernel(tbl_ref, x_ref, sc_ref, sh_ref, gt_ref, posr_ref, gates_ref, wgu_ref, wd_ref,
                   g_ref, b_ref, out_ref, hs_scr, gs_scr, ys_scr, p_scr, *, nb):
    i = pl.program_id(0)
    tm = x_ref.shape[0]
    ns = hs_scr.shape[0]
    R = MOE_ROWS
    x = x_ref[...]
    hb = (x * (1.0 + sc_ref[...]) + sh_ref[...]).astype(BF16)
    slot = lax.broadcasted_iota(jnp.int32, (ns, tm), 0).astype(F32)
    P = jnp.where(slot == posr_ref[...], 1.0, 0.0).astype(BF16)
    p_scr[...] = P
    hs_scr[...] = _dot(P, hb).astype(BF16)
    gates = gates_ref[...]
    g_hi = gates.astype(BF16)
    g_lo = (gates - g_hi.astype(F32)).astype(BF16)
    gs_scr[...] = _dot_nt(P, g_hi) + _dot_nt(P, g_lo)
    if nb * R < ns:
        ys_scr[nb * R:, :] = jnp.zeros((ns - nb * R, D_MODEL), BF16)

    for b in range(nb):
        rs = slice(b * R, (b + 1) * R)
        g = tbl_ref[i, b]

        @pl.when(g < N_GROUPS)
        def _():
            gsb = gs_scr[rs, :]
            lane = lax.broadcasted_iota(jnp.int32, gsb.shape, 1)
            hsb = hs_scr[rs, :]
            acts = []
            for e in range(EXPERTS_PER_GROUP):
                gu = _dot(hsb, wgu_ref[g, e])
                gate = jnp.sum(jnp.where(lane == g * EXPERTS_PER_GROUP + e, gsb, 0.0), axis=1, keepdims=True)
                act = _silu(gu[:, :D_EXPERT]) * gu[:, D_EXPERT:] * gate
                acts.append(act.astype(BF16))
            ys_scr[rs, :] = _dot(jnp.concatenate(acts, axis=1), wd_ref[g]).astype(BF16)

        @pl.when(g >= N_GROUPS)
        def _():
            ys_scr[rs, :] = jnp.zeros((R, D_MODEL), BF16)

    y = _dot_tn(p_scr[...], ys_scr[...])
    z = ALPHA * x + (1.0 + gt_ref[...]) * y
    out_ref[...] = _layer_norm(z, g_ref[...], b_ref[...])


def _moe_ln(geo, x, mod, w_router_t, b_router, w_gu, w_down, layer, ln_g, ln_b):
    T, tm = geo.T, geo.tm
    nt = T // tm
    nb, ns = _moe_blocks(tm)
    posr, gates, tbl = pl.pallas_call(
        _router_kernel,
        grid=(nt,),
        in_specs=[
            pl.BlockSpec((tm, D_MODEL), lambda i: (i, 0)),
            _mod_spec(geo, 4),
            _mod_spec(geo, 3),
            pl.BlockSpec((N_EXPERTS, D_MODEL), lambda i: (0, 0)),
            pl.BlockSpec((N_EXPERTS, 1), lambda i: (0, 0)),
        ],
        out_specs=[
            pl.BlockSpec((None, 1, tm), lambda i: (i, 0, 0)),
            pl.BlockSpec((N_EXPERTS, tm), lambda i: (0, i)),
            pl.BlockSpec((None, 1, LANES), lambda i: (i, 0, 0)),
        ],
        out_shape=[
            jax.ShapeDtypeStruct((nt, 1, tm), F32),
            jax.ShapeDtypeStruct((N_EXPERTS, T), F32),
            jax.ShapeDtypeStruct((nt, 1, LANES), jnp.int32),
        ],
        compiler_params=_cparams(("parallel",)),
        name="moe_router",
    )(x, mod, mod, w_router_t, b_router.reshape(N_EXPERTS, 1))

    vec = pl.BlockSpec((1, D_MODEL), lambda i, t: (0, 0))
    resident = dict(pipeline_mode=pl.Buffered(1))
    grid_spec = pltpu.PrefetchScalarGridSpec(
        num_scalar_prefetch=1,
        grid=(nt,),
        in_specs=[
            pl.BlockSpec((tm, D_MODEL), lambda i, t: (i, 0)),
            _mod_spec(geo, 4),
            _mod_spec(geo, 3),
            _mod_spec(geo, 5),
            pl.BlockSpec((None, 1, tm), lambda i, t: (i, 0, 0)),
            pl.BlockSpec((N_EXPERTS, tm), lambda i, t: (0, i)),
            pl.BlockSpec((None, N_GROUPS, EXPERTS_PER_GROUP, D_MODEL, 2 * D_EXPERT),
                         lambda i, t: (layer, 0, 0, 0, 0), **resident),
            pl.BlockSpec((None, N_GROUPS, EXPERTS_PER_GROUP * D_EXPERT, D_MODEL),
                         lambda i, t: (layer, 0, 0, 0), **resident),
            vec, vec,
        ],
        out_specs=pl.BlockSpec((tm, D_MODEL), lambda i, t: (i, 0)),
        scratch_shapes=[
            pltpu.VMEM((ns, D_MODEL), BF16),
            pltpu.VMEM((ns, N_EXPERTS), F32),
            pltpu.VMEM((ns, D_MODEL), BF16),
            pltpu.VMEM((ns, tm), BF16),
        ],
    )
    return pl.pallas_call(
        functools.partial(_expert_kernel, nb=nb),
        grid_spec=grid_spec,
        out_shape=jax.ShapeDtypeStruct((T, D_MODEL), F32),
        compiler_params=_cparams(("parallel",)),
        name="moe_experts_ln",
    )(tbl.reshape(nt, LANES), x, mod, mod, mod, posr, gates, w_gu, w_down,
      ln_g.reshape(1, D_MODEL), ln_b.reshape(1, D_MODEL))


def _trunk(geo, x, mods, s_ret, s_dn, s_conv, wts, ret_tl, ret_chunk, dn_tl, dn_chunk):
    (ln_g, ln_b, ret_w_in, ret_gn_g, ret_w_out, dn_w_main, dn_w_tail, dn_conv_w, dn_a_log, dn_dt_bias,
     dn_norm_g, dn_w_out, router_w, router_b, moe_w_gu, moe_w_down) = wts
    tables = _ret_tables(geo, ret_chunk)
    decode = s_dn is not None
    proj_tm = None if geo.per_token or geo.L % 1024 else 1024
    ret_stack, dn_stack, conv_tails = None, None, []
    for i in range(DEPTH):
        if geo.per_token:
            mod = jnp.repeat(mods[i], geo.L, axis=0)
        else:
            mod = mods[i].reshape(geo.B, 1, N_MOD * D_MODEL)
        j = i // 2
        if i % 2 == 0:
            p, _ = _inproj(geo, x, mod, ret_w_in[j], None, tn=1536, tm=proj_tm)
            o, ret_stack = _retention(geo, p, tables, ret_gn_g[j], s_ret, j, ret_stack, ret_tl, ret_chunk)
            w_out = ret_w_out[j]
        else:
            if decode:
                p, ab = _inproj(geo, x, mod, dn_w_main[j], dn_w_tail[j], tn=1536)
            else:
                p, ab = _inproj(geo, x, mod, dn_w_main[j], dn_w_tail[j], tn=2048,
                                conv_w=dn_conv_w[j], conv_cols=DN_CONV_CH)
                conv_tails.append(_conv_tail(geo, x, mod, dn_w_main[j], DN_CONV_CH, tn=2048))
            o, dn_stack = _deltanet(geo, p, ab, dn_conv_w[j], dn_a_log[j], dn_dt_bias[j], dn_norm_g[j],
                                    s_dn, s_conv, j, dn_stack, dn_tl, dn_chunk)
            w_out = dn_w_out[j]
        x = _outproj_ln(geo, o, w_out, x, mod, ln_g[i, 0], ln_b[i, 0])
        x = _moe_ln(geo, x, mod, router_w, router_b, moe_w_gu, moe_w_down, i, ln_g[i, 1], ln_b[i, 1])
    if decode:
        new_dn, new_conv = dn_stack
    else:
        new_dn = dn_stack[0]
        new_conv = jnp.stack(conv_tails)[:, :, SUBLANES - (DN_CONV_W - 1):]
    return x, ret_stack[0], new_dn, new_conv


def _run(x_prompt, x_sample, state_ret, state_dn, state_conv, c_prompt, c_sample, w_ada, b_ada, ln_g, ln_b,
         ret_w_in, ret_gn_g, ret_w_out, dn_w_in, dn_conv_w, dn_a_log, dn_dt_bias, dn_norm_g, dn_w_out,
         router_w, router_b, moe_w_gu, moe_w_down, *, past_len):
    Bp, Lp, _ = x_prompt.shape
    Bs, Ls, _ = x_sample.shape
    Ls_pad = -(-Ls // SUBLANES) * SUBLANES
    tm_p = min(512, Lp)
    geo_p = _Geo(Bp, Lp, Lp, 0, tm_p, False, BF16)
    Ts = Bs * Ls_pad
    geo_s = _Geo(Bs, Ls_pad, Ls, past_len, min(512, Ts), True, F32)

    dn_w_tail = jnp.pad(dn_w_in[:, :, DN_MAIN:], ((0, 0), (0, 0), (0, LANES - 2 * DN_V_HEADS))).astype(BF16)
    w_gu_grp = moe_w_gu.astype(BF16).reshape(DEPTH, N_GROUPS, EXPERTS_PER_GROUP, D_MODEL, 2 * D_EXPERT)
    w_down_grp = moe_w_down.astype(BF16).reshape(DEPTH, N_GROUPS, EXPERTS_PER_GROUP * D_EXPERT, D_MODEL)
    wts = (ln_g, ln_b, ret_w_in.astype(BF16), ret_gn_g, ret_w_out.astype(BF16),
           dn_w_in[:, :, :DN_MAIN].astype(BF16), dn_w_tail, dn_conv_w, dn_a_log, dn_dt_bias, dn_norm_g,
           dn_w_out.astype(BF16), router_w.T, router_b, w_gu_grp, w_down_grp)

    mods = _ada(jnp.concatenate([c_prompt, c_sample], axis=0), w_ada, b_ada)

    xp = x_prompt.reshape(Bp * Lp, D_MODEL)
    yp, ret_p, dn_p, conv_p = _trunk(geo_p, xp, mods[:, :Bp], None, None, None, wts,
                                     ret_tl=min(512, Lp), ret_chunk=min(RET_CHUNK, Lp),
                                     dn_tl=min(256, Lp), dn_chunk=DN_CHUNK)
    xs = jnp.pad(x_sample, ((0, 0), (0, Ls_pad - Ls), (0, 0))).reshape(Ts, D_MODEL)
    ys, ret_s, dn_s, conv_s = _trunk(geo_s, xs, mods[:, Bp:], state_ret, state_dn, state_conv, wts,
                                     ret_tl=Ls_pad, ret_chunk=max(Ls_pad, RET_DECODE_CHUNK),
                                     dn_tl=Ls_pad, dn_chunk=max(Ls_pad, DN_DECODE_CHUNK))
    y_prompt = yp.reshape(Bp, Lp, D_MODEL)
    y_sample = ys.reshape(Bs, Ls_pad, D_MODEL)[:, :Ls]
    return (y_prompt, y_sample, ret_p, ret_s, dn_p, dn_s, conv_p, conv_s)


def kernel(x_prompt, x_sample, state_ret, state_dn, state_conv, c_prompt, c_sample, w_ada, b_ada, ln_g, ln_b,
           ret_w_in, ret_gn_g, ret_w_out, dn_w_in, dn_conv_w, dn_a_log, dn_dt_bias, dn_norm_g, dn_w_out,
           router_w, router_b, moe_w_gu, moe_w_down):
    return _run(x_prompt, x_sample, state_ret, state_dn, state_conv, c_prompt, c_sample, w_ada, b_ada,
                ln_g, ln_b, ret_w_in, ret_gn_g, ret_w_out, dn_w_in, dn_conv_w, dn_a_log, dn_dt_bias,
                dn_norm_g, dn_w_out, router_w, router_b, moe_w_gu, moe_w_down, past_len=PAST_LEN)
```

```python
import functools
from typing import NamedTuple

import jax
import jax.numpy as jnp
from jax import lax
from jax.experimental import pallas as pl
from jax.experimental.pallas import tpu as pltpu

F32 = jnp.float32
BF16 = jnp.bfloat16

D_MODEL = 1024
DEPTH = 4
N_RET_LAYERS = (DEPTH + 1) // 2
N_DN_LAYERS = DEPTH // 2
PAST_LEN = 16384
RET_HEADS = 4
RET_DK = 256
RET_DV = 512
RET_QW = RET_HEADS * RET_DK
RET_VW = RET_HEADS * RET_DV
RET_CHUNK = 256
RET_DECODE_CHUNK = 16
DN_QK_HEADS = 8
DN_V_HEADS = 16
DN_DK = 128
DN_DV = 128
DN_QW = DN_QK_HEADS * DN_DK
DN_VW = DN_V_HEADS * DN_DV
DN_CONV_CH = 2 * DN_QW + DN_VW
DN_CONV_W = 4
DN_CHUNK = 64
DN_DECODE_CHUNK = 8
DN_LOCKSTEP = 2
DN_MAIN = DN_CONV_CH + DN_VW
N_EXPERTS = 16
N_GROUPS = 4
EXPERTS_PER_GROUP = N_EXPERTS // N_GROUPS
D_EXPERT = 256
N_MOD = 6
ALPHA = (2.0 * DEPTH) ** 0.25
LN_EPS = 1e-5

LANES = 128
SUBLANES = 8
MXU_DIM = 256
VMEM_LIMIT = 56 * 1024 * 1024


class _Geo(NamedTuple):
    B: int
    L: int
    l_true: int
    pos0: int
    tm: int
    per_token: bool
    act_dtype: object

    @property
    def T(self):
        return self.B * self.L


def _cparams(sem):
    return pltpu.CompilerParams(dimension_semantics=sem, vmem_limit_bytes=VMEM_LIMIT)


def _mod_spec(geo, chunk):
    if geo.per_token:
        return pl.BlockSpec((geo.tm, D_MODEL), lambda *g: (g[0], chunk))
    tpr = geo.L // geo.tm
    return pl.BlockSpec((None, 1, D_MODEL), lambda *g: (g[0] // tpr, 0, chunk))


def _silu(x):
    hx = 0.5 * x
    return hx + hx * jnp.tanh(hx)


def _dot(a, b):
    return jnp.dot(a, b, preferred_element_type=F32)


def _dot_nt(a, b):
    return lax.dot_general(a, b, (((1,), (1,)), ((), ())), preferred_element_type=F32)


def _dot_tn(a, b):
    return lax.dot_general(a, b, (((0,), (0,)), ((), ())), preferred_element_type=F32)


def _layer_norm(z, g, b):
    mu = jnp.mean(z, axis=-1, keepdims=True)
    zc = z - mu
    var = jnp.mean(zc * zc, axis=-1, keepdims=True)
    return zc * lax.rsqrt(var + LN_EPS) * g + b


def _ada_kernel(c_ref, w_ref, b_ref, o_ref):
    cs = _silu(c_ref[...]).astype(BF16)
    o_ref[...] = _dot(cs, w_ref[...].astype(BF16)) + b_ref[...]


def _ada(c_all, w_ada, b_ada):
    R = c_all.shape[0]
    tn = 1024
    n_out = N_MOD * D_MODEL
    return pl.pallas_call(
        _ada_kernel,
        grid=(DEPTH, n_out // tn),
        in_specs=[
            pl.BlockSpec((R, D_MODEL), lambda l, j: (0, 0)),
            pl.BlockSpec((None, D_MODEL, tn), lambda l, j: (l, 0, j)),
            pl.BlockSpec((None, 1, tn), lambda l, j: (l, 0, j)),
        ],
        out_specs=pl.BlockSpec((None, R, tn), lambda l, j: (l, 0, j)),
        out_shape=jax.ShapeDtypeStruct((DEPTH, R, n_out), F32),
        compiler_params=_cparams(("parallel", "parallel")),
        name="ada_mod",
    )(c_all, w_ada, b_ada.reshape(DEPTH, 1, n_out))


def _inproj_kernel(*refs, has_tail, n_conv, tiles_per_row):
    refs = list(refs)
    x_ref, sc_ref, sh_ref, w_ref = refs[:4]
    del refs[:4]
    wt_ref = refs.pop(0) if has_tail else None
    cw_ref = refs.pop(0) if n_conv else None
    o_ref = refs.pop(0)
    ot_ref = refs.pop(0) if has_tail else None
    h_scr = refs.pop(0)
    i = pl.program_id(0)
    j = pl.program_id(1)

    @pl.when(j == 0)
    def _():
        h = x_ref[...] * (1.0 + sc_ref[...]) + sh_ref[...]
        hb = h.astype(BF16)
        h_scr[...] = hb
        if has_tail:
            ot_ref[...] = _dot(hb, wt_ref[...])

    if not n_conv:
        o_ref[...] = _dot(h_scr[...], w_ref[...]).astype(o_ref.dtype)
        return
    carry_scr, p_scr = refs
    tm, tn = o_ref.shape
    W = DN_CONV_W
    P0 = DN_CONV_PAD
    cw = CONV_COL_CHUNK

    def conv_tile(jj):
        row_start = (i % tiles_per_row) == 0
        prev = jnp.where(row_start, 0.0, carry_scr[jj])
        tails = []
        chunks = list(range(0, tn, cw))
        p_scr[0] = _dot(h_scr[...], w_ref[:, 0:cw])
        for k, c0 in enumerate(chunks):
            cs = slice(c0, c0 + cw)
            if k + 1 < len(chunks):
                p_scr[(k + 1) % 2] = _dot(h_scr[...], w_ref[:, c0 + cw:c0 + 2 * cw])
            p = p_scr[k % 2]
            y = p * cw_ref[W - 1:W, cs]
            for t in range(W - 1):
                y = y + pltpu.roll(p, W - 1 - t, 0) * cw_ref[t:t + 1, cs]
            o_ref[:, cs] = _silu(y).astype(o_ref.dtype)
            nfix = 2 * SUBLANES
            head = jnp.concatenate([prev[:, cs], p[:nfix]], axis=0)
            y0 = head[P0:] * cw_ref[W - 1:W, cs]
            for t in range(W - 1):
                y0 = y0 + head[P0 - (W - 1 - t):P0 + nfix - (W - 1 - t)] * cw_ref[t:t + 1, cs]
            o_ref[0:nfix, cs] = _silu(y0).astype(o_ref.dtype)
            tails.append(p[tm - P0:])
        carry_scr[jj] = jnp.concatenate(tails, axis=1)

    for jj in range(n_conv):
        pl.when(j == jj)(functools.partial(conv_tile, jj))

    @pl.when(j >= n_conv)
    def _():
        o_ref[...] = _dot(h_scr[...], w_ref[...]).astype(o_ref.dtype)


def _inproj(geo, x, mod, w, w_tail, tn, conv_w=None, conv_cols=0, tm=None):
    if tm is not None:
        geo = geo._replace(tm=tm)
    T, tm = geo.T, geo.tm
    N = w.shape[1]
    has_tail = w_tail is not None
    n_conv = conv_cols // tn
    in_specs = [
        pl.BlockSpec((tm, D_MODEL), lambda i, j: (i, 0)),
        _mod_spec(geo, 1),
        _mod_spec(geo, 0),
        pl.BlockSpec((D_MODEL, tn), lambda i, j: (0, j)),
    ]
    args = [x, mod, mod, w]
    out_specs = [pl.BlockSpec((tm, tn), lambda i, j: (i, j))]
    out_shape = [jax.ShapeDtypeStruct((T, N), geo.act_dtype)]
    scratch = [pltpu.VMEM((tm, D_MODEL), BF16)]
    if has_tail:
        in_specs.append(pl.BlockSpec((D_MODEL, LANES), lambda i, j: (0, 0)))
        args.append(w_tail)
        out_specs.append(pl.BlockSpec((tm, LANES), lambda i, j: (i, 0)))
        out_shape.append(jax.ShapeDtypeStruct((T, LANES), F32))
    if n_conv:
        assert conv_cols == n_conv * tn and not geo.per_token
        in_specs.append(pl.BlockSpec((DN_CONV_W, tn), lambda i, j: (0, jnp.minimum(j, n_conv - 1))))
        args.append(conv_w)
        scratch += [pltpu.VMEM((n_conv, DN_CONV_PAD, tn), F32), pltpu.VMEM((2, tm, CONV_COL_CHUNK), F32)]
    res = pl.pallas_call(
        functools.partial(_inproj_kernel, has_tail=has_tail, n_conv=n_conv, tiles_per_row=geo.L // tm),
        grid=(T // tm, N // tn),
        in_specs=in_specs,
        out_specs=out_specs,
        out_shape=out_shape,
        scratch_shapes=scratch,
        compiler_params=_cparams(("arbitrary", "arbitrary")),
        name="in_proj",
    )(*args)
    return res if has_tail else (res[0], None)


def _conv_tail_kernel(x_ref, sc_ref, sh_ref, w_ref, o_ref):
    h = x_ref[...] * (1.0 + sc_ref[...]) + sh_ref[...]
    o_ref[...] = _dot(h.astype(BF16), w_ref[...])


def _conv_tail(geo, x, mod, w, n_cols, tn):
    B, L = geo.B, geo.L
    blocks_per_row = L // SUBLANES
    return pl.pallas_call(
        _conv_tail_kernel,
        grid=(n_cols // tn, B),
        in_specs=[
            pl.BlockSpec((SUBLANES, D_MODEL), lambda j, b: (b * blocks_per_row + blocks_per_row - 1, 0)),
            pl.BlockSpec((None, 1, D_MODEL), lambda j, b: (b, 0, 1)),
            pl.BlockSpec((None, 1, D_MODEL), lambda j, b: (b, 0, 0)),
            pl.BlockSpec((D_MODEL, tn), lambda j, b: (0, j)),
        ],
        out_specs=pl.BlockSpec((None, SUBLANES, tn), lambda j, b: (b, 0, j)),
        out_shape=jax.ShapeDtypeStruct((B, SUBLANES, n_cols), F32),
        compiler_params=_cparams(("parallel", "parallel")),
        name="conv_tail",
    )(x, mod, mod, w)


def _call_stacked(kern, prev, first_stacked_out, *, grid, in_specs, args, out_specs, out_shape,
                  scratch_shapes, name):
    n_in = len(args)
    aliases = {}
    body = kern
    if prev is not None:
        n_prev = len(prev)
        in_specs = list(in_specs) + [pl.BlockSpec(memory_space=pl.ANY)] * n_prev
        args = list(args) + list(prev)
        aliases = {n_in + k: first_stacked_out + k for k in range(n_prev)}

        def body(*refs):
            kern(*refs[:n_in], *refs[n_in + n_prev:])

    return pl.pallas_call(
        body, grid=grid, in_specs=in_specs, out_specs=out_specs, out_shape=out_shape,
        scratch_shapes=scratch_shapes, input_output_aliases=aliases,
        compiler_params=_cparams(("parallel", "arbitrary")), name=name,
    )(*args)


def _stack_fill(prev, layer_j, n_layers):
    if prev is None:
        return (layer_j, n_layers), n_layers, 0
    return None, None, layer_j


def _pad_rows(x, rows):
    if x.shape[0] == rows:
        return x
    return jnp.concatenate([x, jnp.zeros((rows - x.shape[0], x.shape[1]), x.dtype)], axis=0)


def _ret_kernel(*refs, c, rows, nchunk, has_state, fill):
    if has_state:
        (q_ref, k_ref, v_ref, g_ref, cos_ref, sin_ref, dintra_ref, qdec_ref, kdec_ref, sdec_ref, gn_ref,
         s0_ref, o_ref, s_out_ref, s_scr) = refs
    else:
        (q_ref, k_ref, v_ref, g_ref, cos_ref, sin_ref, dintra_ref, qdec_ref, kdec_ref, sdec_ref, gn_ref,
         o_ref, s_out_ref, s_scr) = refs
    i = pl.program_id(1)
    half = RET_DK // 2

    @pl.when(i == 0)
    def _():
        if has_state:
            s_scr[...] = s0_ref[...]
        else:
            s_scr[...] = jnp.zeros_like(s_scr)

    def rot(ref, r, h, cos, sin):
        x1 = ref[r, h * RET_DK:h * RET_DK + half].astype(F32)
        x2 = ref[r, h * RET_DK + half:(h + 1) * RET_DK].astype(F32)
        return jnp.concatenate([x1 * cos - x2 * sin, x2 * cos + x1 * sin], axis=1)

    for ci in range(nchunk):
        r = slice(ci * rows, (ci + 1) * rows)
        cos = cos_ref[r, :]
        sin = sin_ref[r, :]
        for h in range(RET_HEADS):
            vs = slice(h * RET_DV, (h + 1) * RET_DV)
            q = _pad_rows(rot(q_ref, r, h, cos, sin), c)
            k = _pad_rows(rot(k_ref, r, h, cos, sin), c) * (RET_DK ** -0.5)
            if rows == c:
                v = v_ref[r, vs].astype(BF16)
            else:
                v = _pad_rows(v_ref[r, vs].astype(F32), c).astype(BF16)
            s = s_scr[h]
            att = _dot_nt(q.astype(BF16), k.astype(BF16)) * dintra_ref[h]
            o = _dot(att.astype(BF16), v) + _dot((q * qdec_ref[h]).astype(BF16), s.astype(BF16))
            s_scr[h] = s * sdec_ref[h] + _dot_tn((k * kdec_ref[h]).astype(BF16), v)
            o = o[:rows]
            mu = jnp.mean(o, axis=-1, keepdims=True)
            oc = o - mu
            var = jnp.mean(oc * oc, axis=-1, keepdims=True)
            on = oc * lax.rsqrt(var + LN_EPS) * gn_ref[...]
            gate = g_ref[r, vs].astype(F32)
            o_ref[r, vs] = (on * _silu(gate)).astype(o_ref.dtype)

    @pl.when(i == pl.num_programs(1) - 1)
    def _():
        if fill is None:
            s_out_ref[...] = s_scr[...]
        else:
            for l in range(fill[1]):
                s_out_ref[l] = s_scr[...] if l == fill[0] else jnp.zeros(s_scr.shape, F32)


def _ret_tables(geo, chunk):
    c_true = min(chunk, geo.l_true)
    half = RET_DK // 2
    inv_freq = 10000.0 ** (-jnp.linspace(0.0, 1.0, half, dtype=F32))
    pos = (geo.pos0 + jnp.arange(geo.L)).astype(F32)
    ang = pos[:, None] * inv_freq[None, :]
    lg = jnp.log(1.0 - 2.0 ** (-5.0 - jnp.arange(RET_HEADS, dtype=F32)))
    idx = jnp.arange(c_true, dtype=F32)
    diff = idx[:, None] - idx[None, :]
    dintra = jnp.exp(jnp.where(diff[None] >= 0, diff[None] * lg[:, None, None], -jnp.inf))
    qdec = jnp.exp((idx + 1.0)[None, :] * lg[:, None])
    kdec = jnp.exp((c_true - 1.0 - idx)[None, :] * lg[:, None])
    sdec = jnp.exp(c_true * lg)
    pad = chunk - c_true
    dintra = jnp.pad(dintra, ((0, 0), (0, pad), (0, pad)))
    qdec = jnp.pad(qdec, ((0, 0), (0, pad)))[..., None]
    kdec = jnp.pad(kdec, ((0, 0), (0, pad)))[..., None]
    sdec = jnp.broadcast_to(sdec[:, None, None], (RET_HEADS, 1, RET_DV))
    return jnp.cos(ang), jnp.sin(ang), dintra, qdec, kdec, sdec


def _retention(geo, p, tables, gn_g, s_all, layer_j, prev, tl, chunk):
    B, L = geo.B, geo.L
    nblk = L // tl
    rows = min(tl, chunk)
    nchunk = tl // rows
    has_state = s_all is not None
    cos, sin, dintra, qdec, kdec, sdec = tables
    rb = lambda b, i: b * nblk + i
    full3 = lambda b, i: (0, 0, 0)
    in_specs = [
        pl.BlockSpec((tl, RET_QW), lambda b, i: (rb(b, i), 0)),
        pl.BlockSpec((tl, RET_QW), lambda b, i: (rb(b, i), 1)),
        pl.BlockSpec((tl, RET_VW), lambda b, i: (rb(b, i), 1)),
        pl.BlockSpec((tl, RET_VW), lambda b, i: (rb(b, i), 2)),
        pl.BlockSpec((tl, RET_DK // 2), lambda b, i: (i, 0)),
        pl.BlockSpec((tl, RET_DK // 2), lambda b, i: (i, 0)),
        pl.BlockSpec((RET_HEADS, chunk, chunk), full3),
        pl.BlockSpec((RET_HEADS, chunk, 1), full3),
        pl.BlockSpec((RET_HEADS, chunk, 1), full3),
        pl.BlockSpec((RET_HEADS, 1, RET_DV), full3),
        pl.BlockSpec((1, RET_DV), lambda b, i: (0, 0)),
    ]
    args = [p, p, p, p, cos, sin, dintra, qdec, kdec, sdec, gn_g.reshape(1, RET_DV)]
    if has_state:
        in_specs.append(pl.BlockSpec((None, None, RET_HEADS, RET_DK, RET_DV),
                                     lambda b, i: (layer_j, b, 0, 0, 0)))
        args.append(s_all)
    fill, lead, lidx = _stack_fill(prev, layer_j, N_RET_LAYERS)
    o, s = _call_stacked(
        functools.partial(_ret_kernel, c=chunk, rows=rows, nchunk=nchunk, has_state=has_state, fill=fill),
        prev, first_stacked_out=1,
        grid=(B, nblk),
        in_specs=in_specs,
        args=args,
        out_specs=[
            pl.BlockSpec((tl, RET_VW), lambda b, i: (rb(b, i), 0)),
            pl.BlockSpec((lead, None, RET_HEADS, RET_DK, RET_DV), lambda b, i: (lidx, b, 0, 0, 0)),
        ],
        out_shape=[
            jax.ShapeDtypeStruct((geo.T, RET_VW), geo.act_dtype),
            jax.ShapeDtypeStruct((N_RET_LAYERS, B, RET_HEADS, RET_DK, RET_DV), F32),
        ],
        scratch_shapes=[pltpu.VMEM((RET_HEADS, RET_DK, RET_DV), F32)],
        name="retention",
    )
    return o, (s,)


DN_CONV_PAD = SUBLANES
CONV_COL_CHUNK = 256


def _block_diag(x_cat, n, mask):
    return jnp.where(mask, jnp.concatenate([x_cat] * n, axis=0), 0.0)


def _dn_kernel(*refs, c, rows, nchunk, l_last, has_state, fill):
    if has_state:
        (qkv_ref, z_ref, ab_ref, cw_ref, alog_ref, dtb_ref, ng_ref, s0_ref, buf_ref,
         o_ref, s_out_ref, buf_out_ref, s_scr, xp_scr, y_scr) = refs
    else:
        (qkv_ref, z_ref, ab_ref, alog_ref, dtb_ref, ng_ref, o_ref, s_out_ref, s_scr) = refs
    i = pl.program_id(1)
    hq = min(DN_V_HEADS, MXU_DIM // c)
    tl = rows * nchunk
    npair = DN_V_HEADS // 2
    W = DN_CONV_W
    P0 = DN_CONV_PAD

    @pl.when(i == 0)
    def _():
        if has_state:
            for pr in range(npair):
                s_scr[pr] = jnp.concatenate([s0_ref[2 * pr], s0_ref[2 * pr + 1]], axis=1)
            xp_scr[P0 - (W - 1):P0, :] = buf_ref[...]
        else:
            s_scr[...] = jnp.zeros_like(s_scr)

    if has_state:
        xp_scr[P0:P0 + tl, :] = qkv_ref[...].astype(F32)
        y = xp_scr[P0 - (W - 1):P0 - (W - 1) + tl, :] * cw_ref[0:1, :]
        for t in range(1, W):
            y = y + xp_scr[P0 - (W - 1) + t:P0 - (W - 1) + t + tl, :] * cw_ref[t:t + 1, :]
        y_scr[0:tl, :] = _silu(y)
        if rows < c:
            y_scr[tl:, :] = jnp.zeros((y_scr.shape[0] - tl, DN_CONV_CH), F32)

        @pl.when(i == pl.num_programs(1) - 1)
        def _():
            tail = xp_scr[P0 + l_last - (W - 1):P0 + l_last, :]
            if fill is None:
                buf_out_ref[...] = tail
            else:
                for l in range(fill[1]):
                    buf_out_ref[l] = tail if l == fill[0] else jnp.zeros(tail.shape, F32)

        xp_scr[P0 - (W - 1):P0, :] = xp_scr[P0 + tl - (W - 1):P0 + tl, :]

        def ycols(r, lo, hi):
            return y_scr[r, lo:hi]
    else:
        def ycols(r, lo, hi):
            return qkv_ref[r, lo:hi].astype(F32)

    ri = lax.broadcasted_iota(jnp.int32, (c, c), 0)
    ci_ = lax.broadcasted_iota(jnp.int32, (c, c), 1)
    incl = ri >= ci_
    strict = ri > ci_
    eye = ri == ci_
    tri_f = incl.astype(F32)
    eye_cat = jnp.concatenate([eye.astype(F32)] * hq, axis=1)
    bi = lax.broadcasted_iota(jnp.int32, (hq * c, hq * c), 0) // c
    bj = lax.broadcasted_iota(jnp.int32, (hq * c, hq * c), 1) // c
    bd_mask = bi == bj
    lane2 = lax.broadcasted_iota(jnp.int32, (DN_DK, 2 * DN_DV), 1)
    left = lane2 < DN_DV
    padded = rows < c or l_last < rows
    row_ok = lax.broadcasted_iota(jnp.int32, (c, 1), 0) < l_last if padded else None
    n_levels = c.bit_length() - 1

    oi = lax.broadcasted_iota(jnp.int32, (2 * DN_DK, 2 * DN_DK), 0) // DN_DK
    oj = lax.broadcasted_iota(jnp.int32, (2 * DN_DK, 2 * DN_DK), 1) // DN_DK
    ones_bd = jnp.where(oi == oj, 1.0, 0.0).astype(BF16)

    def sumsq2(x):
        if has_state:
            parts = [jnp.broadcast_to(jnp.sum(x[:, lo:lo + DN_DK] * x[:, lo:lo + DN_DK], axis=-1, keepdims=True),
                                      (x.shape[0], DN_DK)) for lo in (0, DN_DK)]
            return jnp.concatenate(parts, axis=1)
        return _dot((x * x).astype(BF16), ones_bd)

    nquad = DN_V_HEADS // hq
    quads = [[hq * q + t for t in range(hq)] for q in range(nquad)]

    def phase1(ck):
        r0 = ck * c if rows == c else 0
        r = slice(r0, r0 + c)
        ab = _pad_rows(ab_ref[ck * rows:(ck + 1) * rows, :], c)
        sp = ab + dtb_ref[...]
        g_full = -jnp.exp(alog_ref[...]) * (jnp.maximum(sp, 0.0) + jnp.log1p(jnp.exp(-jnp.abs(sp))))
        beta_full = jax.nn.sigmoid(ab)
        if row_ok is not None:
            g_full = jnp.where(row_ok, g_full, 0.0)
            beta_full = jnp.where(row_ok, beta_full, 0.0)
        G_full = jnp.dot(tri_f, g_full, preferred_element_type=F32, precision=lax.Precision.HIGHEST)
        yield

        kn, qn, KK, QK = {}, {}, {}, {}
        for j in range(0, DN_QK_HEADS, 2):
            q2 = ycols(r, j * DN_DK, (j + 2) * DN_DK)
            k2 = ycols(r, DN_QW + j * DN_DK, DN_QW + (j + 2) * DN_DK)
            q2 = q2 * lax.rsqrt(sumsq2(q2) + 1e-6) * (DN_DK ** -0.5)
            k2 = k2 * lax.rsqrt(sumsq2(k2) + 1e-6)
            qn[j], qn[j + 1] = q2[:, :DN_DK], q2[:, DN_DK:]
            kn[j], kn[j + 1] = k2[:, :DN_DK], k2[:, DN_DK:]
        yield
        for j in range(DN_QK_HEADS):
            kb16 = kn[j].astype(BF16)
            kq = _dot_nt(jnp.concatenate([kb16, qn[j].astype(BF16)], axis=0), kb16)
            KK[j], QK[j] = kq[:c], kq[c:]
        yield

        A, att, eG, Gc, rhs = {}, {}, {}, {}, {}
        for h in range(DN_V_HEADS):
            j = h // 2
            Gc[h] = jnp.broadcast_to(G_full[:, h:h + 1], (c, DN_DV))
            bc = jnp.broadcast_to(beta_full[:, DN_V_HEADS + h:DN_V_HEADS + h + 1], (c, DN_DV))
            Gcc = Gc[h][:, :c]
            Gr = jnp.sum(jnp.where(eye, Gcc, 0.0), axis=0, keepdims=True)
            dec = jnp.exp(jnp.where(incl, Gcc - Gr, -jnp.inf))
            A[h] = jnp.where(strict, bc[:, :c] * KK[j] * dec, 0.0)
            att[h] = QK[j] * dec
            eG[h] = jnp.exp(Gc[h])
            vh = ycols(r, 2 * DN_QW + h * DN_DV, 2 * DN_QW + (h + 1) * DN_DV)
            rhs[h] = jnp.concatenate([vh * bc, kn[j] * (bc * eG[h])], axis=1)
        yield

        X = [jnp.concatenate([A[h] for h in hs], axis=1) for hs in quads]
        Pm = [eye_cat - x for x in X]
        X = [_dot(x.astype(BF16), _block_diag(x, hq, bd_mask).astype(BF16)) for x in X]
        yield
        for lvl in range(1, n_levels):
            Xw = [_block_diag(x, hq, bd_mask).astype(BF16) for x in X]
            if lvl < n_levels - 1:
                both = [_dot(jnp.concatenate([X[q], Pm[q]], axis=0).astype(BF16), Xw[q]) for q in range(nquad)]
                X = [b[:c] for b in both]
                Pm = [Pm[q] + both[q][c:] for q in range(nquad)]
            else:
                Pm = [Pm[q] + _dot(Pm[q].astype(BF16), Xw[q]) for q in range(nquad)]
            yield
        sol = [_dot(_block_diag(Pm[q], hq, bd_mask).astype(BF16),
                    jnp.concatenate([rhs[h] for h in quads[q]], axis=0).astype(BF16))
               for q in range(nquad)]
        att_bd = [_block_diag(jnp.concatenate([att[h] for h in quads[q]], axis=1), hq, bd_mask).astype(BF16)
                  for q in range(nquad)]
        return kn, qn, Gc, eG, sol, att_bd

    def phase2(ck, pre):
        kn, qn, Gc, eG, sol, att_bd = pre
        sol_h = {h: sol[h // hq][c * (h % hq):c * (h % hq + 1)] for h in range(DN_V_HEADS)}
        u, oS = {}, {}
        for pr in range(npair):
            h0, h1 = 2 * pr, 2 * pr + 1
            S = s_scr[pr]
            Sbd = jnp.concatenate([jnp.where(left, S, 0.0), jnp.where(left, 0.0, S)], axis=0).astype(BF16)
            w_cat = jnp.concatenate([sol_h[h0][:, DN_DV:], sol_h[h1][:, DN_DV:]], axis=1)
            qe_cat = jnp.concatenate([qn[pr] * eG[h0], qn[pr] * eG[h1]], axis=1)
            R = _dot(jnp.concatenate([w_cat, qe_cat], axis=0).astype(BF16), Sbd)
            u[h0] = sol_h[h0][:, :DN_DV] - R[:c, :DN_DV]
            u[h1] = sol_h[h1][:, :DN_DV] - R[:c, DN_DV:]
            oS[h0] = R[c:, :DN_DV]
            oS[h1] = R[c:, DN_DV:]
        yield

        o_intra = [_dot(att_bd[q], jnp.concatenate([u[h] for h in quads[q]], axis=0).astype(BF16))
                   for q in range(nquad)]
        yield

        for pr in range(npair):
            h0, h1 = 2 * pr, 2 * pr + 1
            gl0 = Gc[h0][c - 1:c, :]
            gl1 = Gc[h1][c - 1:c, :]
            du = jnp.concatenate([u[h0] * jnp.exp(gl0 - Gc[h0]), u[h1] * jnp.exp(gl1 - Gc[h1])], axis=1)
            sdec = jnp.concatenate([jnp.exp(gl0), jnp.exp(gl1)], axis=1)
            s_scr[pr] = s_scr[pr] * sdec + _dot_tn(kn[pr].astype(BF16), du.astype(BF16))
        yield

        zr = slice(ck * rows, (ck + 1) * rows)
        ng2 = jnp.concatenate([ng_ref[...], ng_ref[...]], axis=1)
        for h in range(0, DN_V_HEADS, 2):
            o = jnp.concatenate([(oS[hh] + o_intra[hh // hq][c * (hh % hq):c * (hh % hq + 1)])[:rows]
                                 for hh in (h, h + 1)], axis=1)
            o = o * lax.rsqrt(sumsq2(o) * (1.0 / DN_DV) + 1e-6) * ng2
            hs = slice(h * DN_DV, (h + 2) * DN_DV)
            z = z_ref[zr, hs].astype(F32)
            o_ref[zr, hs] = (o * _silu(z)).astype(o_ref.dtype)

    def lockstep(gens):
        out = {}
        while gens:
            for key in list(gens):
                try:
                    next(gens[key])
                except StopIteration as done:
                    out[key] = done.value
                    del gens[key]
        return out

    def recurrence(group, pre):
        for ck in group:
            yield from phase2(ck, pre[ck])

    groups = [list(range(g0, min(g0 + DN_LOCKSTEP, nchunk))) for g0 in range(0, nchunk, DN_LOCKSTEP)]
    pre = lockstep({ck: phase1(ck) for ck in groups[0]})
    for gi, group in enumerate(groups):
        gens = {"recurrence": recurrence(group, pre)}
        if gi + 1 < len(groups):
            gens.update({ck: phase1(ck) for ck in groups[gi + 1]})
        pre = lockstep(gens)

    @pl.when(i == pl.num_programs(1) - 1)
    def _():
        for pr in range(npair):
            for l in range(1 if fill is None else fill[1]):
                mine = fill is None or l == fill[0]
                dst = s_out_ref if fill is None else s_out_ref.at[l]
                dst[2 * pr] = s_scr[pr][:, :DN_DV] if mine else jnp.zeros((DN_DK, DN_DV), F32)
                dst[2 * pr + 1] = s_scr[pr][:, DN_DV:] if mine else jnp.zeros((DN_DK, DN_DV), F32)


def _deltanet(geo, p, ab, conv_w, a_log, dt_bias, norm_g, s_all, buf_all, layer_j, prev, tl, chunk):
    B, L = geo.B, geo.L
    nblk = L // tl
    rows = min(tl, chunk)
    nchunk = tl // rows
    l_last = geo.l_true - (nblk - 1) * tl
    assert l_last == tl or (nblk == 1 and nchunk == 1)
    has_state = s_all is not None
    rb = lambda b, i: b * nblk + i
    pad16 = lambda v: jnp.pad(v.reshape(1, DN_V_HEADS), ((0, 0), (0, LANES - DN_V_HEADS)))
    row1 = lambda n: pl.BlockSpec((1, n), lambda b, i: (0, 0))
    in_specs = [
        pl.BlockSpec((tl, DN_CONV_CH), lambda b, i: (rb(b, i), 0)),
        pl.BlockSpec((tl, DN_VW), lambda b, i: (rb(b, i), DN_CONV_CH // DN_VW)),
        pl.BlockSpec((tl, LANES), lambda b, i: (rb(b, i), 0)),
    ]
    args = [p, p, ab]
    if has_state:
        in_specs.append(pl.BlockSpec((DN_CONV_W, DN_CONV_CH), lambda b, i: (0, 0)))
        args.append(conv_w)
    in_specs += [row1(LANES), row1(LANES), row1(DN_DV)]
    args += [pad16(a_log), pad16(dt_bias), norm_g.reshape(1, DN_DV)]
    fill, lead, lidx = _stack_fill(prev, layer_j, N_DN_LAYERS)
    out_specs = [
        pl.BlockSpec((tl, DN_VW), lambda b, i: (rb(b, i), 0)),
        pl.BlockSpec((lead, None, DN_V_HEADS, DN_DK, DN_DV), lambda b, i: (lidx, b, 0, 0, 0)),
    ]
    out_shape = [
        jax.ShapeDtypeStruct((geo.T, DN_VW), geo.act_dtype),
        jax.ShapeDtypeStruct((N_DN_LAYERS, B, DN_V_HEADS, DN_DK, DN_DV), F32),
    ]
    scratch = [pltpu.VMEM((DN_V_HEADS // 2, DN_DK, 2 * DN_DV), F32)]
    if has_state:
        in_specs += [
            pl.BlockSpec((None, None, DN_V_HEADS, DN_DK, DN_DV), lambda b, i: (layer_j, b, 0, 0, 0)),
            pl.BlockSpec((None, None, DN_CONV_W - 1, DN_CONV_CH), lambda b, i: (layer_j, b, 0, 0)),
        ]
        args += [s_all, buf_all]
        out_specs.append(pl.BlockSpec((lead, None, DN_CONV_W - 1, DN_CONV_CH), lambda b, i: (lidx, b, 0, 0)))
        out_shape.append(jax.ShapeDtypeStruct((N_DN_LAYERS, B, DN_CONV_W - 1, DN_CONV_CH), F32))
        scratch += [pltpu.VMEM((DN_CONV_PAD + tl, DN_CONV_CH), F32),
                    pltpu.VMEM((max(tl, chunk), DN_CONV_CH), F32)]
    kern = functools.partial(_dn_kernel, c=chunk, rows=rows, nchunk=nchunk, l_last=l_last, has_state=has_state,
                             fill=fill)
    res = _call_stacked(kern, prev, first_stacked_out=1, grid=(B, nblk), in_specs=in_specs, args=args,
                        out_specs=out_specs, out_shape=out_shape, scratch_shapes=scratch, name="deltanet")
    return res[0], tuple(res[1:])


def _outproj_kernel(o_ref, w_ref, x_ref, gate_ref, g_ref, b_ref, out_ref):
    y = _dot(o_ref[...].astype(BF16), w_ref[...])
    z = ALPHA * x_ref[...] + (1.0 + gate_ref[...]) * y
    out_ref[...] = _layer_norm(z, g_ref[...], b_ref[...])


def _outproj_ln(geo, o, w_out, x, mod, ln_g, ln_b):
    T, tm = geo.T, geo.tm
    K = w_out.shape[0]
    vec = pl.BlockSpec((1, D_MODEL), lambda i: (0, 0))
    return pl.pallas_call(
        _outproj_kernel,
        grid=(T // tm,),
        in_specs=[
            pl.BlockSpec((tm, K), lambda i: (i, 0)),
            pl.BlockSpec((K, D_MODEL), lambda i: (0, 0)),
            pl.BlockSpec((tm, D_MODEL), lambda i: (i, 0)),
            _mod_spec(geo, 2),
            vec, vec,
        ],
        out_specs=pl.BlockSpec((tm, D_MODEL), lambda i: (i, 0)),
        out_shape=jax.ShapeDtypeStruct((T, D_MODEL), F32),
        compiler_params=_cparams(("parallel",)),
        name="out_proj_ln",
    )(o, w_out, x, mod, ln_g.reshape(1, D_MODEL), ln_b.reshape(1, D_MODEL))


MOE_ROWS = 144


def _moe_blocks(tm):
    nb = (tm - N_GROUPS) // MOE_ROWS + N_GROUPS
    ns = -(-(nb * MOE_ROWS) // LANES) * LANES
    return nb, ns


def _first_max(v, row, valid):
    m = jnp.max(jnp.where(valid, v, -jnp.inf), axis=0, keepdims=True)
    idx = jnp.min(jnp.where(valid & (v == m), row, float(N_EXPERTS)), axis=0, keepdims=True)
    return m, idx


def _route(h, wrt, br):
    h_hi = h.astype(BF16)
    h_lo = (h - h_hi.astype(F32)).astype(BF16)
    w_hi = wrt.astype(BF16)
    w_lo = (wrt - w_hi.astype(F32)).astype(BF16)
    ww = _dot_nt(jnp.concatenate([w_hi, w_lo], axis=0), h_hi)
    logits = ww[:N_EXPERTS] + ww[N_EXPERTS:] + _dot_nt(w_hi, h_lo)
    mx = jnp.max(logits, axis=0, keepdims=True)
    ex = jnp.exp(logits - mx)
    probs = ex / jnp.sum(ex, axis=0, keepdims=True)
    sel = probs + br
    row_i = lax.broadcasted_iota(jnp.int32, sel.shape, 0)
    grp = (row_i // EXPERTS_PER_GROUP).astype(F32)
    row = row_i.astype(F32)
    best = None
    gsel = None
    for g in range(N_GROUPS):
        in_g = grp == g
        m1, i1 = _first_max(sel, row, in_g)
        m2, _ = _first_max(sel, row, in_g & (row != i1))
        score = m1 + m2
        if g == 0:
            best, gsel = score, jnp.zeros_like(i1)
        else:
            better = score > best
            gsel = jnp.where(better, float(g), gsel)
            best = jnp.where(better, score, best)
    in_grp = grp == gsel
    _, i1 = _first_max(sel, row, in_grp)
    _, i2 = _first_max(sel, row, in_grp & (row != i1))
    w1 = jnp.sum(jnp.where(row == i1, probs, 0.0), axis=0, keepdims=True)
    w2 = jnp.sum(jnp.where(row == i2, probs, 0.0), axis=0, keepdims=True)
    wsum = w1 + w2
    gates = jnp.where(row == i1, w1 / wsum, 0.0) + jnp.where(row == i2, w2 / wsum, 0.0)
    return gsel, gates


def _router_kernel(x_ref, sc_ref, sh_ref, wrt_ref, br_ref, posr_ref, gates_ref, tbl_ref):
    tm = x_ref.shape[0]
    h = x_ref[...] * (1.0 + sc_ref[...]) + sh_ref[...]
    gsel, gates = _route(h, wrt_ref[...], br_ref[...])
    gates_ref[...] = gates
    grow = lax.broadcasted_iota(jnp.int32, (SUBLANES, tm), 0).astype(F32)
    og = grow == gsel
    ti = lax.broadcasted_iota(jnp.int32, (tm, tm), 0)
    tj = lax.broadcasted_iota(jnp.int32, (tm, tm), 1)
    before = jnp.where(ti < tj, 1.0, 0.0).astype(BF16)
    ogf = jnp.where(og, 1.0, 0.0)
    rank = _dot(ogf.astype(BF16), before)
    cnt = jnp.sum(ogf, axis=1, keepdims=True)
    nblk = jnp.zeros_like(cnt)
    for kb in range(-(-tm // MOE_ROWS)):
        nblk = nblk + jnp.where(cnt > float(kb * MOE_ROWS), 1.0, 0.0)
    seg = nblk * float(MOE_ROWS)
    block_start = lax.broadcasted_iota(jnp.int32, (1, LANES), 1).astype(F32) * float(MOE_ROWS)
    off = jnp.zeros((1, 1), F32)
    pos = jnp.zeros((1, tm), F32)
    tbl = jnp.zeros((1, LANES), F32)
    for g in range(N_GROUPS):
        pos = pos + jnp.where(gsel == float(g), off + rank[g:g + 1, :], 0.0)
        off = off + seg[g:g + 1, :]
        tbl = tbl + jnp.where(block_start >= off, 1.0, 0.0)
    posr_ref[...] = pos
    tbl_ref[...] = tbl.astype(jnp.int32)


def _expert_kernel(tbl_ref, x_ref, sc_ref, sh_ref, gt_ref, posr_ref, gates_ref, wgu_ref, wd_ref,
                   g_ref, b_ref, out_ref, hs_scr, gs_scr, ys_scr, p_scr, *, nb):
    i = pl.program_id(0)
    tm = x_ref.shape[0]
    ns = hs_scr.shape[0]
    R = MOE_ROWS
    x = x_ref[...]
    hb = (x * (1.0 + sc_ref[...]) + sh_ref[...]).astype(BF16)
    slot = lax.broadcasted_iota(jnp.int32, (ns, tm), 0).astype(F32)
    P = jnp.where(slot == posr_ref[...], 1.0, 0.0).astype(BF16)
    p_scr[...] = P
    hs_scr[...] = _dot(P, hb).astype(BF16)
    gates = gates_ref[...]
    g_hi = gates.astype(BF16)
    g_lo = (gates - g_hi.astype(F32)).astype(BF16)
    gs_scr[...] = _dot_nt(P, g_hi) + _dot_nt(P, g_lo)
    if nb * R < ns:
        ys_scr[nb * R:, :] = jnp.zeros((ns - nb * R, D_MODEL), BF16)

    for b in range(nb):
        rs = slice(b * R, (b + 1) * R)
        g = tbl_ref[i, b]

        @pl.when(g < N_GROUPS)
        def _():
            gsb = gs_scr[rs, :]
            lane = lax.broadcasted_iota(jnp.int32, gsb.shape, 1)
            hsb = hs_scr[rs, :]
            acts = []
            for e in range(EXPERTS_PER_GROUP):
                gu = _dot(hsb, wgu_ref[g, e])
                gate = jnp.sum(jnp.where(lane == g * EXPERTS_PER_GROUP + e, gsb, 0.0), axis=1, keepdims=True)
                act = _silu(gu[:, :D_EXPERT]) * gu[:, D_EXPERT:] * gate
                acts.append(act.astype(BF16))
            ys_scr[rs, :] = _dot(jnp.concatenate(acts, axis=1), wd_ref[g]).astype(BF16)

        @pl.when(g >= N_GROUPS)
        def _():
            ys_scr[rs, :] = jnp.zeros((R, D_MODEL), BF16)

    y = _dot_tn(p_scr[...], ys_scr[...])
    z = ALPHA * x + (1.0 + gt_ref[...]) * y
    out_ref[...] = _layer_norm(z, g_ref[...], b_ref[...])


def _moe_ln(geo, x, mod, w_router_t, b_router, w_gu, w_down, layer, ln_g, ln_b):
    T, tm = geo.T, geo.tm
    nt = T // tm
    nb, ns = _moe_blocks(tm)
    posr, gates, tbl = pl.pallas_call(
        _router_kernel,
        grid=(nt,),
        in_specs=[
            pl.BlockSpec((tm, D_MODEL), lambda i: (i, 0)),
            _mod_spec(geo, 4),
            _mod_spec(geo, 3),
            pl.BlockSpec((N_EXPERTS, D_MODEL), lambda i: (0, 0)),
            pl.BlockSpec((N_EXPERTS, 1), lambda i: (0, 0)),
        ],
        out_specs=[
            pl.BlockSpec((None, 1, tm), lambda i: (i, 0, 0)),
            pl.BlockSpec((N_EXPERTS, tm), lambda i: (0, i)),
            pl.BlockSpec((None, 1, LANES), lambda i: (i, 0, 0)),
        ],
        out_shape=[
            jax.ShapeDtypeStruct((nt, 1, tm), F32),
            jax.ShapeDtypeStruct((N_EXPERTS, T), F32),
            jax.ShapeDtypeStruct((nt, 1, LANES), jnp.int32),
        ],
        compiler_params=_cparams(("parallel",)),
        name="moe_router",
    )(x, mod, mod, w_router_t, b_router.reshape(N_EXPERTS, 1))

    vec = pl.BlockSpec((1, D_MODEL), lambda i, t: (0, 0))
    resident = dict(pipeline_mode=pl.Buffered(1))
    grid_spec = pltpu.PrefetchScalarGridSpec(
        num_scalar_prefetch=1,
        grid=(nt,),
        in_specs=[
            pl.BlockSpec((tm, D_MODEL), lambda i, t: (i, 0)),
            _mod_spec(geo, 4),
            _mod_spec(geo, 3),
            _mod_spec(geo, 5),
            pl.BlockSpec((None, 1, tm), lambda i, t: (i, 0, 0)),
            pl.BlockSpec((N_EXPERTS, tm), lambda i, t: (0, i)),
            pl.BlockSpec((None, N_GROUPS, EXPERTS_PER_GROUP, D_MODEL, 2 * D_EXPERT),
                         lambda i, t: (layer, 0, 0, 0, 0), **resident),
            pl.BlockSpec((None, N_GROUPS, EXPERTS_PER_GROUP * D_EXPERT, D_MODEL),
                         lambda i, t: (layer, 0, 0, 0), **resident),
            vec, vec,
        ],
        out_specs=pl.BlockSpec((tm, D_MODEL), lambda i, t: (i, 0)),
        scratch_shapes=[
            pltpu.VMEM((ns, D_MODEL), BF16),
            pltpu.VMEM((ns, N_EXPERTS), F32),
            pltpu.VMEM((ns, D_MODEL), BF16),
            pltpu.VMEM((ns, tm), BF16),
        ],
    )
    return pl.pallas_call(
        functools.partial(_expert_kernel, nb=nb),
        grid_spec=grid_spec,
        out_shape=jax.ShapeDtypeStruct((T, D_MODEL), F32),
        compiler_params=_cparams(("parallel",)),
        name="moe_experts_ln",
    )(tbl.reshape(nt, LANES), x, mod, mod, mod, posr, gates, w_gu, w_down,
      ln_g.reshape(1, D_MODEL), ln_b.reshape(1, D_MODEL))


def _trunk(geo, x, mods, s_ret, s_dn, s_conv, wts, ret_tl, ret_chunk, dn_tl, dn_chunk):
    (ln_g, ln_b, ret_w_in, ret_gn_g, ret_w_out, dn_w_main, dn_w_tail, dn_conv_w, dn_a_log, dn_dt_bias,
     dn_norm_g, dn_w_out, router_w, router_b, moe_w_gu, moe_w_down) = wts
    tables = _ret_tables(geo, ret_chunk)
    decode = s_dn is not None
    proj_tm = None if geo.per_token or geo.L % 1024 else 1024
    ret_stack, dn_stack, conv_tails = None, None, []
    for i in range(DEPTH):
        if geo.per_token:
            mod = jnp.repeat(mods[i], geo.L, axis=0)
        else:
            mod = mods[i].reshape(geo.B, 1, N_MOD * D_MODEL)
        j = i // 2
        if i % 2 == 0:
            p, _ = _inproj(geo, x, mod, ret_w_in[j], None, tn=1536 if decode else 3072, tm=proj_tm)
            o, ret_stack = _retention(geo, p, tables, ret_gn_g[j], s_ret, j, ret_stack, ret_tl, ret_chunk)
            w_out = ret_w_out[j]
        else:
            if decode:
                p, ab = _inproj(geo, x, mod, dn_w_main[j], dn_w_tail[j], tn=1536)
            else:
                p, ab = _inproj(geo, x, mod, dn_w_main[j], dn_w_tail[j], tn=2048,
                                conv_w=dn_conv_w[j], conv_cols=DN_CONV_CH, tm=proj_tm)
                conv_tails.append(_conv_tail(geo, x, mod, dn_w_main[j], DN_CONV_CH, tn=2048))
            o, dn_stack = _deltanet(geo, p, ab, dn_conv_w[j], dn_a_log[j], dn_dt_bias[j], dn_norm_g[j],
                                    s_dn, s_conv, j, dn_stack, dn_tl, dn_chunk)
            w_out = dn_w_out[j]
        x = _outproj_ln(geo, o, w_out, x, mod, ln_g[i, 0], ln_b[i, 0])
        x = _moe_ln(geo, x, mod, router_w, router_b, moe_w_gu, moe_w_down, i, ln_g[i, 1], ln_b[i, 1])
    if decode:
        new_dn, new_conv = dn_stack
    else:
        new_dn = dn_stack[0]
        new_conv = jnp.stack(conv_tails)[:, :, SUBLANES - (DN_CONV_W - 1):]
    return x, ret_stack[0], new_dn, new_conv


def _run(x_prompt, x_sample, state_ret, state_dn, state_conv, c_prompt, c_sample, w_ada, b_ada, ln_g, ln_b,
         ret_w_in, ret_gn_g, ret_w_out, dn_w_in, dn_conv_w, dn_a_log, dn_dt_bias, dn_norm_g, dn_w_out,
         router_w, router_b, moe_w_gu, moe_w_down, *, past_len):
    Bp, Lp, _ = x_prompt.shape
    Bs, Ls, _ = x_sample.shape
    Ls_pad = -(-Ls // SUBLANES) * SUBLANES
    tm_p = min(512, Lp)
    geo_p = _Geo(Bp, Lp, Lp, 0, tm_p, False, BF16)
    Ts = Bs * Ls_pad
    geo_s = _Geo(Bs, Ls_pad, Ls, past_len, min(512, Ts), True, F32)

    dn_w_tail = jnp.pad(dn_w_in[:, :, DN_MAIN:], ((0, 0), (0, 0), (0, LANES - 2 * DN_V_HEADS))).astype(BF16)
    w_gu_grp = moe_w_gu.astype(BF16).reshape(DEPTH, N_GROUPS, EXPERTS_PER_GROUP, D_MODEL, 2 * D_EXPERT)
    w_down_grp = moe_w_down.astype(BF16).reshape(DEPTH, N_GROUPS, EXPERTS_PER_GROUP * D_EXPERT, D_MODEL)
    wts = (ln_g, ln_b, ret_w_in.astype(BF16), ret_gn_g, ret_w_out.astype(BF16),
           dn_w_in[:, :, :DN_MAIN].astype(BF16), dn_w_tail, dn_conv_w, dn_a_log, dn_dt_bias, dn_norm_g,
           dn_w_out.astype(BF16), router_w.T, router_b, w_gu_grp, w_down_grp)

    mods = _ada(jnp.concatenate([c_prompt, c_sample], axis=0), w_ada, b_ada)

    xp = x_prompt.reshape(Bp * Lp, D_MODEL)
    yp, ret_p, dn_p, conv_p = _trunk(geo_p, xp, mods[:, :Bp], None, None, None, wts,
                                     ret_tl=min(512, Lp), ret_chunk=min(RET_CHUNK, Lp),
                                     dn_tl=min(256, Lp), dn_chunk=DN_CHUNK)
    xs = jnp.pad(x_sample, ((0, 0), (0, Ls_pad - Ls), (0, 0))).reshape(Ts, D_MODEL)
    ys, ret_s, dn_s, conv_s = _trunk(geo_s, xs, mods[:, Bp:], state_ret, state_dn, state_conv, wts,
                                     ret_tl=Ls_pad, ret_chunk=max(Ls_pad, RET_DECODE_CHUNK),
                                     dn_tl=Ls_pad, dn_chunk=max(Ls_pad, DN_DECODE_CHUNK))
    y_prompt = yp.reshape(Bp, Lp, D_MODEL)
    y_sample = ys.reshape(Bs, Ls_pad, D_MODEL)[:, :Ls]
    return (y_prompt, y_sample, ret_p, ret_s, dn_p, dn_s, conv_p, conv_s)


def kernel(x_prompt, x_sample, state_ret, state_dn, state_conv, c_prompt, c_sample, w_ada, b_ada, ln_g, ln_b,
           ret_w_in, ret_gn_g, ret_w_out, dn_w_in, dn_conv_w, dn_a_log, dn_dt_bias, dn_norm_g, dn_w_out,
           router_w, router_b, moe_w_gu, moe_w_down):
    return _run(x_prompt, x_sample, state_ret, state_dn, state_conv, c_prompt, c_sample, w_ada, b_ada,
                ln_g, ln_b, ret_w_in, ret_gn_g, ret_w_out, dn_w_in, dn_conv_w, dn_a_log, dn_dt_bias,
                dn_norm_g, dn_w_out, router_w, router_b, moe_w_gu, moe_w_down, past_len=PAST_LEN)
```

```python
import functools
from typing import NamedTuple

import jax
import jax.numpy as jnp
from jax import lax
from jax.experimental import pallas as pl
from jax.experimental.pallas import tpu as pltpu

F32 = jnp.float32
BF16 = jnp.bfloat16

D_MODEL = 1024
DEPTH = 4
N_RET_LAYERS = (DEPTH + 1) // 2
N_DN_LAYERS = DEPTH // 2
PAST_LEN = 16384
RET_HEADS = 4
RET_DK = 256
RET_DV = 512
RET_QW = RET_HEADS * RET_DK
RET_VW = RET_HEADS * RET_DV
RET_CHUNK = 256
RET_DECODE_CHUNK = 16
DN_QK_HEADS = 8
DN_V_HEADS = 16
DN_DK = 128
DN_DV = 128
DN_QW = DN_QK_HEADS * DN_DK
DN_VW = DN_V_HEADS * DN_DV
DN_CONV_CH = 2 * DN_QW + DN_VW
DN_CONV_W = 4
DN_CHUNK = 64
DN_DECODE_CHUNK = 8
DN_LOCKSTEP = 2
DN_MAIN = DN_CONV_CH + DN_VW
N_EXPERTS = 16
N_GROUPS = 4
EXPERTS_PER_GROUP = N_EXPERTS // N_GROUPS
D_EXPERT = 256
N_MOD = 6
ALPHA = (2.0 * DEPTH) ** 0.25
LN_EPS = 1e-5

LANES = 128
SUBLANES = 8
MXU_DIM = 256
VMEM_LIMIT = 56 * 1024 * 1024


class _Geo(NamedTuple):
    B: int
    L: int
    l_true: int
    pos0: int
    tm: int
    per_token: bool
    act_dtype: object

    @property
    def T(self):
        return self.B * self.L


def _cparams(sem):
    return pltpu.CompilerParams(dimension_semantics=sem, vmem_limit_bytes=VMEM_LIMIT)


def _mod_spec(geo, chunk):
    if geo.per_token:
        return pl.BlockSpec((geo.tm, D_MODEL), lambda *g: (g[0], chunk))
    tpr = geo.L // geo.tm
    return pl.BlockSpec((None, 1, D_MODEL), lambda *g: (g[0] // tpr, 0, chunk))


def _silu(x):
    hx = 0.5 * x
    return hx + hx * jnp.tanh(hx)


def _dot(a, b):
    return jnp.dot(a, b, preferred_element_type=F32)


def _dot_nt(a, b):
    return lax.dot_general(a, b, (((1,), (1,)), ((), ())), preferred_element_type=F32)


def _dot_tn(a, b):
    return lax.dot_general(a, b, (((0,), (0,)), ((), ())), preferred_element_type=F32)


def _layer_norm(z, g, b):
    mu = jnp.mean(z, axis=-1, keepdims=True)
    zc = z - mu
    var = jnp.mean(zc * zc, axis=-1, keepdims=True)
    return zc * lax.rsqrt(var + LN_EPS) * g + b


def _ada_kernel(c_ref, w_ref, b_ref, o_ref):
    cs = _silu(c_ref[...]).astype(BF16)
    o_ref[...] = _dot(cs, w_ref[...].astype(BF16)) + b_ref[...]


def _ada(c_all, w_ada, b_ada):
    R = c_all.shape[0]
    tn = 1024
    n_out = N_MOD * D_MODEL
    return pl.pallas_call(
        _ada_kernel,
        grid=(DEPTH, n_out // tn),
        in_specs=[
            pl.BlockSpec((R, D_MODEL), lambda l, j: (0, 0)),
            pl.BlockSpec((None, D_MODEL, tn), lambda l, j: (l, 0, j)),
            pl.BlockSpec((None, 1, tn), lambda l, j: (l, 0, j)),
        ],
        out_specs=pl.BlockSpec((None, R, tn), lambda l, j: (l, 0, j)),
        out_shape=jax.ShapeDtypeStruct((DEPTH, R, n_out), F32),
        compiler_params=_cparams(("parallel", "parallel")),
        name="ada_mod",
    )(c_all, w_ada, b_ada.reshape(DEPTH, 1, n_out))


def _inproj_kernel(*refs, has_tail, n_conv, n_split, tiles_per_row):
    refs = list(refs)
    x_ref, sc_ref, sh_ref, w_ref = refs[:4]
    del refs[:4]
    wt_ref = refs.pop(0) if has_tail else None
    cw_ref = refs.pop(0) if n_conv else None
    o_refs = [refs.pop(0) for _ in range(max(n_split, 1))]
    o_ref = o_refs[0]
    ot_ref = refs.pop(0) if has_tail else None
    h_scr = refs.pop(0)
    i = pl.program_id(0)
    j = pl.program_id(1)

    @pl.when(j == 0)
    def _():
        h = x_ref[...] * (1.0 + sc_ref[...]) + sh_ref[...]
        hb = h.astype(BF16)
        h_scr[...] = hb
        if has_tail:
            ot_ref[...] = _dot(hb, wt_ref[...])

    if n_split:
        for k, ok_ref in enumerate(o_refs):
            @pl.when(j == k)
            def _(ok_ref=ok_ref):
                ok_ref[...] = _dot(h_scr[...], w_ref[...]).astype(ok_ref.dtype)
        return
    if not n_conv:
        o_ref[...] = _dot(h_scr[...], w_ref[...]).astype(o_ref.dtype)
        return
    carry_scr, p_scr = refs
    tm, tn = o_ref.shape
    W = DN_CONV_W
    P0 = DN_CONV_PAD
    cw = CONV_COL_CHUNK

    def conv_tile(jj):
        row_start = (i % tiles_per_row) == 0
        prev = jnp.where(row_start, 0.0, carry_scr[jj])
        tails = []
        chunks = list(range(0, tn, cw))
        p_scr[0] = _dot(h_scr[...], w_ref[:, 0:cw])
        for k, c0 in enumerate(chunks):
            cs = slice(c0, c0 + cw)
            if k + 1 < len(chunks):
                p_scr[(k + 1) % 2] = _dot(h_scr[...], w_ref[:, c0 + cw:c0 + 2 * cw])
            p = p_scr[k % 2]
            y = p * cw_ref[W - 1:W, cs]
            for t in range(W - 1):
                y = y + pltpu.roll(p, W - 1 - t, 0) * cw_ref[t:t + 1, cs]
            o_ref[:, cs] = _silu(y).astype(o_ref.dtype)
            nfix = 2 * SUBLANES
            head = jnp.concatenate([prev[:, cs], p[:nfix]], axis=0)
            y0 = head[P0:] * cw_ref[W - 1:W, cs]
            for t in range(W - 1):
                y0 = y0 + head[P0 - (W - 1 - t):P0 + nfix - (W - 1 - t)] * cw_ref[t:t + 1, cs]
            o_ref[0:nfix, cs] = _silu(y0).astype(o_ref.dtype)
            tails.append(p[tm - P0:])
        carry_scr[jj] = jnp.concatenate(tails, axis=1)

    for jj in range(n_conv):
        pl.when(j == jj)(functools.partial(conv_tile, jj))

    @pl.when(j >= n_conv)
    def _():
        o_ref[...] = _dot(h_scr[...], w_ref[...]).astype(o_ref.dtype)


def _inproj(geo, x, mod, w, w_tail, tn, conv_w=None, conv_cols=0, tm=None, split=False):
    if tm is not None:
        geo = geo._replace(tm=tm)
    T, tm = geo.T, geo.tm
    N = w.shape[1]
    has_tail = w_tail is not None
    n_conv = conv_cols // tn
    in_specs = [
        pl.BlockSpec((tm, D_MODEL), lambda i, j: (i, 0)),
        _mod_spec(geo, 1),
        _mod_spec(geo, 0),
        pl.BlockSpec((D_MODEL, tn), lambda i, j: (0, j)),
    ]
    args = [x, mod, mod, w]
    n_split = N // tn if split else 0
    if split:
        assert not has_tail and not n_conv
        out_specs = [pl.BlockSpec((tm, tn), lambda i, j: (i, 0))] * n_split
        out_shape = [jax.ShapeDtypeStruct((T, tn), geo.act_dtype)] * n_split
    else:
        out_specs = [pl.BlockSpec((tm, tn), lambda i, j: (i, j))]
        out_shape = [jax.ShapeDtypeStruct((T, N), geo.act_dtype)]
    scratch = [pltpu.VMEM((tm, D_MODEL), BF16)]
    if has_tail:
        in_specs.append(pl.BlockSpec((D_MODEL, LANES), lambda i, j: (0, 0)))
        args.append(w_tail)
        out_specs.append(pl.BlockSpec((tm, LANES), lambda i, j: (i, 0)))
        out_shape.append(jax.ShapeDtypeStruct((T, LANES), F32))
    if n_conv:
        assert conv_cols == n_conv * tn and not geo.per_token
        in_specs.append(pl.BlockSpec((DN_CONV_W, tn), lambda i, j: (0, jnp.minimum(j, n_conv - 1))))
        args.append(conv_w)
        scratch += [pltpu.VMEM((n_conv, DN_CONV_PAD, tn), F32), pltpu.VMEM((2, tm, CONV_COL_CHUNK), F32)]
    res = pl.pallas_call(
        functools.partial(_inproj_kernel, has_tail=has_tail, n_conv=n_conv, n_split=n_split,
                          tiles_per_row=geo.L // tm),
        grid=(T // tm, N // tn),
        in_specs=in_specs,
        out_specs=out_specs,
        out_shape=out_shape,
        scratch_shapes=scratch,
        compiler_params=_cparams(("arbitrary", "arbitrary")),
        name="in_proj",
    )(*args)
    if split:
        return tuple(res), None
    return res if has_tail else (res[0], None)


def _conv_tail_kernel(x_ref, sc_ref, sh_ref, w_ref, o_ref):
    h = x_ref[...] * (1.0 + sc_ref[...]) + sh_ref[...]
    o_ref[...] = _dot(h.astype(BF16), w_ref[...])


def _conv_tail(geo, x, mod, w, n_cols, tn):
    B, L = geo.B, geo.L
    blocks_per_row = L // SUBLANES
    return pl.pallas_call(
        _conv_tail_kernel,
        grid=(n_cols // tn, B),
        in_specs=[
            pl.BlockSpec((SUBLANES, D_MODEL), lambda j, b: (b * blocks_per_row + blocks_per_row - 1, 0)),
            pl.BlockSpec((None, 1, D_MODEL), lambda j, b: (b, 0, 1)),
            pl.BlockSpec((None, 1, D_MODEL), lambda j, b: (b, 0, 0)),
            pl.BlockSpec((D_MODEL, tn), lambda j, b: (0, j)),
        ],
        out_specs=pl.BlockSpec((None, SUBLANES, tn), lambda j, b: (b, 0, j)),
        out_shape=jax.ShapeDtypeStruct((B, SUBLANES, n_cols), F32),
        compiler_params=_cparams(("parallel", "parallel")),
        name="conv_tail",
    )(x, mod, mod, w)


def _call_stacked(kern, prev, first_stacked_out, *, grid, in_specs, args, out_specs, out_shape,
                  scratch_shapes, name):
    n_in = len(args)
    aliases = {}
    body = kern
    if prev is not None:
        n_prev = len(prev)
        in_specs = list(in_specs) + [pl.BlockSpec(memory_space=pl.ANY)] * n_prev
        args = list(args) + list(prev)
        aliases = {n_in + k: first_stacked_out + k for k in range(n_prev)}

        def body(*refs):
            kern(*refs[:n_in], *refs[n_in + n_prev:])

    return pl.pallas_call(
        body, grid=grid, in_specs=in_specs, out_specs=out_specs, out_shape=out_shape,
        scratch_shapes=scratch_shapes, input_output_aliases=aliases,
        compiler_params=_cparams(("parallel", "arbitrary")), name=name,
    )(*args)


def _stack_fill(prev, layer_j, n_layers):
    if prev is None:
        return (layer_j, n_layers), n_layers, 0
    return None, None, layer_j


def _pad_rows(x, rows):
    if x.shape[0] == rows:
        return x
    return jnp.concatenate([x, jnp.zeros((rows - x.shape[0], x.shape[1]), x.dtype)], axis=0)


def _ret_kernel(*refs, c, rows, nchunk, has_state, fill):
    if has_state:
        (q_ref, k_ref, v_ref, g_ref, cos_ref, sin_ref, dintra_ref, qdec_ref, kdec_ref, sdec_ref, gn_ref,
         s0_ref, o_ref, s_out_ref, s_scr) = refs
    else:
        (q_ref, k_ref, v_ref, g_ref, cos_ref, sin_ref, dintra_ref, qdec_ref, kdec_ref, sdec_ref, gn_ref,
         o_ref, s_out_ref, s_scr) = refs
    i = pl.program_id(1)
    half = RET_DK // 2

    @pl.when(i == 0)
    def _():
        if has_state:
            s_scr[...] = s0_ref[...]
        else:
            s_scr[...] = jnp.zeros_like(s_scr)

    def rot(ref, r, h, cos, sin):
        x1 = ref[r, h * RET_DK:h * RET_DK + half].astype(F32)
        x2 = ref[r, h * RET_DK + half:(h + 1) * RET_DK].astype(F32)
        return jnp.concatenate([x1 * cos - x2 * sin, x2 * cos + x1 * sin], axis=1)

    for ci in range(nchunk):
        r = slice(ci * rows, (ci + 1) * rows)
        cos = cos_ref[r, :]
        sin = sin_ref[r, :]
        for h in range(RET_HEADS):
            vs = slice(h * RET_DV, (h + 1) * RET_DV)
            q = _pad_rows(rot(q_ref, r, h, cos, sin), c)
            k = _pad_rows(rot(k_ref, r, h, cos, sin), c) * (RET_DK ** -0.5)
            if rows == c:
                v = v_ref[r, vs].astype(BF16)
            else:
                v = _pad_rows(v_ref[r, vs].astype(F32), c).astype(BF16)
            s = s_scr[h]
            att = _dot_nt(q.astype(BF16), k.astype(BF16)) * dintra_ref[h]
            o = _dot(att.astype(BF16), v) + _dot((q * qdec_ref[h]).astype(BF16), s.astype(BF16))
            s_scr[h] = s * sdec_ref[h] + _dot_tn((k * kdec_ref[h]).astype(BF16), v)
            o = o[:rows]
            mu = jnp.mean(o, axis=-1, keepdims=True)
            oc = o - mu
            var = jnp.mean(oc * oc, axis=-1, keepdims=True)
            on = oc * lax.rsqrt(var + LN_EPS) * gn_ref[...]
            gate = g_ref[r, vs].astype(F32)
            o_ref[r, vs] = (on * _silu(gate)).astype(o_ref.dtype)

    @pl.when(i == pl.num_programs(1) - 1)
    def _():
        if fill is None:
            s_out_ref[...] = s_scr[...]
        else:
            for l in range(fill[1]):
                s_out_ref[l] = s_scr[...] if l == fill[0] else jnp.zeros(s_scr.shape, F32)


def _ret_tables(geo, chunk):
    c_true = min(chunk, geo.l_true)
    half = RET_DK // 2
    inv_freq = 10000.0 ** (-jnp.linspace(0.0, 1.0, half, dtype=F32))
    pos = (geo.pos0 + jnp.arange(geo.L)).astype(F32)
    ang = pos[:, None] * inv_freq[None, :]
    lg = jnp.log(1.0 - 2.0 ** (-5.0 - jnp.arange(RET_HEADS, dtype=F32)))
    idx = jnp.arange(c_true, dtype=F32)
    diff = idx[:, None] - idx[None, :]
    dintra = jnp.exp(jnp.where(diff[None] >= 0, diff[None] * lg[:, None, None], -jnp.inf))
    qdec = jnp.exp((idx + 1.0)[None, :] * lg[:, None])
    kdec = jnp.exp((c_true - 1.0 - idx)[None, :] * lg[:, None])
    sdec = jnp.exp(c_true * lg)
    pad = chunk - c_true
    dintra = jnp.pad(dintra, ((0, 0), (0, pad), (0, pad)))
    qdec = jnp.pad(qdec, ((0, 0), (0, pad)))[..., None]
    kdec = jnp.pad(kdec, ((0, 0), (0, pad)))[..., None]
    sdec = jnp.broadcast_to(sdec[:, None, None], (RET_HEADS, 1, RET_DV))
    return jnp.cos(ang), jnp.sin(ang), dintra, qdec, kdec, sdec


def _retention(geo, p, tables, gn_g, s_all, layer_j, prev, tl, chunk):
    B, L = geo.B, geo.L
    nblk = L // tl
    rows = min(tl, chunk)
    nchunk = tl // rows
    has_state = s_all is not None
    cos, sin, dintra, qdec, kdec, sdec = tables
    rb = lambda b, i: b * nblk + i
    full3 = lambda b, i: (0, 0, 0)
    if isinstance(p, tuple):
        qk, v, gate = p
        parts = [(qk, 0), (qk, 1), (v, 0), (gate, 0)]
    else:
        parts = [(p, 0), (p, 1), (p, 1), (p, 2)]
    col = [cb for _, cb in parts]
    in_specs = [
        pl.BlockSpec((tl, RET_QW), lambda b, i: (rb(b, i), col[0])),
        pl.BlockSpec((tl, RET_QW), lambda b, i: (rb(b, i), col[1])),
        pl.BlockSpec((tl, RET_VW), lambda b, i: (rb(b, i), col[2])),
        pl.BlockSpec((tl, RET_VW), lambda b, i: (rb(b, i), col[3])),
        pl.BlockSpec((tl, RET_DK // 2), lambda b, i: (i, 0)),
        pl.BlockSpec((tl, RET_DK // 2), lambda b, i: (i, 0)),
        pl.BlockSpec((RET_HEADS, chunk, chunk), full3),
        pl.BlockSpec((RET_HEADS, chunk, 1), full3),
        pl.BlockSpec((RET_HEADS, chunk, 1), full3),
        pl.BlockSpec((RET_HEADS, 1, RET_DV), full3),
        pl.BlockSpec((1, RET_DV), lambda b, i: (0, 0)),
    ]
    args = [a for a, _ in parts] + [cos, sin, dintra, qdec, kdec, sdec, gn_g.reshape(1, RET_DV)]
    if has_state:
        in_specs.append(pl.BlockSpec((None, None, RET_HEADS, RET_DK, RET_DV),
                                     lambda b, i: (layer_j, b, 0, 0, 0)))
        args.append(s_all)
    fill, lead, lidx = _stack_fill(prev, layer_j, N_RET_LAYERS)
    o, s = _call_stacked(
        functools.partial(_ret_kernel, c=chunk, rows=rows, nchunk=nchunk, has_state=has_state, fill=fill),
        prev, first_stacked_out=1,
        grid=(B, nblk),
        in_specs=in_specs,
        args=args,
        out_specs=[
            pl.BlockSpec((tl, RET_VW), lambda b, i: (rb(b, i), 0)),
            pl.BlockSpec((lead, None, RET_HEADS, RET_DK, RET_DV), lambda b, i: (lidx, b, 0, 0, 0)),
        ],
        out_shape=[
            jax.ShapeDtypeStruct((geo.T, RET_VW), geo.act_dtype),
            jax.ShapeDtypeStruct((N_RET_LAYERS, B, RET_HEADS, RET_DK, RET_DV), F32),
        ],
        scratch_shapes=[pltpu.VMEM((RET_HEADS, RET_DK, RET_DV), F32)],
        name="retention",
    )
    return o, (s,)


DN_CONV_PAD = SUBLANES
CONV_COL_CHUNK = 256


def _block_diag(x_cat, n, mask):
    return jnp.where(mask, jnp.concatenate([x_cat] * n, axis=0), 0.0)


def _dn_kernel(*refs, c, rows, nchunk, l_last, has_state, fill):
    if has_state:
        (qkv_ref, z_ref, ab_ref, cw_ref, alog_ref, dtb_ref, ng_ref, s0_ref, buf_ref,
         o_ref, s_out_ref, buf_out_ref, s_scr, xp_scr, y_scr) = refs
    else:
        (qkv_ref, z_ref, ab_ref, alog_ref, dtb_ref, ng_ref, o_ref, s_out_ref, s_scr) = refs
    i = pl.program_id(1)
    hq = min(DN_V_HEADS, MXU_DIM // c)
    tl = rows * nchunk
    npair = DN_V_HEADS // 2
    W = DN_CONV_W
    P0 = DN_CONV_PAD

    @pl.when(i == 0)
    def _():
        if has_state:
            for pr in range(npair):
                s_scr[pr] = jnp.concatenate([s0_ref[2 * pr], s0_ref[2 * pr + 1]], axis=1)
            xp_scr[P0 - (W - 1):P0, :] = buf_ref[...]
        else:
            s_scr[...] = jnp.zeros_like(s_scr)

    if has_state:
        xp_scr[P0:P0 + tl, :] = qkv_ref[...].astype(F32)
        y = xp_scr[P0 - (W - 1):P0 - (W - 1) + tl, :] * cw_ref[0:1, :]
        for t in range(1, W):
            y = y + xp_scr[P0 - (W - 1) + t:P0 - (W - 1) + t + tl, :] * cw_ref[t:t + 1, :]
        y_scr[0:tl, :] = _silu(y)
        if rows < c:
            y_scr[tl:, :] = jnp.zeros((y_scr.shape[0] - tl, DN_CONV_CH), F32)

        @pl.when(i == pl.num_programs(1) - 1)
        def _():
            tail = xp_scr[P0 + l_last - (W - 1):P0 + l_last, :]
            if fill is None:
                buf_out_ref[...] = tail
            else:
                for l in range(fill[1]):
                    buf_out_ref[l] = tail if l == fill[0] else jnp.zeros(tail.shape, F32)

        xp_scr[P0 - (W - 1):P0, :] = xp_scr[P0 + tl - (W - 1):P0 + tl, :]

        def ycols(r, lo, hi):
            return y_scr[r, lo:hi]
    else:
        def ycols(r, lo, hi):
            return qkv_ref[r, lo:hi].astype(F32)

    ri = lax.broadcasted_iota(jnp.int32, (c, c), 0)
    ci_ = lax.broadcasted_iota(jnp.int32, (c, c), 1)
    incl = ri >= ci_
    strict = ri > ci_
    eye = ri == ci_
    tri_f = incl.astype(F32)
    eye_cat = jnp.concatenate([eye.astype(F32)] * hq, axis=1)
    bi = lax.broadcasted_iota(jnp.int32, (hq * c, hq * c), 0) // c
    bj = lax.broadcasted_iota(jnp.int32, (hq * c, hq * c), 1) // c
    bd_mask = bi == bj
    lane2 = lax.broadcasted_iota(jnp.int32, (DN_DK, 2 * DN_DV), 1)
    left = lane2 < DN_DV
    padded = rows < c or l_last < rows
    row_ok = lax.broadcasted_iota(jnp.int32, (c, 1), 0) < l_last if padded else None
    n_levels = c.bit_length() - 1

    oi = lax.broadcasted_iota(jnp.int32, (2 * DN_DK, 2 * DN_DK), 0) // DN_DK
    oj = lax.broadcasted_iota(jnp.int32, (2 * DN_DK, 2 * DN_DK), 1) // DN_DK
    ones_bd = jnp.where(oi == oj, 1.0, 0.0).astype(BF16)

    def sumsq2(x):
        if has_state:
            parts = [jnp.broadcast_to(jnp.sum(x[:, lo:lo + DN_DK] * x[:, lo:lo + DN_DK], axis=-1, keepdims=True),
                                      (x.shape[0], DN_DK)) for lo in (0, DN_DK)]
            return jnp.concatenate(parts, axis=1)
        return _dot((x * x).astype(BF16), ones_bd)

    nquad = DN_V_HEADS // hq
    quads = [[hq * q + t for t in range(hq)] for q in range(nquad)]

    def phase1(ck):
        r0 = ck * c if rows == c else 0
        r = slice(r0, r0 + c)
        ab = _pad_rows(ab_ref[ck * rows:(ck + 1) * rows, :], c)
        sp = ab + dtb_ref[...]
        g_full = -jnp.exp(alog_ref[...]) * (jnp.maximum(sp, 0.0) + jnp.log1p(jnp.exp(-jnp.abs(sp))))
        beta_full = jax.nn.sigmoid(ab)
        if row_ok is not None:
            g_full = jnp.where(row_ok, g_full, 0.0)
            beta_full = jnp.where(row_ok, beta_full, 0.0)
        G_full = jnp.dot(tri_f, g_full, preferred_element_type=F32, precision=lax.Precision.HIGHEST)
        yield

        kn, qn, KK, QK = {}, {}, {}, {}
        for j in range(0, DN_QK_HEADS, 2):
            q2 = ycols(r, j * DN_DK, (j + 2) * DN_DK)
            k2 = ycols(r, DN_QW + j * DN_DK, DN_QW + (j + 2) * DN_DK)
            q2 = q2 * lax.rsqrt(sumsq2(q2) + 1e-6) * (DN_DK ** -0.5)
            k2 = k2 * lax.rsqrt(sumsq2(k2) + 1e-6)
            qn[j], qn[j + 1] = q2[:, :DN_DK], q2[:, DN_DK:]
            kn[j], kn[j + 1] = k2[:, :DN_DK], k2[:, DN_DK:]
        yield
        for j in range(DN_QK_HEADS):
            kb16 = kn[j].astype(BF16)
            kq = _dot_nt(jnp.concatenate([kb16, qn[j].astype(BF16)], axis=0), kb16)
            KK[j], QK[j] = kq[:c], kq[c:]
        yield

        A, att, eG, Gc, rhs = {}, {}, {}, {}, {}
        for h in range(DN_V_HEADS):
            j = h // 2
            Gc[h] = jnp.broadcast_to(G_full[:, h:h + 1], (c, DN_DV))
            bc = jnp.broadcast_to(beta_full[:, DN_V_HEADS + h:DN_V_HEADS + h + 1], (c, DN_DV))
            Gcc = Gc[h][:, :c]
            Gr = jnp.sum(jnp.where(eye, Gcc, 0.0), axis=0, keepdims=True)
            dec = jnp.exp(jnp.where(incl, Gcc - Gr, -jnp.inf))
            A[h] = jnp.where(strict, bc[:, :c] * KK[j] * dec, 0.0)
            att[h] = QK[j] * dec
            eG[h] = jnp.exp(Gc[h])
            vh = ycols(r, 2 * DN_QW + h * DN_DV, 2 * DN_QW + (h + 1) * DN_DV)
            rhs[h] = jnp.concatenate([vh * bc, kn[j] * (bc * eG[h])], axis=1)
        yield

        X = [jnp.concatenate([A[h] for h in hs], axis=1) for hs in quads]
        Pm = [eye_cat - x for x in X]
        X = [_dot(x.astype(BF16), _block_diag(x, hq, bd_mask).astype(BF16)) for x in X]
        yield
        for lvl in range(1, n_levels):
            Xw = [_block_diag(x, hq, bd_mask).astype(BF16) for x in X]
            if lvl < n_levels - 1:
                both = [_dot(jnp.concatenate([X[q], Pm[q]], axis=0).astype(BF16), Xw[q]) for q in range(nquad)]
                X = [b[:c] for b in both]
                Pm = [Pm[q] + both[q][c:] for q in range(nquad)]
            else:
                Pm = [Pm[q] + _dot(Pm[q].astype(BF16), Xw[q]) for q in range(nquad)]
            yield
        sol = [_dot(_block_diag(Pm[q], hq, bd_mask).astype(BF16),
                    jnp.concatenate([rhs[h] for h in quads[q]], axis=0).astype(BF16))
               for q in range(nquad)]
        att_bd = [_block_diag(jnp.concatenate([att[h] for h in quads[q]], axis=1), hq, bd_mask).astype(BF16)
                  for q in range(nquad)]
        return kn, qn, Gc, eG, sol, att_bd

    def phase2(ck, pre):
        kn, qn, Gc, eG, sol, att_bd = pre
        sol_h = {h: sol[h // hq][c * (h % hq):c * (h % hq + 1)] for h in range(DN_V_HEADS)}
        u, oS = {}, {}
        for pr in range(npair):
            h0, h1 = 2 * pr, 2 * pr + 1
            S = s_scr[pr]
            Sbd = jnp.concatenate([jnp.where(left, S, 0.0), jnp.where(left, 0.0, S)], axis=0).astype(BF16)
            w_cat = jnp.concatenate([sol_h[h0][:, DN_DV:], sol_h[h1][:, DN_DV:]], axis=1)
            qe_cat = jnp.concatenate([qn[pr] * eG[h0], qn[pr] * eG[h1]], axis=1)
            R = _dot(jnp.concatenate([w_cat, qe_cat], axis=0).astype(BF16), Sbd)
            u[h0] = sol_h[h0][:, :DN_DV] - R[:c, :DN_DV]
            u[h1] = sol_h[h1][:, :DN_DV] - R[:c, DN_DV:]
            oS[h0] = R[c:, :DN_DV]
            oS[h1] = R[c:, DN_DV:]
        yield

        o_intra = [_dot(att_bd[q], jnp.concatenate([u[h] for h in quads[q]], axis=0).astype(BF16))
                   for q in range(nquad)]
        yield

        for pr in range(npair):
            h0, h1 = 2 * pr, 2 * pr + 1
            gl0 = Gc[h0][c - 1:c, :]
            gl1 = Gc[h1][c - 1:c, :]
            du = jnp.concatenate([u[h0] * jnp.exp(gl0 - Gc[h0]), u[h1] * jnp.exp(gl1 - Gc[h1])], axis=1)
            sdec = jnp.concatenate([jnp.exp(gl0), jnp.exp(gl1)], axis=1)
            s_scr[pr] = s_scr[pr] * sdec + _dot_tn(kn[pr].astype(BF16), du.astype(BF16))
        yield

        zr = slice(ck * rows, (ck + 1) * rows)
        ng2 = jnp.concatenate([ng_ref[...], ng_ref[...]], axis=1)
        for h in range(0, DN_V_HEADS, 2):
            o = jnp.concatenate([(oS[hh] + o_intra[hh // hq][c * (hh % hq):c * (hh % hq + 1)])[:rows]
                                 for hh in (h, h + 1)], axis=1)
            o = o * lax.rsqrt(sumsq2(o) * (1.0 / DN_DV) + 1e-6) * ng2
            hs = slice(h * DN_DV, (h + 2) * DN_DV)
            z = z_ref[zr, hs].astype(F32)
            o_ref[zr, hs] = (o * _silu(z)).astype(o_ref.dtype)

    def lockstep(gens):
        out = {}
        while gens:
            for key in list(gens):
                try:
                    next(gens[key])
                except StopIteration as done:
                    out[key] = done.value
                    del gens[key]
        return out

    def recurrence(group, pre):
        for ck in group:
            yield from phase2(ck, pre[ck])

    groups = [list(range(g0, min(g0 + DN_LOCKSTEP, nchunk))) for g0 in range(0, nchunk, DN_LOCKSTEP)]
    pre = lockstep({ck: phase1(ck) for ck in groups[0]})
    for gi, group in enumerate(groups):
        gens = {"recurrence": recurrence(group, pre)}
        if gi + 1 < len(groups):
            gens.update({ck: phase1(ck) for ck in groups[gi + 1]})
        pre = lockstep(gens)

    @pl.when(i == pl.num_programs(1) - 1)
    def _():
        for pr in range(npair):
            for l in range(1 if fill is None else fill[1]):
                mine = fill is None or l == fill[0]
                dst = s_out_ref if fill is None else s_out_ref.at[l]
                dst[2 * pr] = s_scr[pr][:, :DN_DV] if mine else jnp.zeros((DN_DK, DN_DV), F32)
                dst[2 * pr + 1] = s_scr[pr][:, DN_DV:] if mine else jnp.zeros((DN_DK, DN_DV), F32)


def _deltanet(geo, p, ab, conv_w, a_log, dt_bias, norm_g, s_all, buf_all, layer_j, prev, tl, chunk):
    B, L = geo.B, geo.L
    nblk = L // tl
    rows = min(tl, chunk)
    nchunk = tl // rows
    l_last = geo.l_true - (nblk - 1) * tl
    assert l_last == tl or (nblk == 1 and nchunk == 1)
    has_state = s_all is not None
    rb = lambda b, i: b * nblk + i
    pad16 = lambda v: jnp.pad(v.reshape(1, DN_V_HEADS), ((0, 0), (0, LANES - DN_V_HEADS)))
    row1 = lambda n: pl.BlockSpec((1, n), lambda b, i: (0, 0))
    in_specs = [
        pl.BlockSpec((tl, DN_CONV_CH), lambda b, i: (rb(b, i), 0)),
        pl.BlockSpec((tl, DN_VW), lambda b, i: (rb(b, i), DN_CONV_CH // DN_VW)),
        pl.BlockSpec((tl, LANES), lambda b, i: (rb(b, i), 0)),
    ]
    args = [p, p, ab]
    if has_state:
        in_specs.append(pl.BlockSpec((DN_CONV_W, DN_CONV_CH), lambda b, i: (0, 0)))
        args.append(conv_w)
    in_specs += [row1(LANES), row1(LANES), row1(DN_DV)]
    args += [pad16(a_log), pad16(dt_bias), norm_g.reshape(1, DN_DV)]
    fill, lead, lidx = _stack_fill(prev, layer_j, N_DN_LAYERS)
    out_specs = [
        pl.BlockSpec((tl, DN_VW), lambda b, i: (rb(b, i), 0)),
        pl.BlockSpec((lead, None, DN_V_HEADS, DN_DK, DN_DV), lambda b, i: (lidx, b, 0, 0, 0)),
    ]
    out_shape = [
        jax.ShapeDtypeStruct((geo.T, DN_VW), geo.act_dtype),
        jax.ShapeDtypeStruct((N_DN_LAYERS, B, DN_V_HEADS, DN_DK, DN_DV), F32),
    ]
    scratch = [pltpu.VMEM((DN_V_HEADS // 2, DN_DK, 2 * DN_DV), F32)]
    if has_state:
        in_specs += [
            pl.BlockSpec((None, None, DN_V_HEADS, DN_DK, DN_DV), lambda b, i: (layer_j, b, 0, 0, 0)),
            pl.BlockSpec((None, None, DN_CONV_W - 1, DN_CONV_CH), lambda b, i: (layer_j, b, 0, 0)),
        ]
        args += [s_all, buf_all]
        out_specs.append(pl.BlockSpec((lead, None, DN_CONV_W - 1, DN_CONV_CH), lambda b, i: (lidx, b, 0, 0)))
        out_shape.append(jax.ShapeDtypeStruct((N_DN_LAYERS, B, DN_CONV_W - 1, DN_CONV_CH), F32))
        scratch += [pltpu.VMEM((DN_CONV_PAD + tl, DN_CONV_CH), F32),
                    pltpu.VMEM((max(tl, chunk), DN_CONV_CH), F32)]
    kern = functools.partial(_dn_kernel, c=chunk, rows=rows, nchunk=nchunk, l_last=l_last, has_state=has_state,
                             fill=fill)
    res = _call_stacked(kern, prev, first_stacked_out=1, grid=(B, nblk), in_specs=in_specs, args=args,
                        out_specs=out_specs, out_shape=out_shape, scratch_shapes=scratch, name="deltanet")
    return res[0], tuple(res[1:])


def _outproj_kernel(o_ref, w_ref, x_ref, gate_ref, g_ref, b_ref, out_ref):
    y = _dot(o_ref[...].astype(BF16), w_ref[...])
    z = ALPHA * x_ref[...] + (1.0 + gate_ref[...]) * y
    out_ref[...] = _layer_norm(z, g_ref[...], b_ref[...])


def _outproj_ln(geo, o, w_out, x, mod, ln_g, ln_b):
    T, tm = geo.T, geo.tm
    K = w_out.shape[0]
    vec = pl.BlockSpec((1, D_MODEL), lambda i: (0, 0))
    return pl.pallas_call(
        _outproj_kernel,
        grid=(T // tm,),
        in_specs=[
            pl.BlockSpec((tm, K), lambda i: (i, 0)),
            pl.BlockSpec((K, D_MODEL), lambda i: (0, 0)),
            pl.BlockSpec((tm, D_MODEL), lambda i: (i, 0)),
            _mod_spec(geo, 2),
            vec, vec,
        ],
        out_specs=pl.BlockSpec((tm, D_MODEL), lambda i: (i, 0)),
        out_shape=jax.ShapeDtypeStruct((T, D_MODEL), F32),
        compiler_params=_cparams(("parallel",)),
        name="out_proj_ln",
    )(o, w_out, x, mod, ln_g.reshape(1, D_MODEL), ln_b.reshape(1, D_MODEL))


MOE_ROWS = 144


def _moe_blocks(tm):
    nb = (tm - N_GROUPS) // MOE_ROWS + N_GROUPS
    ns = -(-(nb * MOE_ROWS) // LANES) * LANES
    return nb, ns


def _first_max(v, row, valid):
    m = jnp.max(jnp.where(valid, v, -jnp.inf), axis=0, keepdims=True)
    idx = jnp.min(jnp.where(valid & (v == m), row, float(N_EXPERTS)), axis=0, keepdims=True)
    return m, idx


def _route(h, wrt, br):
    h_hi = h.astype(BF16)
    h_lo = (h - h_hi.astype(F32)).astype(BF16)
    w_hi = wrt.astype(BF16)
    w_lo = (wrt - w_hi.astype(F32)).astype(BF16)
    ww = _dot_nt(jnp.concatenate([w_hi, w_lo], axis=0), h_hi)
    logits = ww[:N_EXPERTS] + ww[N_EXPERTS:] + _dot_nt(w_hi, h_lo)
    mx = jnp.max(logits, axis=0, keepdims=True)
    ex = jnp.exp(logits - mx)
    probs = ex / jnp.sum(ex, axis=0, keepdims=True)
    sel = probs + br
    row_i = lax.broadcasted_iota(jnp.int32, sel.shape, 0)
    grp = (row_i // EXPERTS_PER_GROUP).astype(F32)
    row = row_i.astype(F32)
    best = None
    gsel = None
    for g in range(N_GROUPS):
        in_g = grp == g
        m1, i1 = _first_max(sel, row, in_g)
        m2, _ = _first_max(sel, row, in_g & (row != i1))
        score = m1 + m2
        if g == 0:
            best, gsel = score, jnp.zeros_like(i1)
        else:
            better = score > best
            gsel = jnp.where(better, float(g), gsel)
            best = jnp.where(better, score, best)
    in_grp = grp == gsel
    _, i1 = _first_max(sel, row, in_grp)
    _, i2 = _first_max(sel, row, in_grp & (row != i1))
    w1 = jnp.sum(jnp.where(row == i1, probs, 0.0), axis=0, keepdims=True)
    w2 = jnp.sum(jnp.where(row == i2, probs, 0.0), axis=0, keepdims=True)
    wsum = w1 + w2
    gates = jnp.where(row == i1, w1 / wsum, 0.0) + jnp.where(row == i2, w2 / wsum, 0.0)
    return gsel, gates


def _router_kernel(x_ref, sc_ref, sh_ref, wrt_ref, br_ref, posr_ref, gates_ref, tbl_ref):
    tm = x_ref.shape[0]
    h = x_ref[...] * (1.0 + sc_ref[...]) + sh_ref[...]
    gsel, gates = _route(h, wrt_ref[...], br_ref[...])
    gates_ref[...] = gates
    grow = lax.broadcasted_iota(jnp.int32, (SUBLANES, tm), 0).astype(F32)
    og = grow == gsel
    ti = lax.broadcasted_iota(jnp.int32, (tm, tm), 0)
    tj = lax.broadcasted_iota(jnp.int32, (tm, tm), 1)
    before = jnp.where(ti < tj, 1.0, 0.0).astype(BF16)
    ogf = jnp.where(og, 1.0, 0.0)
    rank = _dot(ogf.astype(BF16), before)
    cnt = jnp.sum(ogf, axis=1, keepdims=True)
    nblk = jnp.zeros_like(cnt)
    for kb in range(-(-tm // MOE_ROWS)):
        nblk = nblk + jnp.where(cnt > float(kb * MOE_ROWS), 1.0, 0.0)
    seg = nblk * float(MOE_ROWS)
    block_start = lax.broadcasted_iota(jnp.int32, (1, LANES), 1).astype(F32) * float(MOE_ROWS)
    off = jnp.zeros((1, 1), F32)
    pos = jnp.zeros((1, tm), F32)
    tbl = jnp.zeros((1, LANES), F32)
    for g in range(N_GROUPS):
        pos = pos + jnp.where(gsel == float(g), off + rank[g:g + 1, :], 0.0)
        off = off + seg[g:g + 1, :]
        tbl = tbl + jnp.where(block_start >= off, 1.0, 0.0)
    posr_ref[...] = pos
    tbl_ref[...] = tbl.astype(jnp.int32)


def _expert_kernel(tbl_ref, x_ref, sc_ref, sh_ref, gt_ref, posr_ref, gates_ref, wgu_ref, wd_ref,
                   g_ref, b_ref, out_ref, hs_scr, gs_scr, ys_scr, p_scr, *, nb):
    i = pl.program_id(0)
    tm = x_ref.shape[0]
    ns = hs_scr.shape[0]
    R = MOE_ROWS
    x = x_ref[...]
    hb = (x * (1.0 + sc_ref[...]) + sh_ref[...]).astype(BF16)
    slot = lax.broadcasted_iota(jnp.int32, (ns, tm), 0).astype(F32)
    P = jnp.where(slot == posr_ref[...], 1.0, 0.0).astype(BF16)
    p_scr[...] = P
    hs_scr[...] = _dot(P, hb).astype(BF16)
    gates = gates_ref[...]
    g_hi = gates.astype(BF16)
    g_lo = (gates - g_hi.astype(F32)).astype(BF16)
    gs_scr[...] = _dot_nt(P, g_hi) + _dot_nt(P, g_lo)
    if nb * R < ns:
        ys_scr[nb * R:, :] = jnp.zeros((ns - nb * R, D_MODEL), BF16)

    for b in range(nb):
        rs = slice(b * R, (b + 1) * R)
        g = tbl_ref[i, b]

        @pl.when(g < N_GROUPS)
        def _():
            gsb = gs_scr[rs, :]
            lane = lax.broadcasted_iota(jnp.int32, gsb.shape, 1)
            hsb = hs_scr[rs, :]
            acts = []
            for e in range(EXPERTS_PER_GROUP):
                gu = _dot(hsb, wgu_ref[g, e])
                gate = jnp.sum(jnp.where(lane == g * EXPERTS_PER_GROUP + e, gsb, 0.0), axis=1, keepdims=True)
                act = _silu(gu[:, :D_EXPERT]) * gu[:, D_EXPERT:] * gate
                acts.append(act.astype(BF16))
            ys_scr[rs, :] = _dot(jnp.concatenate(acts, axis=1), wd_ref[g]).astype(BF16)

        @pl.when(g >= N_GROUPS)
        def _():
            ys_scr[rs, :] = jnp.zeros((R, D_MODEL), BF16)

    y = _dot_tn(p_scr[...], ys_scr[...])
    z = ALPHA * x + (1.0 + gt_ref[...]) * y
    out_ref[...] = _layer_norm(z, g_ref[...], b_ref[...])


def _moe_ln(geo, x, mod, w_router_t, b_router, w_gu, w_down, layer, ln_g, ln_b):
    T, tm = geo.T, geo.tm
    nt = T // tm
    nb, ns = _moe_blocks(tm)
    posr, gates, tbl = pl.pallas_call(
        _router_kernel,
        grid=(nt,),
        in_specs=[
            pl.BlockSpec((tm, D_MODEL), lambda i: (i, 0)),
            _mod_spec(geo, 4),
            _mod_spec(geo, 3),
            pl.BlockSpec((N_EXPERTS, D_MODEL), lambda i: (0, 0)),
            pl.BlockSpec((N_EXPERTS, 1), lambda i: (0, 0)),
        ],
        out_specs=[
            pl.BlockSpec((None, 1, tm), lambda i: (i, 0, 0)),
            pl.BlockSpec((N_EXPERTS, tm), lambda i: (0, i)),
            pl.BlockSpec((None, 1, LANES), lambda i: (i, 0, 0)),
        ],
        out_shape=[
            jax.ShapeDtypeStruct((nt, 1, tm), F32),
            jax.ShapeDtypeStruct((N_EXPERTS, T), F32),
            jax.ShapeDtypeStruct((nt, 1, LANES), jnp.int32),
        ],
        compiler_params=_cparams(("parallel",)),
        name="moe_router",
    )(x, mod, mod, w_router_t, b_router.reshape(N_EXPERTS, 1))

    vec = pl.BlockSpec((1, D_MODEL), lambda i, t: (0, 0))
    resident = dict(pipeline_mode=pl.Buffered(1))
    grid_spec = pltpu.PrefetchScalarGridSpec(
        num_scalar_prefetch=1,
        grid=(nt,),
        in_specs=[
            pl.BlockSpec((tm, D_MODEL), lambda i, t: (i, 0)),
            _mod_spec(geo, 4),
            _mod_spec(geo, 3),
            _mod_spec(geo, 5),
            pl.BlockSpec((None, 1, tm), lambda i, t: (i, 0, 0)),
            pl.BlockSpec((N_EXPERTS, tm), lambda i, t: (0, i)),
            pl.BlockSpec((None, N_GROUPS, EXPERTS_PER_GROUP, D_MODEL, 2 * D_EXPERT),
                         lambda i, t: (layer, 0, 0, 0, 0), **resident),
            pl.BlockSpec((None, N_GROUPS, EXPERTS_PER_GROUP * D_EXPERT, D_MODEL),
                         lambda i, t: (layer, 0, 0, 0), **resident),
            vec, vec,
        ],
        out_specs=pl.BlockSpec((tm, D_MODEL), lambda i, t: (i, 0)),
        scratch_shapes=[
            pltpu.VMEM((ns, D_MODEL), BF16),
            pltpu.VMEM((ns, N_EXPERTS), F32),
            pltpu.VMEM((ns, D_MODEL), BF16),
            pltpu.VMEM((ns, tm), BF16),
        ],
    )
    return pl.pallas_call(
        functools.partial(_expert_kernel, nb=nb),
        grid_spec=grid_spec,
        out_shape=jax.ShapeDtypeStruct((T, D_MODEL), F32),
        compiler_params=_cparams(("parallel",)),
        name="moe_experts_ln",
    )(tbl.reshape(nt, LANES), x, mod, mod, mod, posr, gates, w_gu, w_down,
      ln_g.reshape(1, D_MODEL), ln_b.reshape(1, D_MODEL))


def _trunk(geo, x, mods, s_ret, s_dn, s_conv, wts, ret_tl, ret_chunk, dn_tl, dn_chunk):
    (ln_g, ln_b, ret_w_in, ret_gn_g, ret_w_out, dn_w_main, dn_w_tail, dn_conv_w, dn_a_log, dn_dt_bias,
     dn_norm_g, dn_w_out, router_w, router_b, moe_w_gu, moe_w_down) = wts
    tables = _ret_tables(geo, ret_chunk)
    decode = s_dn is not None
    proj_tm = None if geo.per_token or geo.L % 1024 else 1024
    ret_stack, dn_stack, conv_tails = None, None, []
    for i in range(DEPTH):
        if geo.per_token:
            mod = jnp.repeat(mods[i], geo.L, axis=0)
        else:
            mod = mods[i].reshape(geo.B, 1, N_MOD * D_MODEL)
        j = i // 2
        if i % 2 == 0:
            if decode:
                p, _ = _inproj(geo, x, mod, ret_w_in[j], None, tn=1536)
            else:
                p, _ = _inproj(geo, x, mod, ret_w_in[j], None, tn=RET_VW, tm=proj_tm, split=True)
            o, ret_stack = _retention(geo, p, tables, ret_gn_g[j], s_ret, j, ret_stack, ret_tl, ret_chunk)
            w_out = ret_w_out[j]
        else:
            if decode:
                p, ab = _inproj(geo, x, mod, dn_w_main[j], dn_w_tail[j], tn=1536)
            else:
                p, ab = _inproj(geo, x, mod, dn_w_main[j], dn_w_tail[j], tn=2048,
                                conv_w=dn_conv_w[j], conv_cols=DN_CONV_CH, tm=proj_tm)
                conv_tails.append(_conv_tail(geo, x, mod, dn_w_main[j], DN_CONV_CH, tn=2048))
            o, dn_stack = _deltanet(geo, p, ab, dn_conv_w[j], dn_a_log[j], dn_dt_bias[j], dn_norm_g[j],
                                    s_dn, s_conv, j, dn_stack, dn_tl, dn_chunk)
            w_out = dn_w_out[j]
        x = _outproj_ln(geo, o, w_out, x, mod, ln_g[i, 0], ln_b[i, 0])
        x = _moe_ln(geo, x, mod, router_w, router_b, moe_w_gu, moe_w_down, i, ln_g[i, 1], ln_b[i, 1])
    if decode:
        new_dn, new_conv = dn_stack
    else:
        new_dn = dn_stack[0]
        new_conv = jnp.stack(conv_tails)[:, :, SUBLANES - (DN_CONV_W - 1):]
    return x, ret_stack[0], new_dn, new_conv


def _run(x_prompt, x_sample, state_ret, state_dn, state_conv, c_prompt, c_sample, w_ada, b_ada, ln_g, ln_b,
         ret_w_in, ret_gn_g, ret_w_out, dn_w_in, dn_conv_w, dn_a_log, dn_dt_bias, dn_norm_g, dn_w_out,
         router_w, router_b, moe_w_gu, moe_w_down, *, past_len):
    Bp, Lp, _ = x_prompt.shape
    Bs, Ls, _ = x_sample.shape
    Ls_pad = -(-Ls // SUBLANES) * SUBLANES
    tm_p = min(512, Lp)
    geo_p = _Geo(Bp, Lp, Lp, 0, tm_p, False, BF16)
    Ts = Bs * Ls_pad
    geo_s = _Geo(Bs, Ls_pad, Ls, past_len, min(512, Ts), True, F32)

    dn_w_tail = jnp.pad(dn_w_in[:, :, DN_MAIN:], ((0, 0), (0, 0), (0, LANES - 2 * DN_V_HEADS))).astype(BF16)
    w_gu_grp = moe_w_gu.astype(BF16).reshape(DEPTH, N_GROUPS, EXPERTS_PER_GROUP, D_MODEL, 2 * D_EXPERT)
    w_down_grp = moe_w_down.astype(BF16).reshape(DEPTH, N_GROUPS, EXPERTS_PER_GROUP * D_EXPERT, D_MODEL)
    wts = (ln_g, ln_b, ret_w_in.astype(BF16), ret_gn_g, ret_w_out.astype(BF16),
           dn_w_in[:, :, :DN_MAIN].astype(BF16), dn_w_tail, dn_conv_w, dn_a_log, dn_dt_bias, dn_norm_g,
           dn_w_out.astype(BF16), router_w.T, router_b, w_gu_grp, w_down_grp)

    mods = _ada(jnp.concatenate([c_prompt, c_sample], axis=0), w_ada, b_ada)

    xp = x_prompt.reshape(Bp * Lp, D_MODEL)
    yp, ret_p, dn_p, conv_p = _trunk(geo_p, xp, mods[:, :Bp], None, None, None, wts,
                                     ret_tl=min(512, Lp), ret_chunk=min(RET_CHUNK, Lp),
                                     dn_tl=min(256, Lp), dn_chunk=DN_CHUNK)
    xs = jnp.pad(x_sample, ((0, 0), (0, Ls_pad - Ls), (0, 0))).reshape(Ts, D_MODEL)
    ys, ret_s, dn_s, conv_s = _trunk(geo_s, xs, mods[:, Bp:], state_ret, state_dn, state_conv, wts,
                                     ret_tl=Ls_pad, ret_chunk=max(Ls_pad, RET_DECODE_CHUNK),
                                     dn_tl=Ls_pad, dn_chunk=max(Ls_pad, DN_DECODE_CHUNK))
    y_prompt = yp.reshape(Bp, Lp, D_MODEL)
    y_sample = ys.reshape(Bs, Ls_pad, D_MODEL)[:, :Ls]
    return (y_prompt, y_sample, ret_p, ret_s, dn_p, dn_s, conv_p, conv_s)


def kernel(x_prompt, x_sample, state_ret, state_dn, state_conv, c_prompt, c_sample, w_ada, b_ada, ln_g, ln_b,
           ret_w_in, ret_gn_g, ret_w_out, dn_w_in, dn_conv_w, dn_a_log, dn_dt_bias, dn_norm_g, dn_w_out,
           router_w, router_b, moe_w_gu, moe_w_down):
    return _run(x_prompt, x_sample, state_ret, state_dn, state_conv, c_prompt, c_sample, w_ada, b_ada,
                ln_g, ln_b, ret_w_in, ret_gn_g, ret_w_out, dn_w_in, dn_conv_w, dn_a_log, dn_dt_bias,
                dn_norm_g, dn_w_out, router_w, router_b, moe_w_gu, moe_w_down, past_len=PAST_LEN)
```

```python
import functools
from typing import NamedTuple

import jax
import jax.numpy as jnp
from jax import lax
from jax.experimental import pallas as pl
from jax.experimental.pallas import tpu as pltpu

F32 = jnp.float32
BF16 = jnp.bfloat16

D_MODEL = 1024
DEPTH = 4
N_RET_LAYERS = (DEPTH + 1) // 2
N_DN_LAYERS = DEPTH // 2
PAST_LEN = 16384
RET_HEADS = 4
RET_DK = 256
RET_DV = 512
RET_QW = RET_HEADS * RET_DK
RET_VW = RET_HEADS * RET_DV
RET_CHUNK = 256
RET_DECODE_CHUNK = 16
DN_QK_HEADS = 8
DN_V_HEADS = 16
DN_DK = 128
DN_DV = 128
DN_QW = DN_QK_HEADS * DN_DK
DN_VW = DN_V_HEADS * DN_DV
DN_CONV_CH = 2 * DN_QW + DN_VW
DN_CONV_W = 4
DN_CHUNK = 64
DN_DECODE_CHUNK = 8
DN_LOCKSTEP = 2
DN_MAIN = DN_CONV_CH + DN_VW
N_EXPERTS = 16
N_GROUPS = 4
EXPERTS_PER_GROUP = N_EXPERTS // N_GROUPS
D_EXPERT = 256
N_MOD = 6
ALPHA = (2.0 * DEPTH) ** 0.25
LN_EPS = 1e-5

LANES = 128
SUBLANES = 8
MXU_DIM = 256
VMEM_LIMIT = 56 * 1024 * 1024


class _Geo(NamedTuple):
    B: int
    L: int
    l_true: int
    pos0: int
    tm: int
    per_token: bool
    act_dtype: object

    @property
    def T(self):
        return self.B * self.L


def _cparams(sem):
    return pltpu.CompilerParams(dimension_semantics=sem, vmem_limit_bytes=VMEM_LIMIT)


def _mod_spec(geo, chunk):
    if geo.per_token:
        return pl.BlockSpec((geo.tm, D_MODEL), lambda *g: (g[0], chunk))
    tpr = geo.L // geo.tm
    return pl.BlockSpec((None, 1, D_MODEL), lambda *g: (g[0] // tpr, 0, chunk))


def _silu(x):
    hx = 0.5 * x
    return hx + hx * jnp.tanh(hx)


def _dot(a, b):
    return jnp.dot(a, b, preferred_element_type=F32)


def _dot_nt(a, b):
    return lax.dot_general(a, b, (((1,), (1,)), ((), ())), preferred_element_type=F32)


def _dot_tn(a, b):
    return lax.dot_general(a, b, (((0,), (0,)), ((), ())), preferred_element_type=F32)


def _layer_norm(z, g, b):
    mu = jnp.mean(z, axis=-1, keepdims=True)
    zc = z - mu
    var = jnp.mean(zc * zc, axis=-1, keepdims=True)
    return zc * lax.rsqrt(var + LN_EPS) * g + b


def _ada_kernel(c_ref, w_ref, b_ref, o_ref):
    cs = _silu(c_ref[...]).astype(BF16)
    o_ref[...] = _dot(cs, w_ref[...].astype(BF16)) + b_ref[...]


def _ada(c_all, w_ada, b_ada):
    R = c_all.shape[0]
    tn = 1024
    n_out = N_MOD * D_MODEL
    return pl.pallas_call(
        _ada_kernel,
        grid=(DEPTH, n_out // tn),
        in_specs=[
            pl.BlockSpec((R, D_MODEL), lambda l, j: (0, 0)),
            pl.BlockSpec((None, D_MODEL, tn), lambda l, j: (l, 0, j)),
            pl.BlockSpec((None, 1, tn), lambda l, j: (l, 0, j)),
        ],
        out_specs=pl.BlockSpec((None, R, tn), lambda l, j: (l, 0, j)),
        out_shape=jax.ShapeDtypeStruct((DEPTH, R, n_out), F32),
        compiler_params=_cparams(("parallel", "parallel")),
        name="ada_mod",
    )(c_all, w_ada, b_ada.reshape(DEPTH, 1, n_out))


def _inproj_kernel(*refs, has_tail, n_conv, n_split, tiles_per_row):
    refs = list(refs)
    x_ref, sc_ref, sh_ref, w_ref = refs[:4]
    del refs[:4]
    wt_ref = refs.pop(0) if has_tail else None
    cw_ref = refs.pop(0) if n_conv else None
    o_refs = [refs.pop(0) for _ in range(max(n_split, 1))]
    o_ref = o_refs[0]
    ot_ref = refs.pop(0) if has_tail else None
    h_scr = refs.pop(0)
    i = pl.program_id(0)
    j = pl.program_id(1)

    @pl.when(j == 0)
    def _():
        h = x_ref[...] * (1.0 + sc_ref[...]) + sh_ref[...]
        hb = h.astype(BF16)
        h_scr[...] = hb
        if has_tail:
            ot_ref[...] = _dot(hb, wt_ref[...])

    if n_split:
        for k, ok_ref in enumerate(o_refs):
            @pl.when(j == k)
            def _(ok_ref=ok_ref):
                ok_ref[...] = _dot(h_scr[...], w_ref[...]).astype(ok_ref.dtype)
        return
    if not n_conv:
        o_ref[...] = _dot(h_scr[...], w_ref[...]).astype(o_ref.dtype)
        return
    carry_scr, p_scr = refs
    tm, tn = o_ref.shape
    W = DN_CONV_W
    P0 = DN_CONV_PAD
    cw = CONV_COL_CHUNK

    def conv_tile(jj):
        row_start = (i % tiles_per_row) == 0
        prev = jnp.where(row_start, 0.0, carry_scr[jj])
        tails = []
        chunks = list(range(0, tn, cw))
        p_scr[0] = _dot(h_scr[...], w_ref[:, 0:cw])
        for k, c0 in enumerate(chunks):
            cs = slice(c0, c0 + cw)
            if k + 1 < len(chunks):
                p_scr[(k + 1) % 2] = _dot(h_scr[...], w_ref[:, c0 + cw:c0 + 2 * cw])
            p = p_scr[k % 2]
            y = p * cw_ref[W - 1:W, cs]
            for t in range(W - 1):
                y = y + pltpu.roll(p, W - 1 - t, 0) * cw_ref[t:t + 1, cs]
            o_ref[:, cs] = _silu(y).astype(o_ref.dtype)
            nfix = 2 * SUBLANES
            head = jnp.concatenate([prev[:, cs], p[:nfix]], axis=0)
            y0 = head[P0:] * cw_ref[W - 1:W, cs]
            for t in range(W - 1):
                y0 = y0 + head[P0 - (W - 1 - t):P0 + nfix - (W - 1 - t)] * cw_ref[t:t + 1, cs]
            o_ref[0:nfix, cs] = _silu(y0).astype(o_ref.dtype)
            tails.append(p[tm - P0:])
        carry_scr[jj] = jnp.concatenate(tails, axis=1)

    for jj in range(n_conv):
        pl.when(j == jj)(functools.partial(conv_tile, jj))

    @pl.when(j >= n_conv)
    def _():
        o_ref[...] = _dot(h_scr[...], w_ref[...]).astype(o_ref.dtype)


def _inproj(geo, x, mod, w, w_tail, tn, conv_w=None, conv_cols=0, tm=None, split=False):
    if tm is not None:
        geo = geo._replace(tm=tm)
    T, tm = geo.T, geo.tm
    N = w.shape[1]
    has_tail = w_tail is not None
    n_conv = conv_cols // tn
    in_specs = [
        pl.BlockSpec((tm, D_MODEL), lambda i, j: (i, 0)),
        _mod_spec(geo, 1),
        _mod_spec(geo, 0),
        pl.BlockSpec((D_MODEL, tn), lambda i, j: (0, j)),
    ]
    args = [x, mod, mod, w]
    n_split = N // tn if split else 0
    if split:
        assert not has_tail and not n_conv
        out_specs = [pl.BlockSpec((tm, tn), lambda i, j: (i, 0))] * n_split
        out_shape = [jax.ShapeDtypeStruct((T, tn), geo.act_dtype)] * n_split
    else:
        out_specs = [pl.BlockSpec((tm, tn), lambda i, j: (i, j))]
        out_shape = [jax.ShapeDtypeStruct((T, N), geo.act_dtype)]
    scratch = [pltpu.VMEM((tm, D_MODEL), BF16)]
    if has_tail:
        in_specs.append(pl.BlockSpec((D_MODEL, LANES), lambda i, j: (0, 0)))
        args.append(w_tail)
        out_specs.append(pl.BlockSpec((tm, LANES), lambda i, j: (i, 0)))
        out_shape.append(jax.ShapeDtypeStruct((T, LANES), F32))
    if n_conv:
        assert conv_cols == n_conv * tn and not geo.per_token
        in_specs.append(pl.BlockSpec((DN_CONV_W, tn), lambda i, j: (0, jnp.minimum(j, n_conv - 1))))
        args.append(conv_w)
        scratch += [pltpu.VMEM((n_conv, DN_CONV_PAD, tn), F32), pltpu.VMEM((2, tm, CONV_COL_CHUNK), F32)]
    res = pl.pallas_call(
        functools.partial(_inproj_kernel, has_tail=has_tail, n_conv=n_conv, n_split=n_split,
                          tiles_per_row=geo.L // tm),
        grid=(T // tm, N // tn),
        in_specs=in_specs,
        out_specs=out_specs,
        out_shape=out_shape,
        scratch_shapes=scratch,
        compiler_params=_cparams(("arbitrary", "arbitrary")),
        name="in_proj",
    )(*args)
    if split:
        return tuple(res), None
    return res if has_tail else (res[0], None)


def _conv_tail_kernel(x_ref, sc_ref, sh_ref, w_ref, o_ref):
    h = x_ref[...] * (1.0 + sc_ref[...]) + sh_ref[...]
    o_ref[...] = _dot(h.astype(BF16), w_ref[...])


def _conv_tail(geo, x, mod, w, n_cols, tn):
    B, L = geo.B, geo.L
    blocks_per_row = L // SUBLANES
    return pl.pallas_call(
        _conv_tail_kernel,
        grid=(n_cols // tn, B),
        in_specs=[
            pl.BlockSpec((SUBLANES, D_MODEL), lambda j, b: (b * blocks_per_row + blocks_per_row - 1, 0)),
            pl.BlockSpec((None, 1, D_MODEL), lambda j, b: (b, 0, 1)),
            pl.BlockSpec((None, 1, D_MODEL), lambda j, b: (b, 0, 0)),
            pl.BlockSpec((D_MODEL, tn), lambda j, b: (0, j)),
        ],
        out_specs=pl.BlockSpec((None, SUBLANES, tn), lambda j, b: (b, 0, j)),
        out_shape=jax.ShapeDtypeStruct((B, SUBLANES, n_cols), F32),
        compiler_params=_cparams(("parallel", "parallel")),
        name="conv_tail",
    )(x, mod, mod, w)


def _call_stacked(kern, prev, first_stacked_out, *, grid, in_specs, args, out_specs, out_shape,
                  scratch_shapes, name):
    n_in = len(args)
    aliases = {}
    body = kern
    if prev is not None:
        n_prev = len(prev)
        in_specs = list(in_specs) + [pl.BlockSpec(memory_space=pl.ANY)] * n_prev
        args = list(args) + list(prev)
        aliases = {n_in + k: first_stacked_out + k for k in range(n_prev)}

        def body(*refs):
            kern(*refs[:n_in], *refs[n_in + n_prev:])

    return pl.pallas_call(
        body, grid=grid, in_specs=in_specs, out_specs=out_specs, out_shape=out_shape,
        scratch_shapes=scratch_shapes, input_output_aliases=aliases,
        compiler_params=_cparams(("parallel", "arbitrary")), name=name,
    )(*args)


def _stack_fill(prev, layer_j, n_layers):
    if prev is None:
        return (layer_j, n_layers), n_layers, 0
    return None, None, layer_j


def _pad_rows(x, rows):
    if x.shape[0] == rows:
        return x
    return jnp.concatenate([x, jnp.zeros((rows - x.shape[0], x.shape[1]), x.dtype)], axis=0)


def _ret_kernel(*refs, c, rows, nchunk, has_state, fill):
    if has_state:
        (q_ref, k_ref, v_ref, g_ref, cos_ref, sin_ref, dintra_ref, qdec_ref, kdec_ref, sdec_ref, gn_ref,
         s0_ref, o_ref, s_out_ref, s_scr) = refs
    else:
        (q_ref, k_ref, v_ref, g_ref, cos_ref, sin_ref, dintra_ref, qdec_ref, kdec_ref, sdec_ref, gn_ref,
         o_ref, s_out_ref, s_scr) = refs
    i = pl.program_id(1)
    half = RET_DK // 2

    @pl.when(i == 0)
    def _():
        if has_state:
            s_scr[...] = s0_ref[...]
        else:
            s_scr[...] = jnp.zeros_like(s_scr)

    def rot(ref, r, h, cos, sin):
        x1 = ref[r, h * RET_DK:h * RET_DK + half].astype(F32)
        x2 = ref[r, h * RET_DK + half:(h + 1) * RET_DK].astype(F32)
        return jnp.concatenate([x1 * cos - x2 * sin, x2 * cos + x1 * sin], axis=1)

    for ci in range(nchunk):
        r = slice(ci * rows, (ci + 1) * rows)
        cos = cos_ref[r, :]
        sin = sin_ref[r, :]
        for h in range(RET_HEADS):
            vs = slice(h * RET_DV, (h + 1) * RET_DV)
            q = _pad_rows(rot(q_ref, r, h, cos, sin), c)
            k = _pad_rows(rot(k_ref, r, h, cos, sin), c) * (RET_DK ** -0.5)
            if rows == c:
                v = v_ref[r, vs].astype(BF16)
            else:
                v = _pad_rows(v_ref[r, vs].astype(F32), c).astype(BF16)
            s = s_scr[h]
            att = _dot_nt(q.astype(BF16), k.astype(BF16)) * dintra_ref[h]
            o = _dot(att.astype(BF16), v) + _dot((q * qdec_ref[h]).astype(BF16), s.astype(BF16))
            s_scr[h] = s * sdec_ref[h] + _dot_tn((k * kdec_ref[h]).astype(BF16), v)
            o = o[:rows]
            mu = jnp.mean(o, axis=-1, keepdims=True)
            oc = o - mu
            var = jnp.mean(oc * oc, axis=-1, keepdims=True)
            on = oc * lax.rsqrt(var + LN_EPS) * gn_ref[...]
            gate = g_ref[r, vs].astype(F32)
            o_ref[r, vs] = (on * _silu(gate)).astype(o_ref.dtype)

    @pl.when(i == pl.num_programs(1) - 1)
    def _():
        if fill is None:
            s_out_ref[...] = s_scr[...]
        else:
            for l in range(fill[1]):
                s_out_ref[l] = s_scr[...] if l == fill[0] else jnp.zeros(s_scr.shape, F32)


def _ret_tables(geo, chunk):
    c_true = min(chunk, geo.l_true)
    half = RET_DK // 2
    inv_freq = 10000.0 ** (-jnp.linspace(0.0, 1.0, half, dtype=F32))
    pos = (geo.pos0 + jnp.arange(geo.L)).astype(F32)
    ang = pos[:, None] * inv_freq[None, :]
    lg = jnp.log(1.0 - 2.0 ** (-5.0 - jnp.arange(RET_HEADS, dtype=F32)))
    idx = jnp.arange(c_true, dtype=F32)
    diff = idx[:, None] - idx[None, :]
    dintra = jnp.exp(jnp.where(diff[None] >= 0, diff[None] * lg[:, None, None], -jnp.inf))
    qdec = jnp.exp((idx + 1.0)[None, :] * lg[:, None])
    kdec = jnp.exp((c_true - 1.0 - idx)[None, :] * lg[:, None])
    sdec = jnp.exp(c_true * lg)
    pad = chunk - c_true
    dintra = jnp.pad(dintra, ((0, 0), (0, pad), (0, pad)))
    qdec = jnp.pad(qdec, ((0, 0), (0, pad)))[..., None]
    kdec = jnp.pad(kdec, ((0, 0), (0, pad)))[..., None]
    sdec = jnp.broadcast_to(sdec[:, None, None], (RET_HEADS, 1, RET_DV))
    return jnp.cos(ang), jnp.sin(ang), dintra, qdec, kdec, sdec


def _retention(geo, p, tables, gn_g, s_all, layer_j, prev, tl, chunk):
    B, L = geo.B, geo.L
    nblk = L // tl
    rows = min(tl, chunk)
    nchunk = tl // rows
    has_state = s_all is not None
    cos, sin, dintra, qdec, kdec, sdec = tables
    rb = lambda b, i: b * nblk + i
    full3 = lambda b, i: (0, 0, 0)
    if isinstance(p, tuple):
        qk, v, gate = p
        parts = [(qk, 0), (qk, 1), (v, 0), (gate, 0)]
    else:
        parts = [(p, 0), (p, 1), (p, 1), (p, 2)]
    col = [cb for _, cb in parts]
    in_specs = [
        pl.BlockSpec((tl, RET_QW), lambda b, i: (rb(b, i), col[0])),
        pl.BlockSpec((tl, RET_QW), lambda b, i: (rb(b, i), col[1])),
        pl.BlockSpec((tl, RET_VW), lambda b, i: (rb(b, i), col[2])),
        pl.BlockSpec((tl, RET_VW), lambda b, i: (rb(b, i), col[3])),
        pl.BlockSpec((tl, RET_DK // 2), lambda b, i: (i, 0)),
        pl.BlockSpec((tl, RET_DK // 2), lambda b, i: (i, 0)),
        pl.BlockSpec((RET_HEADS, chunk, chunk), full3),
        pl.BlockSpec((RET_HEADS, chunk, 1), full3),
        pl.BlockSpec((RET_HEADS, chunk, 1), full3),
        pl.BlockSpec((RET_HEADS, 1, RET_DV), full3),
        pl.BlockSpec((1, RET_DV), lambda b, i: (0, 0)),
    ]
    args = [a for a, _ in parts] + [cos, sin, dintra, qdec, kdec, sdec, gn_g.reshape(1, RET_DV)]
    if has_state:
        in_specs.append(pl.BlockSpec((None, None, RET_HEADS, RET_DK, RET_DV),
                                     lambda b, i: (layer_j, b, 0, 0, 0)))
        args.append(s_all)
    fill, lead, lidx = _stack_fill(prev, layer_j, N_RET_LAYERS)
    o, s = _call_stacked(
        functools.partial(_ret_kernel, c=chunk, rows=rows, nchunk=nchunk, has_state=has_state, fill=fill),
        prev, first_stacked_out=1,
        grid=(B, nblk),
        in_specs=in_specs,
        args=args,
        out_specs=[
            pl.BlockSpec((tl, RET_VW), lambda b, i: (rb(b, i), 0)),
            pl.BlockSpec((lead, None, RET_HEADS, RET_DK, RET_DV), lambda b, i: (lidx, b, 0, 0, 0)),
        ],
        out_shape=[
            jax.ShapeDtypeStruct((geo.T, RET_VW), geo.act_dtype),
            jax.ShapeDtypeStruct((N_RET_LAYERS, B, RET_HEADS, RET_DK, RET_DV), F32),
        ],
        scratch_shapes=[pltpu.VMEM((RET_HEADS, RET_DK, RET_DV), F32)],
        name="retention",
    )
    return o, (s,)


DN_CONV_PAD = SUBLANES
CONV_COL_CHUNK = 256


def _block_diag(x_cat, n, mask):
    return jnp.where(mask, jnp.concatenate([x_cat] * n, axis=0), 0.0)


def _dn_kernel(*refs, c, rows, nchunk, l_last, has_state, fill):
    if has_state:
        (qkv_ref, z_ref, ab_ref, cw_ref, alog_ref, dtb_ref, ng_ref, s0_ref, buf_ref,
         o_ref, s_out_ref, buf_out_ref, s_scr, xp_scr, y_scr) = refs
    else:
        (qkv_ref, z_ref, ab_ref, alog_ref, dtb_ref, ng_ref, o_ref, s_out_ref, s_scr) = refs
    i = pl.program_id(1)
    hq = min(DN_V_HEADS, MXU_DIM // c)
    tl = rows * nchunk
    npair = DN_V_HEADS // 2
    W = DN_CONV_W
    P0 = DN_CONV_PAD

    @pl.when(i == 0)
    def _():
        if has_state:
            for pr in range(npair):
                s_scr[pr] = jnp.concatenate([s0_ref[2 * pr], s0_ref[2 * pr + 1]], axis=1)
            xp_scr[P0 - (W - 1):P0, :] = buf_ref[...]
        else:
            s_scr[...] = jnp.zeros_like(s_scr)

    if has_state:
        xp_scr[P0:P0 + tl, :] = qkv_ref[...].astype(F32)
        y = xp_scr[P0 - (W - 1):P0 - (W - 1) + tl, :] * cw_ref[0:1, :]
        for t in range(1, W):
            y = y + xp_scr[P0 - (W - 1) + t:P0 - (W - 1) + t + tl, :] * cw_ref[t:t + 1, :]
        y_scr[0:tl, :] = _silu(y)
        if rows < c:
            y_scr[tl:, :] = jnp.zeros((y_scr.shape[0] - tl, DN_CONV_CH), F32)

        @pl.when(i == pl.num_programs(1) - 1)
        def _():
            tail = xp_scr[P0 + l_last - (W - 1):P0 + l_last, :]
            if fill is None:
                buf_out_ref[...] = tail
            else:
                for l in range(fill[1]):
                    buf_out_ref[l] = tail if l == fill[0] else jnp.zeros(tail.shape, F32)

        xp_scr[P0 - (W - 1):P0, :] = xp_scr[P0 + tl - (W - 1):P0 + tl, :]

        def ycols(r, lo, hi):
            return y_scr[r, lo:hi]
    else:
        def ycols(r, lo, hi):
            return qkv_ref[r, lo:hi].astype(F32)

    ri = lax.broadcasted_iota(jnp.int32, (c, c), 0)
    ci_ = lax.broadcasted_iota(jnp.int32, (c, c), 1)
    incl = ri >= ci_
    strict = ri > ci_
    eye = ri == ci_
    tri_f = incl.astype(F32)
    eye_cat = jnp.concatenate([eye.astype(F32)] * hq, axis=1)
    bi = lax.broadcasted_iota(jnp.int32, (hq * c, hq * c), 0) // c
    bj = lax.broadcasted_iota(jnp.int32, (hq * c, hq * c), 1) // c
    bd_mask = bi == bj
    lane2 = lax.broadcasted_iota(jnp.int32, (DN_DK, 2 * DN_DV), 1)
    left = lane2 < DN_DV
    padded = rows < c or l_last < rows
    row_ok = lax.broadcasted_iota(jnp.int32, (c, 1), 0) < l_last if padded else None
    n_levels = c.bit_length() - 1

    oi = lax.broadcasted_iota(jnp.int32, (2 * DN_DK, 2 * DN_DK), 0) // DN_DK
    oj = lax.broadcasted_iota(jnp.int32, (2 * DN_DK, 2 * DN_DK), 1) // DN_DK
    ones_bd = jnp.where(oi == oj, 1.0, 0.0).astype(BF16)

    def sumsq2(x):
        if has_state:
            parts = [jnp.broadcast_to(jnp.sum(x[:, lo:lo + DN_DK] * x[:, lo:lo + DN_DK], axis=-1, keepdims=True),
                                      (x.shape[0], DN_DK)) for lo in (0, DN_DK)]
            return jnp.concatenate(parts, axis=1)
        return _dot((x * x).astype(BF16), ones_bd)

    nquad = DN_V_HEADS // hq
    quads = [[hq * q + t for t in range(hq)] for q in range(nquad)]

    def phase1(ck):
        r0 = ck * c if rows == c else 0
        r = slice(r0, r0 + c)
        ab = _pad_rows(ab_ref[ck * rows:(ck + 1) * rows, :], c)
        sp = ab + dtb_ref[...]
        g_full = -jnp.exp(alog_ref[...]) * (jnp.maximum(sp, 0.0) + jnp.log1p(jnp.exp(-jnp.abs(sp))))
        beta_full = jax.nn.sigmoid(ab)
        if row_ok is not None:
            g_full = jnp.where(row_ok, g_full, 0.0)
            beta_full = jnp.where(row_ok, beta_full, 0.0)
        G_full = jnp.dot(tri_f, g_full, preferred_element_type=F32, precision=lax.Precision.HIGHEST)
        yield

        kn, qn, KK, QK = {}, {}, {}, {}
        for j in range(0, DN_QK_HEADS, 2):
            q2 = ycols(r, j * DN_DK, (j + 2) * DN_DK)
            k2 = ycols(r, DN_QW + j * DN_DK, DN_QW + (j + 2) * DN_DK)
            q2 = q2 * lax.rsqrt(sumsq2(q2) + 1e-6) * (DN_DK ** -0.5)
            k2 = k2 * lax.rsqrt(sumsq2(k2) + 1e-6)
            qn[j], qn[j + 1] = q2[:, :DN_DK], q2[:, DN_DK:]
            kn[j], kn[j + 1] = k2[:, :DN_DK], k2[:, DN_DK:]
        yield
        for j in range(DN_QK_HEADS):
            kb16 = kn[j].astype(BF16)
            kq = _dot_nt(jnp.concatenate([kb16, qn[j].astype(BF16)], axis=0), kb16)
            KK[j], QK[j] = kq[:c], kq[c:]
        yield

        A, att, eG, Gc, rhs = {}, {}, {}, {}, {}
        for h in range(DN_V_HEADS):
            j = h // 2
            Gc[h] = jnp.broadcast_to(G_full[:, h:h + 1], (c, DN_DV))
            bc = jnp.broadcast_to(beta_full[:, DN_V_HEADS + h:DN_V_HEADS + h + 1], (c, DN_DV))
            Gcc = Gc[h][:, :c]
            Gr = jnp.sum(jnp.where(eye, Gcc, 0.0), axis=0, keepdims=True)
            dec = jnp.exp(jnp.where(incl, Gcc - Gr, -jnp.inf))
            A[h] = jnp.where(strict, bc[:, :c] * KK[j] * dec, 0.0)
            att[h] = QK[j] * dec
            eG[h] = jnp.exp(Gc[h])
            vh = ycols(r, 2 * DN_QW + h * DN_DV, 2 * DN_QW + (h + 1) * DN_DV)
            rhs[h] = jnp.concatenate([vh * bc, kn[j] * (bc * eG[h])], axis=1)
        yield

        X = [jnp.concatenate([A[h] for h in hs], axis=1) for hs in quads]
        Pm = [eye_cat - x for x in X]
        X = [_dot(x.astype(BF16), _block_diag(x, hq, bd_mask).astype(BF16)) for x in X]
        yield
        for lvl in range(1, n_levels):
            Xw = [_block_diag(x, hq, bd_mask).astype(BF16) for x in X]
            if lvl < n_levels - 1:
                both = [_dot(jnp.concatenate([X[q], Pm[q]], axis=0).astype(BF16), Xw[q]) for q in range(nquad)]
                X = [b[:c] for b in both]
                Pm = [Pm[q] + both[q][c:] for q in range(nquad)]
            else:
                Pm = [Pm[q] + _dot(Pm[q].astype(BF16), Xw[q]) for q in range(nquad)]
            yield
        sol = [_dot(_block_diag(Pm[q], hq, bd_mask).astype(BF16),
                    jnp.concatenate([rhs[h] for h in quads[q]], axis=0).astype(BF16))
               for q in range(nquad)]
        att_bd = [_block_diag(jnp.concatenate([att[h] for h in quads[q]], axis=1), hq, bd_mask).astype(BF16)
                  for q in range(nquad)]
        return kn, qn, Gc, eG, sol, att_bd

    def phase2(ck, pre):
        kn, qn, Gc, eG, sol, att_bd = pre
        sol_h = {h: sol[h // hq][c * (h % hq):c * (h % hq + 1)] for h in range(DN_V_HEADS)}
        u, oS = {}, {}
        for pr in range(npair):
            h0, h1 = 2 * pr, 2 * pr + 1
            S = s_scr[pr]
            Sbd = jnp.concatenate([jnp.where(left, S, 0.0), jnp.where(left, 0.0, S)], axis=0).astype(BF16)
            w_cat = jnp.concatenate([sol_h[h0][:, DN_DV:], sol_h[h1][:, DN_DV:]], axis=1)
            qe_cat = jnp.concatenate([qn[pr] * eG[h0], qn[pr] * eG[h1]], axis=1)
            R = _dot(jnp.concatenate([w_cat, qe_cat], axis=0).astype(BF16), Sbd)
            u[h0] = sol_h[h0][:, :DN_DV] - R[:c, :DN_DV]
            u[h1] = sol_h[h1][:, :DN_DV] - R[:c, DN_DV:]
            oS[h0] = R[c:, :DN_DV]
            oS[h1] = R[c:, DN_DV:]
        yield

        o_intra = [_dot(att_bd[q], jnp.concatenate([u[h] for h in quads[q]], axis=0).astype(BF16))
                   for q in range(nquad)]
        yield

        for pr in range(npair):
            h0, h1 = 2 * pr, 2 * pr + 1
            gl0 = Gc[h0][c - 1:c, :]
            gl1 = Gc[h1][c - 1:c, :]
            du = jnp.concatenate([u[h0] * jnp.exp(gl0 - Gc[h0]), u[h1] * jnp.exp(gl1 - Gc[h1])], axis=1)
            sdec = jnp.concatenate([jnp.exp(gl0), jnp.exp(gl1)], axis=1)
            s_scr[pr] = s_scr[pr] * sdec + _dot_tn(kn[pr].astype(BF16), du.astype(BF16))
        yield

        zr = slice(ck * rows, (ck + 1) * rows)
        ng2 = jnp.concatenate([ng_ref[...], ng_ref[...]], axis=1)
        for h in range(0, DN_V_HEADS, 2):
            o = jnp.concatenate([(oS[hh] + o_intra[hh // hq][c * (hh % hq):c * (hh % hq + 1)])[:rows]
                                 for hh in (h, h + 1)], axis=1)
            o = o * lax.rsqrt(sumsq2(o) * (1.0 / DN_DV) + 1e-6) * ng2
            hs = slice(h * DN_DV, (h + 2) * DN_DV)
            z = z_ref[zr, hs].astype(F32)
            o_ref[zr, hs] = (o * _silu(z)).astype(o_ref.dtype)

    def lockstep(gens):
        out = {}
        while gens:
            for key in list(gens):
                try:
                    next(gens[key])
                except StopIteration as done:
                    out[key] = done.value
                    del gens[key]
        return out

    def recurrence(group, pre):
        for ck in group:
            yield from phase2(ck, pre[ck])

    groups = [list(range(g0, min(g0 + DN_LOCKSTEP, nchunk))) for g0 in range(0, nchunk, DN_LOCKSTEP)]
    pre = lockstep({ck: phase1(ck) for ck in groups[0]})
    for gi, group in enumerate(groups):
        gens = {"recurrence": recurrence(group, pre)}
        if gi + 1 < len(groups):
            gens.update({ck: phase1(ck) for ck in groups[gi + 1]})
        pre = lockstep(gens)

    @pl.when(i == pl.num_programs(1) - 1)
    def _():
        for pr in range(npair):
            for l in range(1 if fill is None else fill[1]):
                mine = fill is None or l == fill[0]
                dst = s_out_ref if fill is None else s_out_ref.at[l]
                dst[2 * pr] = s_scr[pr][:, :DN_DV] if mine else jnp.zeros((DN_DK, DN_DV), F32)
                dst[2 * pr + 1] = s_scr[pr][:, DN_DV:] if mine else jnp.zeros((DN_DK, DN_DV), F32)


def _deltanet(geo, p, ab, conv_w, a_log, dt_bias, norm_g, s_all, buf_all, layer_j, prev, tl, chunk):
    B, L = geo.B, geo.L
    nblk = L // tl
    rows = min(tl, chunk)
    nchunk = tl // rows
    l_last = geo.l_true - (nblk - 1) * tl
    assert l_last == tl or (nblk == 1 and nchunk == 1)
    has_state = s_all is not None
    rb = lambda b, i: b * nblk + i
    pad16 = lambda v: jnp.pad(v.reshape(1, DN_V_HEADS), ((0, 0), (0, LANES - DN_V_HEADS)))
    row1 = lambda n: pl.BlockSpec((1, n), lambda b, i: (0, 0))
    in_specs = [
        pl.BlockSpec((tl, DN_CONV_CH), lambda b, i: (rb(b, i), 0)),
        pl.BlockSpec((tl, DN_VW), lambda b, i: (rb(b, i), DN_CONV_CH // DN_VW)),
        pl.BlockSpec((tl, LANES), lambda b, i: (rb(b, i), 0)),
    ]
    args = [p, p, ab]
    if has_state:
        in_specs.append(pl.BlockSpec((DN_CONV_W, DN_CONV_CH), lambda b, i: (0, 0)))
        args.append(conv_w)
    in_specs += [row1(LANES), row1(LANES), row1(DN_DV)]
    args += [pad16(a_log), pad16(dt_bias), norm_g.reshape(1, DN_DV)]
    fill, lead, lidx = _stack_fill(prev, layer_j, N_DN_LAYERS)
    out_specs = [
        pl.BlockSpec((tl, DN_VW), lambda b, i: (rb(b, i), 0)),
        pl.BlockSpec((lead, None, DN_V_HEADS, DN_DK, DN_DV), lambda b, i: (lidx, b, 0, 0, 0)),
    ]
    out_shape = [
        jax.ShapeDtypeStruct((geo.T, DN_VW), geo.act_dtype),
        jax.ShapeDtypeStruct((N_DN_LAYERS, B, DN_V_HEADS, DN_DK, DN_DV), F32),
    ]
    scratch = [pltpu.VMEM((DN_V_HEADS // 2, DN_DK, 2 * DN_DV), F32)]
    if has_state:
        in_specs += [
            pl.BlockSpec((None, None, DN_V_HEADS, DN_DK, DN_DV), lambda b, i: (layer_j, b, 0, 0, 0)),
            pl.BlockSpec((None, None, DN_CONV_W - 1, DN_CONV_CH), lambda b, i: (layer_j, b, 0, 0)),
        ]
        args += [s_all, buf_all]
        out_specs.append(pl.BlockSpec((lead, None, DN_CONV_W - 1, DN_CONV_CH), lambda b, i: (lidx, b, 0, 0)))
        out_shape.append(jax.ShapeDtypeStruct((N_DN_LAYERS, B, DN_CONV_W - 1, DN_CONV_CH), F32))
        scratch += [pltpu.VMEM((DN_CONV_PAD + tl, DN_CONV_CH), F32),
                    pltpu.VMEM((max(tl, chunk), DN_CONV_CH), F32)]
    kern = functools.partial(_dn_kernel, c=chunk, rows=rows, nchunk=nchunk, l_last=l_last, has_state=has_state,
                             fill=fill)
    res = _call_stacked(kern, prev, first_stacked_out=1, grid=(B, nblk), in_specs=in_specs, args=args,
                        out_specs=out_specs, out_shape=out_shape, scratch_shapes=scratch, name="deltanet")
    return res[0], tuple(res[1:])


def _outproj_kernel(o_ref, w_ref, x_ref, gate_ref, g_ref, b_ref, out_ref):
    y = _dot(o_ref[...].astype(BF16), w_ref[...])
    z = ALPHA * x_ref[...] + (1.0 + gate_ref[...]) * y
    out_ref[...] = _layer_norm(z, g_ref[...], b_ref[...])


def _outproj_ln(geo, o, w_out, x, mod, ln_g, ln_b):
    if not geo.per_token and geo.L % 1024 == 0:
        geo = geo._replace(tm=1024)
    T, tm = geo.T, geo.tm
    K = w_out.shape[0]
    vec = pl.BlockSpec((1, D_MODEL), lambda i: (0, 0))
    return pl.pallas_call(
        _outproj_kernel,
        grid=(T // tm,),
        in_specs=[
            pl.BlockSpec((tm, K), lambda i: (i, 0)),
            pl.BlockSpec((K, D_MODEL), lambda i: (0, 0)),
            pl.BlockSpec((tm, D_MODEL), lambda i: (i, 0)),
            _mod_spec(geo, 2),
            vec, vec,
        ],
        out_specs=pl.BlockSpec((tm, D_MODEL), lambda i: (i, 0)),
        out_shape=jax.ShapeDtypeStruct((T, D_MODEL), F32),
        compiler_params=_cparams(("parallel",)),
        name="out_proj_ln",
    )(o, w_out, x, mod, ln_g.reshape(1, D_MODEL), ln_b.reshape(1, D_MODEL))


MOE_ROWS = 144


def _moe_blocks(tm):
    nb = (tm - N_GROUPS) // MOE_ROWS + N_GROUPS
    ns = -(-(nb * MOE_ROWS) // LANES) * LANES
    return nb, ns


def _first_max(v, row, valid):
    m = jnp.max(jnp.where(valid, v, -jnp.inf), axis=0, keepdims=True)
    idx = jnp.min(jnp.where(valid & (v == m), row, float(N_EXPERTS)), axis=0, keepdims=True)
    return m, idx


def _route(h, wrt, br):
    h_hi = h.astype(BF16)
    h_lo = (h - h_hi.astype(F32)).astype(BF16)
    w_hi = wrt.astype(BF16)
    w_lo = (wrt - w_hi.astype(F32)).astype(BF16)
    ww = _dot_nt(jnp.concatenate([w_hi, w_lo], axis=0), h_hi)
    logits = ww[:N_EXPERTS] + ww[N_EXPERTS:] + _dot_nt(w_hi, h_lo)
    mx = jnp.max(logits, axis=0, keepdims=True)
    ex = jnp.exp(logits - mx)
    probs = ex / jnp.sum(ex, axis=0, keepdims=True)
    sel = probs + br
    row_i = lax.broadcasted_iota(jnp.int32, sel.shape, 0)
    grp = (row_i // EXPERTS_PER_GROUP).astype(F32)
    row = row_i.astype(F32)
    best = None
    gsel = None
    for g in range(N_GROUPS):
        in_g = grp == g
        m1, i1 = _first_max(sel, row, in_g)
        m2, _ = _first_max(sel, row, in_g & (row != i1))
        score = m1 + m2
        if g == 0:
            best, gsel = score, jnp.zeros_like(i1)
        else:
            better = score > best
            gsel = jnp.where(better, float(g), gsel)
            best = jnp.where(better, score, best)
    in_grp = grp == gsel
    _, i1 = _first_max(sel, row, in_grp)
    _, i2 = _first_max(sel, row, in_grp & (row != i1))
    w1 = jnp.sum(jnp.where(row == i1, probs, 0.0), axis=0, keepdims=True)
    w2 = jnp.sum(jnp.where(row == i2, probs, 0.0), axis=0, keepdims=True)
    wsum = w1 + w2
    gates = jnp.where(row == i1, w1 / wsum, 0.0) + jnp.where(row == i2, w2 / wsum, 0.0)
    return gsel, gates


def _router_kernel(x_ref, sc_ref, sh_ref, wrt_ref, br_ref, posr_ref, gates_ref, tbl_ref):
    tm = x_ref.shape[0]
    h = x_ref[...] * (1.0 + sc_ref[...]) + sh_ref[...]
    gsel, gates = _route(h, wrt_ref[...], br_ref[...])
    gates_ref[...] = gates
    grow = lax.broadcasted_iota(jnp.int32, (SUBLANES, tm), 0).astype(F32)
    og = grow == gsel
    ti = lax.broadcasted_iota(jnp.int32, (tm, tm), 0)
    tj = lax.broadcasted_iota(jnp.int32, (tm, tm), 1)
    before = jnp.where(ti < tj, 1.0, 0.0).astype(BF16)
    ogf = jnp.where(og, 1.0, 0.0)
    rank = _dot(ogf.astype(BF16), before)
    cnt = jnp.sum(ogf, axis=1, keepdims=True)
    nblk = jnp.zeros_like(cnt)
    for kb in range(-(-tm // MOE_ROWS)):
        nblk = nblk + jnp.where(cnt > float(kb * MOE_ROWS), 1.0, 0.0)
    seg = nblk * float(MOE_ROWS)
    block_start = lax.broadcasted_iota(jnp.int32, (1, LANES), 1).astype(F32) * float(MOE_ROWS)
    off = jnp.zeros((1, 1), F32)
    pos = jnp.zeros((1, tm), F32)
    tbl = jnp.zeros((1, LANES), F32)
    for g in range(N_GROUPS):
        pos = pos + jnp.where(gsel == float(g), off + rank[g:g + 1, :], 0.0)
        off = off + seg[g:g + 1, :]
        tbl = tbl + jnp.where(block_start >= off, 1.0, 0.0)
    posr_ref[...] = pos
    tbl_ref[...] = tbl.astype(jnp.int32)


def _expert_kernel(tbl_ref, x_ref, sc_ref, sh_ref, gt_ref, posr_ref, gates_ref, wgu_ref, wd_ref,
                   g_ref, b_ref, out_ref, hs_scr, gs_scr, ys_scr, p_scr, *, nb):
    i = pl.program_id(0)
    tm = x_ref.shape[0]
    ns = hs_scr.shape[0]
    R = MOE_ROWS
    x = x_ref[...]
    hb = (x * (1.0 + sc_ref[...]) + sh_ref[...]).astype(BF16)
    slot = lax.broadcasted_iota(jnp.int32, (ns, tm), 0).astype(F32)
    P = jnp.where(slot == posr_ref[...], 1.0, 0.0).astype(BF16)
    p_scr[...] = P
    hs_scr[...] = _dot(P, hb).astype(BF16)
    gates = gates_ref[...]
    g_hi = gates.astype(BF16)
    g_lo = (gates - g_hi.astype(F32)).astype(BF16)
    gs_scr[...] = _dot_nt(P, g_hi) + _dot_nt(P, g_lo)
    if nb * R < ns:
        ys_scr[nb * R:, :] = jnp.zeros((ns - nb * R, D_MODEL), BF16)

    for b in range(nb):
        rs = slice(b * R, (b + 1) * R)
        g = tbl_ref[i, b]

        @pl.when(g < N_GROUPS)
        def _():
            gsb = gs_scr[rs, :]
            lane = lax.broadcasted_iota(jnp.int32, gsb.shape, 1)
            hsb = hs_scr[rs, :]
            acts = []
            for e in range(EXPERTS_PER_GROUP):
                gu = _dot(hsb, wgu_ref[g, e])
                gate = jnp.sum(jnp.where(lane == g * EXPERTS_PER_GROUP + e, gsb, 0.0), axis=1, keepdims=True)
                act = _silu(gu[:, :D_EXPERT]) * gu[:, D_EXPERT:] * gate
                acts.append(act.astype(BF16))
            ys_scr[rs, :] = _dot(jnp.concatenate(acts, axis=1), wd_ref[g]).astype(BF16)

        @pl.when(g >= N_GROUPS)
        def _():
            ys_scr[rs, :] = jnp.zeros((R, D_MODEL), BF16)

    y = _dot_tn(p_scr[...], ys_scr[...])
    z = ALPHA * x + (1.0 + gt_ref[...]) * y
    out_ref[...] = _layer_norm(z, g_ref[...], b_ref[...])


def _moe_ln(geo, x, mod, w_router_t, b_router, w_gu, w_down, layer, ln_g, ln_b):
    T, tm = geo.T, geo.tm
    nt = T // tm
    nb, ns = _moe_blocks(tm)
    posr, gates, tbl = pl.pallas_call(
        _router_kernel,
        grid=(nt,),
        in_specs=[
            pl.BlockSpec((tm, D_MODEL), lambda i: (i, 0)),
            _mod_spec(geo, 4),
            _mod_spec(geo, 3),
            pl.BlockSpec((N_EXPERTS, D_MODEL), lambda i: (0, 0)),
            pl.BlockSpec((N_EXPERTS, 1), lambda i: (0, 0)),
        ],
        out_specs=[
            pl.BlockSpec((None, 1, tm), lambda i: (i, 0, 0)),
            pl.BlockSpec((N_EXPERTS, tm), lambda i: (0, i)),
            pl.BlockSpec((None, 1, LANES), lambda i: (i, 0, 0)),
        ],
        out_shape=[
            jax.ShapeDtypeStruct((nt, 1, tm), F32),
            jax.ShapeDtypeStruct((N_EXPERTS, T), F32),
            jax.ShapeDtypeStruct((nt, 1, LANES), jnp.int32),
        ],
        compiler_params=_cparams(("parallel",)),
        name="moe_router",
    )(x, mod, mod, w_router_t, b_router.reshape(N_EXPERTS, 1))

    vec = pl.BlockSpec((1, D_MODEL), lambda i, t: (0, 0))
    resident = dict(pipeline_mode=pl.Buffered(1))
    grid_spec = pltpu.PrefetchScalarGridSpec(
        num_scalar_prefetch=1,
        grid=(nt,),
        in_specs=[
            pl.BlockSpec((tm, D_MODEL), lambda i, t: (i, 0)),
            _mod_spec(geo, 4),
            _mod_spec(geo, 3),
            _mod_spec(geo, 5),
            pl.BlockSpec((None, 1, tm), lambda i, t: (i, 0, 0)),
            pl.BlockSpec((N_EXPERTS, tm), lambda i, t: (0, i)),
            pl.BlockSpec((None, N_GROUPS, EXPERTS_PER_GROUP, D_MODEL, 2 * D_EXPERT),
                         lambda i, t: (layer, 0, 0, 0, 0), **resident),
            pl.BlockSpec((None, N_GROUPS, EXPERTS_PER_GROUP * D_EXPERT, D_MODEL),
                         lambda i, t: (layer, 0, 0, 0), **resident),
            vec, vec,
        ],
        out_specs=pl.BlockSpec((tm, D_MODEL), lambda i, t: (i, 0)),
        scratch_shapes=[
            pltpu.VMEM((ns, D_MODEL), BF16),
            pltpu.VMEM((ns, N_EXPERTS), F32),
            pltpu.VMEM((ns, D_MODEL), BF16),
            pltpu.VMEM((ns, tm), BF16),
        ],
    )
    return pl.pallas_call(
        functools.partial(_expert_kernel, nb=nb),
        grid_spec=grid_spec,
        out_shape=jax.ShapeDtypeStruct((T, D_MODEL), F32),
        compiler_params=_cparams(("parallel",)),
        name="moe_experts_ln",
    )(tbl.reshape(nt, LANES), x, mod, mod, mod, posr, gates, w_gu, w_down,
      ln_g.reshape(1, D_MODEL), ln_b.reshape(1, D_MODEL))


def _trunk(geo, x, mods, s_ret, s_dn, s_conv, wts, ret_tl, ret_chunk, dn_tl, dn_chunk):
    (ln_g, ln_b, ret_w_in, ret_gn_g, ret_w_out, dn_w_main, dn_w_tail, dn_conv_w, dn_a_log, dn_dt_bias,
     dn_norm_g, dn_w_out, router_w, router_b, moe_w_gu, moe_w_down) = wts
    tables = _ret_tables(geo, ret_chunk)
    decode = s_dn is not None
    proj_tm = None if geo.per_token or geo.L % 1024 else 1024
    ret_stack, dn_stack, conv_tails = None, None, []
    for i in range(DEPTH):
        if geo.per_token:
            mod = jnp.repeat(mods[i], geo.L, axis=0)
        else:
            mod = mods[i].reshape(geo.B, 1, N_MOD * D_MODEL)
        j = i // 2
        if i % 2 == 0:
            if decode:
                p, _ = _inproj(geo, x, mod, ret_w_in[j], None, tn=1536)
            else:
                p, _ = _inproj(geo, x, mod, ret_w_in[j], None, tn=RET_VW, tm=proj_tm, split=True)
            o, ret_stack = _retention(geo, p, tables, ret_gn_g[j], s_ret, j, ret_stack, ret_tl, ret_chunk)
            w_out = ret_w_out[j]
        else:
            if decode:
                p, ab = _inproj(geo, x, mod, dn_w_main[j], dn_w_tail[j], tn=1536)
            else:
                p, ab = _inproj(geo, x, mod, dn_w_main[j], dn_w_tail[j], tn=2048,
                                conv_w=dn_conv_w[j], conv_cols=DN_CONV_CH, tm=proj_tm)
                conv_tails.append(_conv_tail(geo, x, mod, dn_w_main[j], DN_CONV_CH, tn=2048))
            o, dn_stack = _deltanet(geo, p, ab, dn_conv_w[j], dn_a_log[j], dn_dt_bias[j], dn_norm_g[j],
                                    s_dn, s_conv, j, dn_stack, dn_tl, dn_chunk)
            w_out = dn_w_out[j]
        x = _outproj_ln(geo, o, w_out, x, mod, ln_g[i, 0], ln_b[i, 0])
        x = _moe_ln(geo, x, mod, router_w, router_b, moe_w_gu, moe_w_down, i, ln_g[i, 1], ln_b[i, 1])
    if decode:
        new_dn, new_conv = dn_stack
    else:
        new_dn = dn_stack[0]
        new_conv = jnp.stack(conv_tails)[:, :, SUBLANES - (DN_CONV_W - 1):]
    return x, ret_stack[0], new_dn, new_conv


def _run(x_prompt, x_sample, state_ret, state_dn, state_conv, c_prompt, c_sample, w_ada, b_ada, ln_g, ln_b,
         ret_w_in, ret_gn_g, ret_w_out, dn_w_in, dn_conv_w, dn_a_log, dn_dt_bias, dn_norm_g, dn_w_out,
         router_w, router_b, moe_w_gu, moe_w_down, *, past_len):
    Bp, Lp, _ = x_prompt.shape
    Bs, Ls, _ = x_sample.shape
    Ls_pad = -(-Ls // SUBLANES) * SUBLANES
    tm_p = min(512, Lp)
    geo_p = _Geo(Bp, Lp, Lp, 0, tm_p, False, BF16)
    Ts = Bs * Ls_pad
    geo_s = _Geo(Bs, Ls_pad, Ls, past_len, min(512, Ts), True, F32)

    dn_w_tail = jnp.pad(dn_w_in[:, :, DN_MAIN:], ((0, 0), (0, 0), (0, LANES - 2 * DN_V_HEADS))).astype(BF16)
    w_gu_grp = moe_w_gu.astype(BF16).reshape(DEPTH, N_GROUPS, EXPERTS_PER_GROUP, D_MODEL, 2 * D_EXPERT)
    w_down_grp = moe_w_down.astype(BF16).reshape(DEPTH, N_GROUPS, EXPERTS_PER_GROUP * D_EXPERT, D_MODEL)
    wts = (ln_g, ln_b, ret_w_in.astype(BF16), ret_gn_g, ret_w_out.astype(BF16),
           dn_w_in[:, :, :DN_MAIN].astype(BF16), dn_w_tail, dn_conv_w, dn_a_log, dn_dt_bias, dn_norm_g,
           dn_w_out.astype(BF16), router_w.T, router_b, w_gu_grp, w_down_grp)

    mods = _ada(jnp.concatenate([c_prompt, c_sample], axis=0), w_ada, b_ada)

    xp = x_prompt.reshape(Bp * Lp, D_MODEL)
    yp, ret_p, dn_p, conv_p = _trunk(geo_p, xp, mods[:, :Bp], None, None, None, wts,
                                     ret_tl=min(512, Lp), ret_chunk=min(RET_CHUNK, Lp),
                                     dn_tl=min(256, Lp), dn_chunk=DN_CHUNK)
    xs = jnp.pad(x_sample, ((0, 0), (0, Ls_pad - Ls), (0, 0))).reshape(Ts, D_MODEL)
    ys, ret_s, dn_s, conv_s = _trunk(geo_s, xs, mods[:, Bp:], state_ret, state_dn, state_conv, wts,
                                     ret_tl=Ls_pad, ret_chunk=max(Ls_pad, RET_DECODE_CHUNK),
                                     dn_tl=Ls_pad, dn_chunk=max(Ls_pad, DN_DECODE_CHUNK))
    y_prompt = yp.reshape(Bp, Lp, D_MODEL)
    y_sample = ys.reshape(Bs, Ls_pad, D_MODEL)[:, :Ls]
    return (y_prompt, y_sample, ret_p, ret_s, dn_p, dn_s, conv_p, conv_s)


def kernel(x_prompt, x_sample, state_ret, state_dn, state_conv, c_prompt, c_sample, w_ada, b_ada, ln_g, ln_b,
           ret_w_in, ret_gn_g, ret_w_out, dn_w_in, dn_conv_w, dn_a_log, dn_dt_bias, dn_norm_g, dn_w_out,
           router_w, router_b, moe_w_gu, moe_w_down):
    return _run(x_prompt, x_sample, state_ret, state_dn, state_conv, c_prompt, c_sample, w_ada, b_ada,
                ln_g, ln_b, ret_w_in, ret_gn_g, ret_w_out, dn_w_in, dn_conv_w, dn_a_log, dn_dt_bias,
                dn_norm_g, dn_w_out, router_w, router_b, moe_w_gu, moe_w_down, past_len=PAST_LEN)
```
